```python
import math
import jax
import jax.numpy as jnp
from jax import lax
import numpy as np

D_MODEL = 1024
BATCH = 2
SEQ = 8192
DEPTH = 2

GRID_W = 64
CTX_LEN = 256
N_ADA = 6
N_BRANCH = 3
BRANCH_WIDTH = D_MODEL
POOL_WIDTH = BRANCH_WIDTH
POOL_GROUPS = 4
POOL_WINDOWS = (2, 4, 8, 16)
FOURIER_WIDTH = BRANCH_WIDTH
FOURIER_GROUPS = 4
SSM_INNER = BRANCH_WIDTH
SSM_HEAD_DIM = 64
SSM_HEADS = SSM_INNER // SSM_HEAD_DIM
SSM_GROUPS = 4
SSM_STATE = 128
SSM_CONV = 5
SSM_CHUNK = 128
CONV_DIM = SSM_INNER + 2 * SSM_GROUPS * SSM_STATE
OFF_FOURIER = POOL_WIDTH
OFF_Z = OFF_FOURIER + FOURIER_WIDTH
OFF_XBC = OFF_Z + SSM_INNER
OFF_DT = OFF_XBC + CONV_DIM
OFF_GATE = OFF_DT + 2 * SSM_HEADS
IN_COLS = OFF_GATE + N_BRANCH * D_MODEL
N_EXPERTS = 32
TOP_K = 4
D_FF = D_MODEL
SWIGLU_ALPHA = 1.702
SWIGLU_LIMIT = 7.0
MOE_BLOCK = 256
RMS_EPS = 1e-6

kernel_name = 'hybrid_pool_fourier_ssd_moe_dit'


def rms_norm(x, w):
    xf = x.astype(jnp.float32)
    y = xf * lax.rsqrt(jnp.mean(xf * xf, axis=-1, keepdims=True) + RMS_EPS)
    return y * w.astype(jnp.float32)


def modulate(h, shift, scale):
    return h * (1.0 + scale) + shift


def _window_bounds(n, w):
    t = jnp.arange(n)
    lo = jnp.clip(t - w // 2, 0, n)
    hi = jnp.clip(t + w - w // 2, 0, n)
    return lo, hi


def _pool1d(u, w):
    n = u.shape[1]
    s = jnp.pad(jnp.cumsum(u, axis=1), ((0, 0), (1, 0), (0, 0)))
    lo, hi = _window_bounds(n, w)
    return (s[:, hi] - s[:, lo]) / (hi - lo).astype(u.dtype)[None, :, None]


def _pool2d(u, w):
    rows, cols = u.shape[1], u.shape[2]
    s = jnp.pad(jnp.cumsum(jnp.cumsum(u, axis=1), axis=2), ((0, 0), (1, 0), (1, 0), (0, 0)))
    r0, r1 = _window_bounds(rows, w)
    c0, c1 = _window_bounds(cols, w)
    tot = (s[:, r1][:, :, c1] - s[:, r0][:, :, c1]
           - s[:, r1][:, :, c0] + s[:, r0][:, :, c0])
    cnt = ((r1 - r0)[:, None] * (c1 - c0)[None, :]).astype(u.dtype)
    return tot / cnt[None, :, :, None]


def pool_mixer(u, pool_w, pool_scale, grid):
    b, n, _ = u.shape
    gw = POOL_WIDTH // POOL_GROUPS
    ug = u.reshape(b, n, POOL_GROUPS, gw)
    outs = []
    for g, w in enumerate(POOL_WINDOWS):
        v = ug[:, :, g]
        if grid:
            rows = n // GRID_W
            p = _pool2d(v.reshape(b, rows, GRID_W, gw), w).reshape(b, n, gw)
        else:
            p = _pool1d(v, w)
        outs.append(p - v)
    d = jnp.stack(outs, axis=2)
    y = jnp.einsum('blgc,gcd->blgd', d, pool_w).reshape(b, n, POOL_WIDTH)
    return y * pool_scale


def fourier_mixer(u):
    b, n, _ = u.shape
    ug = u.astype(jnp.float32).reshape(b, n, FOURIER_GROUPS, FOURIER_WIDTH // FOURIER_GROUPS)
    f = jnp.fft.fft2(ug, axes=(1, 3), norm='ortho').real
    return f.reshape(b, n, FOURIER_WIDTH).astype(jnp.float32)


def depthwise_conv(u, w, bias):
    k = w.shape[0]
    out = lax.conv_general_dilated(
        u, w.astype(jnp.float32)[:, None, :], window_strides=(1,),
        padding=[(k // 2, k - 1 - k // 2)],
        dimension_numbers=('NWC', 'WIO', 'NWC'), feature_group_count=u.shape[-1])
    return out + bias.astype(jnp.float32)


def _segsum(a):
    t = a.shape[-1]
    cs = jnp.cumsum(a, axis=-1)
    diff = cs[..., :, None] - cs[..., None, :]
    return jnp.where(jnp.tril(jnp.ones((t, t), dtype=bool)), diff, -jnp.inf)


def ssd_scan(x, dt, a, bm, cm, init_state):
    b, n, h, p = x.shape
    q = SSM_CHUNK
    nc = n // q
    xd = (x * dt[..., None]).reshape(b, nc, q, h, p)
    ad = (dt * a).reshape(b, nc, q, h).transpose(0, 3, 1, 2)
    bc = bm.reshape(b, nc, q, h, -1)
    cc = cm.reshape(b, nc, q, h, -1)
    a_cum = jnp.cumsum(ad, axis=-1)
    lmat = jnp.exp(_segsum(ad))
    cb = jnp.einsum('bclhn,bcshn->bhcls', cc, bc)
    y_diag = jnp.einsum('bhcls,bcshp->bclhp', cb * lmat, xd)
    decay_states = jnp.exp(a_cum[..., -1:] - a_cum).transpose(0, 2, 3, 1)
    chunk_states = jnp.einsum('bclhn,bclhp->bchpn', bc, xd * decay_states[..., None])
    chunk_decay = jnp.exp(a_cum[..., -1])

    def step(s, inp):
        st, dec = inp
        return s * dec[..., None, None] + st, s

    final, starts = lax.scan(step, init_state,
                             (chunk_states.transpose(1, 0, 2, 3, 4), chunk_decay.transpose(2, 0, 1)))
    starts = starts.transpose(1, 0, 2, 3, 4)
    y_off = (jnp.einsum('bclhn,bchpn->bclhp', cc, starts)
             * jnp.exp(a_cum).transpose(0, 2, 3, 1)[..., None])
    return (y_diag + y_off).reshape(b, n, h, p), final


def _bidir_ssd(xs, bm, cm, dt, a, init_f, init_b):
    y_f, fin_f = ssd_scan(xs, dt[:, :, 0], a[0], bm, cm, init_f)
    rev = lambda t: jnp.flip(t, axis=1)
    y_b, fin_b = ssd_scan(rev(xs), rev(dt[:, :, 1]), a[1], rev(bm), rev(cm), init_b)
    return y_f + rev(y_b), fin_f, fin_b


def _ssm_inputs(xbc, dt_raw, conv_w, conv_b, dt_bias):
    b, n, _ = xbc.shape
    v = jax.nn.silu(depthwise_conv(xbc, conv_w, conv_b))
    gn = SSM_GROUPS * SSM_STATE
    rep = SSM_HEADS // SSM_GROUPS
    xs = v[..., :SSM_INNER].reshape(b, n, SSM_HEADS, SSM_HEAD_DIM)
    bm = jnp.repeat(v[..., SSM_INNER:SSM_INNER + gn].reshape(b, n, SSM_GROUPS, SSM_STATE), rep, axis=2)
    cm = jnp.repeat(v[..., SSM_INNER + gn:].reshape(b, n, SSM_GROUPS, SSM_STATE), rep, axis=2)
    dt = jax.nn.softplus(dt_raw.reshape(b, n, 2, SSM_HEADS) + dt_bias.astype(jnp.float32))
    return xs, bm, cm, dt


def _ssm_out(y, xs, z, d_skip, norm_w):
    b, n = y.shape[:2]
    y = (y + d_skip.astype(jnp.float32)[:, None] * xs).reshape(b, n, SSM_INNER)
    return rms_norm(y * jax.nn.silu(z), norm_w)


def merge_branches(ya, yb, yc, gate_raw, w_branch, w_out):
    b, n, _ = ya.shape
    ys = jnp.stack([ya, yb, yc], axis=2)
    proj = jnp.einsum('blkw,kwd->blkd', ys, w_branch)
    g = jax.nn.sigmoid(gate_raw.reshape(b, n, N_BRANCH, D_MODEL))
    return jnp.sum(g * proj, axis=2) @ w_out


def token_mixer(hl, hc, w_in, pool_w, pool_scale, conv_w, conv_b, dt_bias, a_log, d_skip,
                ssm_norm_w, w_branch, w_out, need_ctx):
    ul = (hl @ w_in).astype(jnp.float32)
    uc = (hc @ w_in).astype(jnp.float32)
    a = -jnp.exp(a_log.astype(jnp.float32))
    b = hl.shape[0]
    xs_c, bm_c, cm_c, dt_c = _ssm_inputs(uc[..., OFF_XBC:OFF_DT], uc[..., OFF_DT:OFF_GATE],
                                         conv_w, conv_b, dt_bias)
    xs_l, bm_l, cm_l, dt_l = _ssm_inputs(ul[..., OFF_XBC:OFF_DT], ul[..., OFF_DT:OFF_GATE],
                                         conv_w, conv_b, dt_bias)
    zero = jnp.zeros((b, SSM_HEADS, SSM_HEAD_DIM, SSM_STATE), jnp.float32)
    y_c, st_f, st_b = _bidir_ssd(xs_c, bm_c, cm_c, dt_c, a, zero, zero)
    y_l, _, _ = _bidir_ssd(xs_l, bm_l, cm_l, dt_l, a, st_f, st_b)

    def branches(u, y_ssm, xs, grid):
        ya = pool_mixer(u[..., :OFF_FOURIER], pool_w, pool_scale, grid)
        yb = fourier_mixer(u[..., OFF_FOURIER:OFF_Z])
        yc = _ssm_out(y_ssm, xs, u[..., OFF_Z:OFF_XBC], d_skip, ssm_norm_w)
        return merge_branches(ya, yb, yc, u[..., OFF_GATE:], w_branch, w_out)

    out_l = branches(ul, y_l, xs_l, True)
    out_c = branches(uc, y_c, xs_c, False) if need_ctx else None
    return out_l, out_c


def moe(h, router_w, router_b, w1, b1, w2, b2):
    t, d = h.shape
    logits = (h @ router_w + router_b).astype(jnp.float32)
    top_val, top_idx = lax.top_k(logits, TOP_K)
    wts = jax.nn.softmax(top_val, axis=-1)
    flat_e = top_idx.reshape(-1)
    flat_tok = jnp.repeat(jnp.arange(t, dtype=jnp.int32), TOP_K)
    flat_w = wts.reshape(-1)
    order = jnp.argsort(flat_e)
    se, stok, sw = flat_e[order], flat_tok[order], flat_w[order]
    counts = jnp.zeros((N_EXPERTS,), jnp.int32).at[flat_e].add(1)
    starts = jnp.cumsum(counts) - counts
    padded = (counts + MOE_BLOCK - 1) // MOE_BLOCK * MOE_BLOCK
    pends = jnp.cumsum(padded)
    pstarts = pends - padded
    dest = pstarts[se] + jnp.arange(t * TOP_K, dtype=jnp.int32) - starts[se]
    n_blocks = -(-(t * TOP_K) // MOE_BLOCK) + N_EXPERTS
    n_slots = n_blocks * MOE_BLOCK
    slot_tok = jnp.full((n_slots,), t, jnp.int32).at[dest].set(stok)
    slot_w = jnp.zeros((n_slots,), jnp.float32).at[dest].set(sw)
    block_e = jnp.clip(jnp.searchsorted(pends, jnp.arange(n_blocks) * MOE_BLOCK, side='right'),
                       0, N_EXPERTS - 1)
    hp = jnp.concatenate([h, jnp.zeros((1, d), h.dtype)], axis=0)
    xs = hp[slot_tok].reshape(n_blocks, MOE_BLOCK, d)

    def expert_block(args):
        xb, e = args
        gu = xb @ w1[e] + b1[e]
        gate = jnp.minimum(gu[:, :D_FF], SWIGLU_LIMIT)
        lin = jnp.clip(gu[:, D_FF:], -SWIGLU_LIMIT, SWIGLU_LIMIT)
        act = gate * jax.nn.sigmoid(SWIGLU_ALPHA * gate) * (lin + 1.0)
        return (act @ w2[e] + b2[e]).astype(jnp.float32)

    ys = lax.map(expert_block, (xs, block_e)).reshape(n_slots, d)
    out = jnp.zeros((t + 1, d), jnp.float32).at[slot_tok].add(ys * slot_w[:, None])
    return out[:t]


def setup_inputs(seed: int = 0) -> dict:
    key = jax.random.key(seed)
    ks = iter(jax.random.split(key, 32))
    nrm = lambda shape, scale: scale * jax.random.normal(next(ks), shape, jnp.float32)
    L, D = DEPTH, D_MODEL
    pg = POOL_WIDTH // POOL_GROUPS
    x = nrm((BATCH, SEQ, D), 1.0)
    c = nrm((BATCH, D), 1.0)
    ctx = nrm((BATCH, CTX_LEN, D), 1.0)
    c_ctx = nrm((D,), 1.0)
    w_ada = nrm((L, D, N_ADA * D), 0.5 * D ** -0.5)
    b_ada = nrm((L, N_ADA * D), 0.02)
    norm1_w = 1.0 + nrm((L, D), 0.02)
    norm2_w = 1.0 + nrm((L, D), 0.02)
    w_in = nrm((L, D, IN_COLS), D ** -0.5)
    pool_w = nrm((L, POOL_GROUPS, pg, pg), pg ** -0.5)
    pool_scale = 1.0 + nrm((L, POOL_WIDTH), 0.1)
    conv_w = nrm((L, SSM_CONV, CONV_DIM), SSM_CONV ** -0.5)
    conv_b = nrm((L, CONV_DIM), 0.02)
    dt0 = jnp.exp(jax.random.uniform(next(ks), (L, 2, SSM_HEADS), jnp.float32,
                                     minval=math.log(1e-3), maxval=math.log(1e-1)))
    dt_bias = dt0 + jnp.log(-jnp.expm1(-dt0))
    a_log = jnp.log(jax.random.uniform(next(ks), (L, 2, SSM_HEADS), jnp.float32, minval=1.0, maxval=16.0))
    d_skip = 1.0 + nrm((L, SSM_HEADS), 0.1)
    ssm_norm_w = 1.0 + nrm((L, SSM_INNER), 0.02)
    w_branch = nrm((L, N_BRANCH, BRANCH_WIDTH, D), BRANCH_WIDTH ** -0.5)
    w_out = nrm((L, D, D), D ** -0.5)
    router_w = nrm((L, D, N_EXPERTS), D ** -0.5)
    router_b = nrm((L, N_EXPERTS), 0.01)
    moe_w1 = nrm((L, N_EXPERTS, D, 2 * D_FF), D ** -0.5)
    moe_b1 = nrm((L, N_EXPERTS, 2 * D_FF), 0.02)
    moe_w2 = nrm((L, N_EXPERTS, D_FF, D), D_FF ** -0.5)
    moe_b2 = nrm((L, N_EXPERTS, D), 0.02)
    final_norm_w = 1.0 + nrm((D,), 0.02)
    return {'x': x, 'c': c, 'ctx': ctx, 'c_ctx': c_ctx, 'w_ada': w_ada, 'b_ada': b_ada,
            'norm1_w': norm1_w, 'norm2_w': norm2_w, 'w_in': w_in, 'pool_w': pool_w,
            'pool_scale': pool_scale, 'conv_w': conv_w, 'conv_b': conv_b, 'dt_bias': dt_bias,
            'a_log': a_log, 'd_skip': d_skip, 'ssm_norm_w': ssm_norm_w, 'w_branch': w_branch,
            'w_out': w_out, 'router_w': router_w, 'router_b': router_b, 'moe_w1': moe_w1,
            'moe_b1': moe_b1, 'moe_w2': moe_w2, 'moe_b2': moe_b2, 'final_norm_w': final_norm_w}


def reference(x, c, ctx, c_ctx, w_ada, b_ada, norm1_w, norm2_w, w_in, pool_w, pool_scale,
              conv_w, conv_b, dt_bias, a_log, d_skip, ssm_norm_w, w_branch, w_out,
              router_w, router_b, moe_w1, moe_b1, moe_w2, moe_b2, final_norm_w):
    b, n, d = x.shape
    lat, cx = x, ctx
    for i in range(DEPTH):
        last = i == DEPTH - 1
        mod_l = (jax.nn.silu(c.astype(jnp.float32)) @ w_ada[i] + b_ada[i]).reshape(b, N_ADA, 1, d)
        mod_c = (jax.nn.silu(c_ctx.astype(jnp.float32)) @ w_ada[i] + b_ada[i]).reshape(N_ADA, 1, 1, d)
        hl = modulate(rms_norm(lat, norm1_w[i]), mod_l[:, 0], mod_l[:, 1])
        hc = modulate(rms_norm(cx, norm1_w[i]), mod_c[0], mod_c[1])
        ol, oc = token_mixer(hl, hc, w_in[i], pool_w[i], pool_scale[i], conv_w[i], conv_b[i],
                             dt_bias[i], a_log[i], d_skip[i], ssm_norm_w[i], w_branch[i], w_out[i],
                             not last)
        lat = lat + (mod_l[:, 2] * ol).astype(lat.dtype)
        hl = modulate(rms_norm(lat, norm2_w[i]), mod_l[:, 3], mod_l[:, 4])
        if last:
            fl = moe(hl.reshape(-1, d), router_w[i], router_b[i], moe_w1[i], moe_b1[i],
                     moe_w2[i], moe_b2[i])
        else:
            cx = cx + (mod_c[2] * oc).astype(cx.dtype)
            hc = modulate(rms_norm(cx, norm2_w[i]), mod_c[3], mod_c[4])
            f = moe(jnp.concatenate([hl.reshape(-1, d), hc.reshape(-1, d)], axis=0), router_w[i],
                    router_b[i], moe_w1[i], moe_b1[i], moe_w2[i], moe_b2[i])
            fl = f[:b * n]
            cx = cx + (mod_c[5] * f[b * n:].reshape(b, -1, d)).astype(cx.dtype)
        lat = lat + (mod_l[:, 5] * fl.reshape(b, n, d)).astype(lat.dtype)
    return rms_norm(lat, final_norm_w).astype(x.dtype)
```

```python
import functools
import math

import numpy as np
import jax
import jax.numpy as jnp
from jax import lax
from jax.experimental import pallas as pl
from jax.experimental.pallas import tpu as pltpu

F32 = jnp.float32
BF16 = jnp.bfloat16
I32 = jnp.int32
HIGHEST = lax.Precision.HIGHEST

GRID_W = 64
N_ADA = 6
N_BRANCH = 3
MIX_GROUPS = 4
POOL_WINDOWS = (2, 4, 8, 16)
SSM_HEADS = 16
SSM_HEAD_DIM = 64
SSM_GROUPS = 4
SSM_STATE = 128
SSM_CONV = 5
SSM_CHUNK = 128
N_EXPERTS = 32
TOP_K = 4
SWIGLU_ALPHA = 1.702
SWIGLU_LIMIT = 7.0
RMS_EPS = 1e-6
MOE_BLOCK = 256

V7X_VMEM_BYTES = 64 * 2**20
V7X_LANES = 128
V7X_SUBLANES = 8
V7X_BF16_SUBLANES = 16
VMEM_COMPILER_ALLOWANCE = 6 * 2**20

TM_LAT = 1024
TM_ROW = 512
T_DMA = 256
FOURIER_N1 = 64
F1_BLOCK = 16
F2_BLOCK = 8


def _params(semantics, vmem_bytes):
    limit = min(int(vmem_bytes) + VMEM_COMPILER_ALLOWANCE, V7X_VMEM_BYTES - 2 * 2**20)
    return pltpu.CompilerParams(dimension_semantics=semantics, vmem_limit_bytes=limit)


def _nbytes(shape, dtype):
    return int(np.prod(shape)) * jnp.dtype(dtype).itemsize


def _silu(v):
    return v * jax.nn.sigmoid(v)


def _rms(v, w):
    return v * lax.rsqrt(jnp.mean(v * v, axis=-1, keepdims=True) + RMS_EPS) * w


def _ada_kernel(c_ref, w_ref, b_ref, o_ref):
    o_ref[...] = jnp.dot(_silu(c_ref[...]), w_ref[...], preferred_element_type=F32,
                         precision=HIGHEST) + b_ref[...]


def _ada(cvec, w_ada, b_ada):
    depth, d, nd = w_ada.shape
    return pl.pallas_call(
        _ada_kernel,
        out_shape=jax.ShapeDtypeStruct((depth, V7X_SUBLANES, nd), F32),
        grid=(depth, nd // d),
        in_specs=[pl.BlockSpec((V7X_SUBLANES, d), lambda l, j: (0, 0)),
                  pl.BlockSpec((None, d, d), lambda l, j: (l, 0, j)),
                  pl.BlockSpec((None, 1, d), lambda l, j: (l, 0, j))],
        out_specs=pl.BlockSpec((None, V7X_SUBLANES, d), lambda l, j: (l, 0, j)),
        compiler_params=_params(("parallel", "parallel"), 2 * _nbytes((d, d), F32)),
        name="ada",
    )(cvec, w_ada, b_ada.reshape(depth, 1, nd))


U_XBC, U_POOL, U_FOUR, U_Z, U_GATE = 0, 2, 3, 4, 5
U_BLOCKS = 8


def _inproj_kernel(x_ref, mod_ref, nw_ref, w_ref, wdt_ref, u_ref, dt_ref, h_scr):
    @pl.when(pl.program_id(1) == 0)
    def _():
        h = _rms(x_ref[...], nw_ref[...]) * (1.0 + mod_ref[1:2, :]) + mod_ref[0:1, :]
        hb = h.astype(BF16)
        h_scr[...] = hb
        dt_ref[...] = jnp.dot(hb, wdt_ref[...], preferred_element_type=F32)

    u_ref[...] = jnp.dot(h_scr[...], w_ref[...], preferred_element_type=F32).astype(BF16)


def _inproj(x, modv, nw, w_main, w_dt, *, tm, seg0, tiles_per_seg):
    t, d = x.shape
    vmem = 2 * (_nbytes((tm, d), F32) + 2 * _nbytes((tm, d), BF16) + _nbytes((d, d), BF16)
                + _nbytes((tm, V7X_LANES), F32) + _nbytes((d, V7X_LANES), BF16))
    return pl.pallas_call(
        _inproj_kernel,
        out_shape=(jax.ShapeDtypeStruct((t, U_BLOCKS * d), BF16),
                   jax.ShapeDtypeStruct((t, V7X_LANES), F32)),
        grid=(t // tm, U_BLOCKS),
        in_specs=[pl.BlockSpec((tm, d), lambda i, j: (i, 0)),
                  pl.BlockSpec((None, V7X_SUBLANES, d), lambda i, j: (seg0 + i // tiles_per_seg, 0, 0)),
                  pl.BlockSpec((1, d), lambda i, j: (0, 0)),
                  pl.BlockSpec((d, d), lambda i, j: (0, j)),
                  pl.BlockSpec((d, V7X_LANES), lambda i, j: (0, 0))],
        out_specs=(pl.BlockSpec((tm, d), lambda i, j: (i, j)),
                   pl.BlockSpec((tm, V7X_LANES), lambda i, j: (i, 0))),
        scratch_shapes=[pltpu.VMEM((tm, d), BF16)],
        compiler_params=_params(("parallel", "arbitrary"), vmem),
        name="inproj",
    )(x, modv, nw.reshape(1, d), w_main, w_dt)


CONV_HALO = V7X_BF16_SUBLANES
CONV_RB = 128
CONV_CB = 128


def _conv_kernel(cur_ref, prev_ref, next_ref, dtraw_ref, cw_ref, cb_ref, dtb_ref, alog_ref,
                 xt_ref, bc_ref, ct_ref, aux_ref, acum_ref, xp_scr, *, tc, tiles_per_seq, d_inner):
    p = pl.program_id(0) % tiles_per_seq
    xp_scr[0:CONV_HALO, :] = jnp.where(p > 0, prev_ref[...].astype(F32), 0.0)
    xp_scr[CONV_HALO:CONV_HALO + tc, :] = cur_ref[...].astype(F32)
    xp_scr[CONV_HALO + tc:, :] = jnp.where(p < tiles_per_seq - 1, next_ref[...].astype(F32), 0.0)
    n_bc = SSM_GROUPS * SSM_STATE
    for rb in range(tc // CONV_RB):
        rows = slice(rb * CONV_RB, (rb + 1) * CONV_RB)
        for cb in range(cur_ref.shape[1] // CONV_CB):
            c0 = cb * CONV_CB
            cols = slice(c0, c0 + CONV_CB)
            acc = jnp.broadcast_to(cb_ref[:, cols], (CONV_RB, CONV_CB))
            for k in range(SSM_CONV):
                r0 = CONV_HALO - SSM_CONV // 2 + k + rb * CONV_RB
                acc = acc + cw_ref[k:k + 1, cols] * xp_scr[r0:r0 + CONV_RB, cols]
            v = _silu(acc)
            if c0 < d_inner:
                xt_ref[cols, rows] = v.T.astype(BF16)
            elif c0 < d_inner + n_bc:
                bc_ref[rows, c0 - d_inner:c0 - d_inner + CONV_CB] = v.astype(BF16)
            else:
                bc_ref[rows, c0 - d_inner:c0 - d_inner + CONV_CB] = v.astype(BF16)
                cc = c0 - d_inner - n_bc
                ct_ref[cc:cc + CONV_CB, rows] = v.T.astype(BF16)

    dtr = dtraw_ref[...] + dtb_ref[...]
    dt = jnp.maximum(dtr, 0.0) + jnp.log1p(jnp.exp(-jnp.abs(dtr)))
    dta = dt * (-jnp.exp(alog_ref[...]))
    dt_t = dt.T
    dta_t = dta.T
    ki = lax.broadcasted_iota(I32, (tc, tc), 0)
    li = lax.broadcasted_iota(I32, (tc, tc), 1)
    same = (ki // SSM_CHUNK) == (li // SSM_CHUNK)
    one, zero = jnp.float32(1.0), jnp.float32(0.0)
    cumf = jnp.dot(dta_t, jnp.where(same & (ki <= li), one, zero), preferred_element_type=F32,
                   precision=HIGHEST)
    cumb = jnp.dot(dta_t, jnp.where(same & (ki >= li), one, zero), preferred_element_type=F32,
                   precision=HIGHEST)
    tot = jnp.dot(dta_t, jnp.where(same, one, zero), preferred_element_type=F32, precision=HIGHEST)
    nh = SSM_HEADS
    aux_ref[0:2 * nh, :] = dt_t[0:2 * nh, :]
    aux_ref[2 * nh:3 * nh, :] = cumf[0:nh, :]
    aux_ref[3 * nh:4 * nh, :] = cumb[nh:2 * nh, :]
    aux_ref[4 * nh:6 * nh, :] = tot[0:2 * nh, :]
    aux_ref[6 * nh:, :] = jnp.zeros((V7X_LANES - 6 * nh, tc), F32)
    hrow = lax.broadcasted_iota(I32, (V7X_LANES, tc), 0)
    acum_ref[...] = jnp.where(hrow < nh, cumf, cumb).T


def _conv(u, dt_raw, conv_w, conv_b, dt_bias, a_log, *, tc, seq_len, d_inner):
    t = u.shape[0]
    conv_dim = conv_w.shape[1]
    n_bc = SSM_GROUPS * SSM_STATE
    nt = t // tc
    hb = tc // CONV_HALO
    last_halo = t // CONV_HALO - 1
    cw = jnp.pad(conv_w.astype(F32), ((0, V7X_SUBLANES - SSM_CONV), (0, 0)))
    pad32 = lambda v: jnp.pad(v.astype(F32).reshape(1, -1), ((0, 0), (0, V7X_LANES - 2 * SSM_HEADS)))
    vmem = (2 * (_nbytes((tc, conv_dim), BF16) * 2 + _nbytes((tc, V7X_LANES), F32) * 3
                 + _nbytes((d_inner + 2 * n_bc, tc), BF16))
            + _nbytes((tc + 2 * CONV_HALO, conv_dim), F32) + 4 * _nbytes((tc, tc), F32))
    kern = functools.partial(_conv_kernel, tc=tc, tiles_per_seq=seq_len // tc, d_inner=d_inner)
    return pl.pallas_call(
        kern,
        out_shape=(jax.ShapeDtypeStruct((d_inner, t), BF16),
                   jax.ShapeDtypeStruct((t, 2 * n_bc), BF16),
                   jax.ShapeDtypeStruct((n_bc, t), BF16),
                   jax.ShapeDtypeStruct((V7X_LANES, t), F32),
                   jax.ShapeDtypeStruct((t, V7X_LANES), F32)),
        grid=(nt,),
        in_specs=[pl.BlockSpec((tc, conv_dim), lambda i: (i, U_XBC)),
                  pl.BlockSpec((CONV_HALO, conv_dim), lambda i: (jnp.maximum(i * hb - 1, 0), U_XBC)),
                  pl.BlockSpec((CONV_HALO, conv_dim), lambda i: (jnp.minimum((i + 1) * hb, last_halo), U_XBC)),
                  pl.BlockSpec((tc, V7X_LANES), lambda i: (i, 0)),
                  pl.BlockSpec((V7X_SUBLANES, conv_dim), lambda i: (0, 0)),
                  pl.BlockSpec((1, conv_dim), lambda i: (0, 0)),
                  pl.BlockSpec((1, V7X_LANES), lambda i: (0, 0)),
                  pl.BlockSpec((1, V7X_LANES), lambda i: (0, 0))],
        out_specs=(pl.BlockSpec((d_inner, tc), lambda i: (0, i)),
                   pl.BlockSpec((tc, 2 * n_bc), lambda i: (i, 0)),
                   pl.BlockSpec((n_bc, tc), lambda i: (0, i)),
                   pl.BlockSpec((V7X_LANES, tc), lambda i: (0, i)),
                   pl.BlockSpec((tc, V7X_LANES), lambda i: (i, 0))),
        scratch_shapes=[pltpu.VMEM((tc + 2 * CONV_HALO, conv_dim), F32)],
        compiler_params=_params(("parallel",), vmem),
        name="conv",
    )(u, u, u, dt_raw, cw, conv_b.astype(F32).reshape(1, -1), pad32(dt_bias), pad32(a_log))


def _ssd_direction(d, rev, xt_ref, bc_ref, ct_ref, aux_ref, acum_ref, st_scr, y_ref, dsk_ref):
    q = SSM_CHUNK
    hd = SSM_HEAD_DIM
    nh = SSM_HEADS
    si = lax.broadcasted_iota(I32, (q, q), 0)
    li = lax.broadcasted_iota(I32, (q, q), 1)
    mask = (li <= si) if rev else (li >= si)
    hpg = nh // SSM_GROUPS
    for g in range(SSM_GROUPS):
        b_g = bc_ref[:, g * SSM_STATE:(g + 1) * SSM_STATE]
        c_g = bc_ref[:, (SSM_GROUPS + g) * SSM_STATE:(SSM_GROUPS + g + 1) * SSM_STATE]
        ct_g = ct_ref[g * SSM_STATE:(g + 1) * SSM_STATE, :].astype(F32)
        cbt = lax.dot_general(b_g, c_g, (((1,), (1,)), ((), ())), preferred_element_type=F32)
        for hh in range(hpg):
            h = g * hpg + hh
            r = d * nh + h
            dt_row = aux_ref[r:r + 1, :]
            cum_row = aux_ref[2 * nh + r:2 * nh + r + 1, :]
            tot_row = aux_ref[4 * nh + r:4 * nh + r + 1, :]
            cum_col = acum_ref[:, r:r + 1]
            decay = jnp.exp(jnp.where(mask, cum_row - cum_col, -jnp.inf))
            w_top = (cbt * decay).astype(BF16)
            w_bot = (ct_g * jnp.exp(cum_row)).astype(BF16)
            w = jnp.concatenate([w_top, w_bot], axis=0)
            rows = slice(h * hd, (h + 1) * hd)
            x_t = xt_ref[rows, :].astype(F32)
            xd = x_t * dt_row
            s_old = st_scr[d, rows, :]
            lhs = jnp.concatenate([xd.astype(BF16), s_old.astype(BF16)], axis=1)
            y_t = jnp.dot(lhs, w, preferred_element_type=F32)
            if not rev:
                y_t = y_t + dsk_ref[h] * x_t
            y_ref[rows, :] = y_t.astype(y_ref.dtype)
            xdw = (xd * jnp.exp(tot_row - cum_row)).astype(BF16)
            st_scr[d, rows, :] = s_old * jnp.exp(tot_row) + jnp.dot(xdw, b_g, preferred_element_type=F32)


def _ssd_kernel(dsk_ref, xt_f, bc_f, ct_f, aux_f, ac_f, xt_b, bc_b, ct_b, aux_b, ac_b, init_ref,
                yf_ref, yb_ref, fin_ref, st_scr):
    c = pl.program_id(1)

    @pl.when(c == 0)
    def _():
        st_scr[...] = init_ref[...]

    _ssd_direction(0, False, xt_f, bc_f, ct_f, aux_f, ac_f, st_scr, yf_ref, dsk_ref)
    _ssd_direction(1, True, xt_b, bc_b, ct_b, aux_b, ac_b, st_scr, yb_ref, dsk_ref)

    @pl.when(c == pl.num_programs(1) - 1)
    def _():
        fin_ref[...] = st_scr[...]


def _ssd(conv_out, d_skip, init_state, *, batch, seq_len):
    xt, bc, ct, aux, acum = conv_out
    d_inner, t = xt.shape
    n_bc = ct.shape[0]
    q = SSM_CHUNK
    nc = seq_len // q
    fwd = lambda b, c: b * nc + c
    bwd = lambda b, c: b * nc + nc - 1 - c

    def specs(pos):
        return [pl.BlockSpec((d_inner, q), lambda b, c: (0, pos(b, c))),
                pl.BlockSpec((q, 2 * n_bc), lambda b, c: (pos(b, c), 0)),
                pl.BlockSpec((n_bc, q), lambda b, c: (0, pos(b, c))),
                pl.BlockSpec((V7X_LANES, q), lambda b, c: (0, pos(b, c))),
                pl.BlockSpec((q, V7X_LANES), lambda b, c: (pos(b, c), 0))]

    st_spec = pl.BlockSpec((None, 2, d_inner, SSM_STATE), lambda b, c: (b, 0, 0, 0))
    st_bytes = _nbytes((2, d_inner, SSM_STATE), F32)
    vmem = 5 * st_bytes + 4 * (_nbytes((d_inner, q), BF16) * 2 + _nbytes((q, 2 * n_bc), BF16)
                               + _nbytes((n_bc, q), BF16) + 2 * _nbytes((q, V7X_LANES), F32))
    return pl.pallas_call(
        _ssd_kernel,
        out_shape=(jax.ShapeDtypeStruct((d_inner, t), BF16),
                   jax.ShapeDtypeStruct((d_inner, t), BF16),
                   jax.ShapeDtypeStruct((batch, 2, d_inner, SSM_STATE), F32)),
        grid=(batch, nc),
        in_specs=[pl.BlockSpec(memory_space=pltpu.SMEM)] + specs(fwd) + specs(bwd) + [st_spec],
        out_specs=(pl.BlockSpec((d_inner, q), lambda b, c: (0, fwd(b, c))),
                   pl.BlockSpec((d_inner, q), lambda b, c: (0, bwd(b, c))),
                   st_spec),
        scratch_shapes=[pltpu.VMEM((2, d_inner, SSM_STATE), F32)],
        compiler_params=_params(("arbitrary", "arbitrary"), vmem),
        name="ssd",
    )(d_skip.astype(F32), xt, bc, ct, aux, acum, xt, bc, ct, aux, acum, init_state)


def _pool_body(slabs, cur_ref, mask_refs, icnt_ref, pw_ref, ps_ref, o_ref):
    gw = cur_ref.shape[1] // MIX_GROUPS
    for g in range(MIX_GROUPS):
        cols = slice(g * gw, (g + 1) * gw)
        xin = slabs(g, cols)
        s = jnp.dot(mask_refs[g][...], xin, preferred_element_type=F32)
        inv = icnt_ref[:, g * V7X_LANES:(g + 1) * V7X_LANES]
        p = s * jnp.concatenate([inv] * (gw // V7X_LANES), axis=1)
        dlt = p - cur_ref[:, cols].astype(F32)
        y = jnp.dot(dlt.astype(BF16), pw_ref[g], preferred_element_type=F32) * ps_ref[:, cols]
        o_ref[:, cols] = y.astype(BF16)


def _pool2d_kernel(prev_ref, cur_ref, next_ref, m0, m1, m2, m3, icnt_ref, pw_ref, ps_ref, o_ref):
    tile = cur_ref.shape[0]

    def slabs(g, cols):
        w = POOL_WINDOWS[g]
        up, dn = w // 2, w - w // 2 - 1
        parts = [prev_ref[tile - GRID_W * up:, cols], cur_ref[:, cols]]
        if dn:
            parts.append(next_ref[:GRID_W * dn, cols])
        return jnp.concatenate(parts, axis=0)

    _pool_body(slabs, cur_ref, (m0, m1, m2, m3), icnt_ref, pw_ref, ps_ref, o_ref)


def _pool1d_kernel(cur_ref, m0, m1, m2, m3, icnt_ref, pw_ref, ps_ref, o_ref):
    _pool_body(lambda g, cols: cur_ref[:, cols], cur_ref, (m0, m1, m2, m3), icnt_ref, pw_ref, ps_ref, o_ref)


def _window_ok(out_pos, in_pos, w):
    lo = out_pos[:, None] - w // 2
    return (in_pos[None, :] >= lo) & (in_pos[None, :] < lo + w)


def _pool_tables_2d(tile):
    rows = tile // GRID_W
    t = jnp.arange(tile)
    masks, invs = [], []
    for w in POOL_WINDOWS:
        up, dn = w // 2, w - w // 2 - 1
        u = jnp.arange(tile + GRID_W * (up + dn))
        r_in, c_in = u // GRID_W - up, u % GRID_W
        base = _window_ok(t // GRID_W, r_in, w) & _window_ok(t % GRID_W, c_in, w)
        var = jnp.stack([base & (r_in >= 0)[None, :], base, base & (r_in < rows)[None, :]])
        masks.append(var.astype(BF16))
        invs.append(1.0 / jnp.sum(var.astype(F32), axis=2))
    inv = jnp.stack(invs, axis=1)
    inv = jnp.broadcast_to(inv[:, :, :, None], inv.shape + (V7X_LANES,))
    inv = jnp.transpose(inv, (0, 2, 1, 3)).reshape(3, tile, MIX_GROUPS * V7X_LANES)
    return masks, inv


def _pool_tables_1d(n):
    t = jnp.arange(n)
    masks, invs = [], []
    for w in POOL_WINDOWS:
        m = _window_ok(t, t, w)
        masks.append(m.astype(BF16))
        invs.append(1.0 / jnp.sum(m.astype(F32), axis=1))
    inv = jnp.stack(invs, axis=0)
    inv = jnp.broadcast_to(inv[:, :, None], inv.shape + (V7X_LANES,))
    return masks, jnp.transpose(inv, (1, 0, 2)).reshape(n, MIX_GROUPS * V7X_LANES)


def _pool2d(u, pool_w, pool_scale, *, seq_len):
    t = u.shape[0]
    d = pool_scale.shape[0]
    tile = TM_ROW
    nt, per_img = t // tile, seq_len // tile
    masks, inv = _pool_tables_2d(tile)

    def variant(i):
        p = i % per_img
        return jnp.where(p == 0, 0, jnp.where(p == per_img - 1, 2, 1))

    gw = d // MIX_GROUPS
    vmem = 2 * (4 * _nbytes((tile, d), BF16) + sum(_nbytes(m.shape[1:], BF16) for m in masks)
                + _nbytes(inv.shape[1:], F32) + _nbytes((MIX_GROUPS, gw, gw), BF16))
    return pl.pallas_call(
        _pool2d_kernel,
        out_shape=jax.ShapeDtypeStruct((t, d), BF16),
        grid=(nt,),
        in_specs=[pl.BlockSpec((tile, d), lambda i: (jnp.maximum(i - 1, 0), U_POOL)),
                  pl.BlockSpec((tile, d), lambda i: (i, U_POOL)),
                  pl.BlockSpec((tile, d), lambda i: (jnp.minimum(i + 1, nt - 1), U_POOL))]
                 + [pl.BlockSpec((None,) + m.shape[1:], lambda i: (variant(i), 0, 0)) for m in masks]
                 + [pl.BlockSpec((None,) + inv.shape[1:], lambda i: (variant(i), 0, 0)),
                    pl.BlockSpec((MIX_GROUPS, gw, gw), lambda i: (0, 0, 0)),
                    pl.BlockSpec((1, d), lambda i: (0, 0))],
        out_specs=pl.BlockSpec((tile, d), lambda i: (i, 0)),
        compiler_params=_params(("parallel",), vmem),
        name="pool2d",
    )(u, u, u, *masks, inv, pool_w.astype(BF16), pool_scale.astype(F32).reshape(1, d))


def _pool1d(u, pool_w, pool_scale, *, seq_len):
    t = u.shape[0]
    d = pool_scale.shape[0]
    masks, inv = _pool_tables_1d(seq_len)
    gw = d // MIX_GROUPS
    vmem = 2 * (2 * _nbytes((seq_len, d), BF16) + 4 * _nbytes((seq_len, seq_len), BF16)
                + _nbytes(inv.shape, F32) + _nbytes((MIX_GROUPS, gw, gw), BF16))
    return pl.pallas_call(
        _pool1d_kernel,
        out_shape=jax.ShapeDtypeStruct((t, d), BF16),
        grid=(t // seq_len,),
        in_specs=[pl.BlockSpec((seq_len, d), lambda i: (i, U_POOL))]
                 + [pl.BlockSpec(m.shape, lambda i: (0, 0)) for m in masks]
                 + [pl.BlockSpec(inv.shape, lambda i: (0, 0)),
                    pl.BlockSpec((MIX_GROUPS, gw, gw), lambda i: (0, 0, 0)),
                    pl.BlockSpec((1, d), lambda i: (0, 0))],
        out_specs=pl.BlockSpec((seq_len, d), lambda i: (i, 0)),
        compiler_params=_params(("parallel",), vmem),
        name="pool1d",
    )(u, *masks, inv, pool_w.astype(BF16), pool_scale.astype(F32).reshape(1, d))


def _dft_tables(n_pos, gw):
    n1, n2 = FOURIER_N1, n_pos // FOURIER_N1
    c = np.arange(gw)
    ang = 2.0 * np.pi * ((c[:, None] * c[None, :]) % gw) / gw
    cs = np.concatenate([np.cos(ang), -np.sin(ang)], axis=1)
    a = np.arange(n1)
    ang1 = 2.0 * np.pi * ((a[:, None] * a[None, :]) % n1) / n1
    wc, ws = np.cos(ang1), np.sin(ang1)
    w1 = np.block([[wc, ws], [-ws, wc]])
    k = a[:, None, None] + n1 * np.arange(n2)[None, :, None]
    ang2 = 2.0 * np.pi * ((k * np.arange(n2)[None, None, :]) % n_pos) / n_pos
    m2 = np.concatenate([np.cos(ang2), np.sin(ang2)], axis=2)
    to_bf = lambda v: jnp.asarray(v, dtype=F32).astype(BF16)
    return to_bf(cs), to_bf(w1), to_bf(m2)


def _f1_kernel(u_ref, cs_ref, w1_ref, o_ref, z_scr):
    nb, n1, d = u_ref.shape
    gw = d // MIX_GROUPS
    u = u_ref[...].reshape(nb * n1, d)
    for g in range(MIX_GROUPS):
        z = jnp.dot(u[:, g * gw:(g + 1) * gw], cs_ref[...], preferred_element_type=F32)
        z_scr[:, g * gw:(g + 1) * gw] = z[:, :gw].astype(BF16)
        z_scr[:, d + g * gw:d + (g + 1) * gw] = z[:, gw:].astype(BF16)
    for j in range(nb):
        zz = jnp.concatenate([z_scr[j * n1:(j + 1) * n1, :d], z_scr[j * n1:(j + 1) * n1, d:]], axis=0)
        o_ref[j] = jnp.dot(w1_ref[...], zz, preferred_element_type=F32).astype(BF16)


def _f2_kernel(a_ref, m2_ref, o_ref, *, scale):
    for j in range(a_ref.shape[0]):
        o_ref[j] = (jnp.dot(m2_ref[j], a_ref[j], preferred_element_type=F32) * scale).astype(BF16)


def _fourier_lat(u, *, batch, seq_len, d):
    n1, n2 = FOURIER_N1, seq_len // FOURIER_N1
    gw = d // MIX_GROUPS
    cs, w1, m2 = _dft_tables(seq_len, gw)
    ut = u[:, U_FOUR * d:(U_FOUR + 1) * d].reshape(batch, n1, n2, d).transpose(0, 2, 1, 3)
    nb = F1_BLOCK
    vmem1 = 2 * (_nbytes((nb, n1, d), BF16) + _nbytes((nb, 2 * n1, d), BF16)) + _nbytes((nb * n1, 2 * d), BF16) \
        + 2 * _nbytes((nb * n1, 2 * gw), F32)
    a = pl.pallas_call(
        _f1_kernel,
        out_shape=jax.ShapeDtypeStruct((batch, n2, 2 * n1, d), BF16),
        grid=(batch, n2 // nb),
        in_specs=[pl.BlockSpec((None, nb, n1, d), lambda b, j: (b, j, 0, 0)),
                  pl.BlockSpec(cs.shape, lambda b, j: (0, 0)),
                  pl.BlockSpec(w1.shape, lambda b, j: (0, 0))],
        out_specs=pl.BlockSpec((None, nb, 2 * n1, d), lambda b, j: (b, j, 0, 0)),
        scratch_shapes=[pltpu.VMEM((nb * n1, 2 * d), BF16)],
        compiler_params=_params(("parallel", "parallel"), vmem1),
        name="fourier_stage1",
    )(ut, cs, w1)
    at = a.reshape(batch, n2, 2, n1, d).transpose(0, 3, 2, 1, 4).reshape(batch, n1, 2 * n2, d)
    kb = F2_BLOCK
    vmem2 = 2 * (_nbytes((kb, 2 * n2, d), BF16) + _nbytes((kb, n2, 2 * n2), BF16) + _nbytes((kb, n2, d), BF16))
    x2 = pl.pallas_call(
        functools.partial(_f2_kernel, scale=1.0 / math.sqrt(seq_len * gw)),
        out_shape=jax.ShapeDtypeStruct((batch, n1, n2, d), BF16),
        grid=(batch, n1 // kb),
        in_specs=[pl.BlockSpec((None, kb, 2 * n2, d), lambda b, j: (b, j, 0, 0)),
                  pl.BlockSpec((kb, n2, 2 * n2), lambda b, j: (j, 0, 0))],
        out_specs=pl.BlockSpec((None, kb, n2, d), lambda b, j: (b, j, 0, 0)),
        compiler_params=_params(("parallel", "parallel"), vmem2),
        name="fourier_stage2",
    )(at, m2)
    return x2.transpose(0, 2, 1, 3).reshape(batch * seq_len, d)


def _fourier_ctx_kernel(u_ref, cs_ref, cn_ref, o_ref, *, scale):
    d = u_ref.shape[1]
    gw = d // MIX_GROUPS
    zr, zi = [], []
    for g in range(MIX_GROUPS):
        z = jnp.dot(u_ref[:, g * gw:(g + 1) * gw], cs_ref[...], preferred_element_type=F32)
        zr.append(z[:, :gw].astype(BF16))
        zi.append(z[:, gw:].astype(BF16))
    zz = jnp.concatenate([jnp.concatenate(zr, axis=1), jnp.concatenate(zi, axis=1)], axis=0)
    o_ref[...] = (jnp.dot(cn_ref[...], zz, preferred_element_type=F32) * scale).astype(BF16)


def _fourier_ctx(u, *, seq_len, d):
    t = u.shape[0]
    gw = d // MIX_GROUPS
    cs, _, _ = _dft_tables(FOURIER_N1 * 2, gw)
    n = np.arange(seq_len)
    ang = 2.0 * np.pi * ((n[:, None] * n[None, :]) % seq_len) / seq_len
    cn = jnp.asarray(np.concatenate([np.cos(ang), np.sin(ang)], axis=1), dtype=F32).astype(BF16)
    vmem = 2 * (2 * _nbytes((seq_len, d), BF16) + _nbytes(cs.shape, BF16) + _nbytes(cn.shape, BF16)) \
        + 4 * _nbytes((seq_len, d), F32)
    return pl.pallas_call(
        functools.partial(_fourier_ctx_kernel, scale=1.0 / math.sqrt(seq_len * gw)),
        out_shape=jax.ShapeDtypeStruct((t, d), BF16),
        grid=(t // seq_len,),
        in_specs=[pl.BlockSpec((seq_len, d), lambda i: (i, U_FOUR)),
                  pl.BlockSpec(cs.shape, lambda i: (0, 0)),
                  pl.BlockSpec(cn.shape, lambda i: (0, 0))],
        out_specs=pl.BlockSpec((seq_len, d), lambda i: (i, 0)),
        compiler_params=_params(("parallel",), vmem),
        name="fourier_ctx",
    )(u, cs, cn)


def _merge_kernel(ya_ref, yb_ref, yf_ref, ybw_ref, z_ref, g0_ref, g1_ref, g2_ref, lat_ref, mod_ref,
                  snw_ref, wb_ref, wo_ref, n2w_ref, rw_ref, rb_ref, tri_ref, ones_ref, cnt0_ref,
                  lat_o, h2_o, idx_o, rank_o, wts_o, cnt_o, run_scr):
    tm = lat_ref.shape[0]

    @pl.when(pl.program_id(0) == 0)
    def _():
        run_scr[...] = cnt0_ref[...]

    y_ssm = (yf_ref[...].astype(F32) + ybw_ref[...].astype(F32)).T
    yc = _rms(y_ssm * _silu(z_ref[...].astype(F32)), snw_ref[...])
    merged = jnp.zeros(lat_ref.shape, F32)
    for k, (y, g_ref) in enumerate(((ya_ref[...], g0_ref), (yb_ref[...], g1_ref), (yc.astype(BF16), g2_ref))):
        proj = jnp.dot(y, wb_ref[k], preferred_element_type=F32)
        merged = merged + jax.nn.sigmoid(g_ref[...].astype(F32)) * proj
    ol = jnp.dot(merged.astype(BF16), wo_ref[...], preferred_element_type=F32)
    lat = lat_ref[...] + mod_ref[2:3, :] * ol
    lat_o[...] = lat
    h2 = _rms(lat, n2w_ref[...]) * (1.0 + mod_ref[4:5, :]) + mod_ref[3:4, :]
    h2_o[...] = h2

    logits = jnp.dot(h2, rw_ref[...], preferred_element_type=F32, precision=HIGHEST)
    lt = logits.T + jnp.concatenate([rb_ref[...]] * (tm // V7X_LANES), axis=1)
    ei = lax.broadcasted_iota(I32, lt.shape, 0).astype(F32)
    vals, idxs = [], []
    for _ in range(TOP_K):
        m = jnp.max(lt, axis=0, keepdims=True)
        sel = jnp.min(jnp.where(lt == m, ei, float(V7X_LANES)), axis=0, keepdims=True)
        vals.append(m)
        idxs.append(sel)
        lt = jnp.where(ei == sel, -jnp.inf, lt)
    exps = [jnp.exp(v - vals[0]) for v in vals]
    den = exps[0] + exps[1] + exps[2] + exps[3]
    onehots = [ei == s for s in idxs]
    chosen = jnp.zeros(lt.shape, F32)
    for oh in onehots:
        chosen = chosen + jnp.where(oh, 1.0, 0.0)
    chosen_b = chosen.astype(BF16)
    before = jnp.dot(chosen_b, tri_ref[...], preferred_element_type=F32) \
        + jnp.concatenate([run_scr[...]] * (tm // V7X_LANES), axis=1)
    for k in range(TOP_K):
        idx_o[k:k + 1, :] = idxs[k].astype(I32)
        rank_o[k:k + 1, :] = jnp.sum(jnp.where(onehots[k], before, 0.0), axis=0, keepdims=True).astype(I32)
        wts_o[k:k + 1, :] = exps[k] / den
    pad = V7X_SUBLANES - TOP_K
    idx_o[TOP_K:, :] = jnp.zeros((pad, tm), I32)
    rank_o[TOP_K:, :] = jnp.zeros((pad, tm), I32)
    wts_o[TOP_K:, :] = jnp.zeros((pad, tm), F32)
    run_scr[...] = run_scr[...] + jnp.dot(chosen_b, ones_ref[...], preferred_element_type=F32)
    cnt_o[...] = run_scr[...]


def _merge(ya, yb, yf_t, yb_t, u, lat, modv, ssm_norm_w, wb, wo, norm2_w, rw, rb, cnt0, *,
           tm, seg0, tiles_per_seg):
    t, d = lat.shape
    ne = V7X_LANES
    tri = (jnp.arange(tm)[:, None] < jnp.arange(tm)[None, :]).astype(BF16)
    ones = jnp.ones((tm, ne), BF16)
    row = lambda c: pl.BlockSpec((tm, d), lambda i: (i, c))
    col = pl.BlockSpec((d, tm), lambda i: (0, i))
    const = lambda shape: pl.BlockSpec(shape, lambda i: (0,) * len(shape), pipeline_mode=pl.Buffered(1))
    tok = pl.BlockSpec((V7X_SUBLANES, tm), lambda i: (0, i))
    vmem = (2 * (4 * _nbytes((tm, d), BF16) + 2 * _nbytes((d, tm), BF16) + 2 * _nbytes((tm, d), BF16)
                 + 3 * _nbytes((tm, d), F32))
            + 4 * _nbytes((d, d), BF16) + _nbytes((d, ne), F32) + _nbytes((tm, tm), BF16)
            + 8 * _nbytes((tm, d), F32))
    return pl.pallas_call(
        _merge_kernel,
        out_shape=(jax.ShapeDtypeStruct((t, d), F32), jax.ShapeDtypeStruct((t, d), F32),
                   jax.ShapeDtypeStruct((V7X_SUBLANES, t), I32), jax.ShapeDtypeStruct((V7X_SUBLANES, t), I32),
                   jax.ShapeDtypeStruct((V7X_SUBLANES, t), F32), jax.ShapeDtypeStruct((ne, ne), F32)),
        grid=(t // tm,),
        in_specs=[row(0), row(0), col, col, row(U_Z), row(U_GATE), row(U_GATE + 1), row(U_GATE + 2),
                  row(0),
                  pl.BlockSpec((None, V7X_SUBLANES, d), lambda i: (seg0 + i // tiles_per_seg, 0, 0)),
                  const((1, d)), const((N_BRANCH, d, d)), const((d, d)), const((1, d)),
                  const((d, ne)), const((ne, ne)), const((tm, tm)), const((tm, ne)), const((ne, ne))],
        out_specs=(row(0), row(0), tok, tok, tok, pl.BlockSpec((ne, ne), lambda i: (0, 0))),
        scratch_shapes=[pltpu.VMEM((ne, ne), F32)],
        compiler_params=_params(("arbitrary",), vmem),
        name="merge",
    )(ya, yb, yf_t, yb_t, u, u, u, u, lat, modv, ssm_norm_w.astype(F32).reshape(1, d), wb, wo,
      norm2_w.astype(F32).reshape(1, d), rw, rb, tri, ones, cnt0)


def _dispatch_kernel(dest_ref, h_ref, *rest):
    xs_ref, sem = rest[-2], rest[-1]
    n = h_ref.shape[0]

    def row_copy(t, dst):
        return pltpu.make_async_copy(h_ref.at[pl.ds(t, 1), :], xs_ref.at[pl.ds(dst, 1), :], sem)

    def start(t, carry):
        for k in range(TOP_K):
            row_copy(t, dest_ref[0, TOP_K * t + k]).start()
        return carry

    def wait(t, carry):
        for k in range(TOP_K):
            row_copy(t, dest_ref[0, TOP_K * t + k]).wait()
        return carry

    lax.fori_loop(0, n, start, 0)
    lax.fori_loop(0, n, wait, 0)


def _dispatch(h2, dest_tiles, n_slots, xs_prev=None):
    t, d = h2.shape
    td = T_DMA
    in_specs = [pl.BlockSpec((None, 1, TOP_K * td), lambda i: (i, 0, 0), memory_space=pltpu.SMEM),
                pl.BlockSpec((td, d), lambda i: (i, 0))]
    args = [dest_tiles, h2]
    aliases = {}
    if xs_prev is not None:
        in_specs.append(pl.BlockSpec(memory_space=pl.ANY))
        args.append(xs_prev)
        aliases = {2: 0}
    return pl.pallas_call(
        _dispatch_kernel,
        out_shape=jax.ShapeDtypeStruct((n_slots, d), F32),
        grid=(t // td,),
        in_specs=in_specs,
        out_specs=pl.BlockSpec(memory_space=pl.ANY),
        scratch_shapes=[pltpu.SemaphoreType.DMA(())],
        input_output_aliases=aliases,
        compiler_params=_params(("arbitrary",), 2 * _nbytes((td, d), F32)),
        name="dispatch",
    )(*args)


EXPERT_NB = 256


def _expert_kernel(be_ref, nv_ref, xs_ref, w1_ref, b1_ref, w2_ref, b2_ref, ys_ref, w1_scr, w2_scr, act_scr):
    i = pl.program_id(0)
    d = xs_ref.shape[1]
    dff = w2_ref.shape[0]
    changed = (i == 0) | (be_ref[i] != be_ref[jnp.maximum(i - 1, 0)])

    @pl.when(changed)
    def _():
        rows = 128

        def cast1(r, carry):
            sl = pl.ds(pl.multiple_of(r * rows, rows), rows)
            w1_scr[sl, :] = w1_ref[sl, :].astype(BF16)
            return carry

        def cast2(r, carry):
            sl = pl.ds(pl.multiple_of(r * rows, rows), rows)
            w2_scr[sl, :] = w2_ref[sl, :].astype(BF16)
            return carry

        lax.fori_loop(0, d // rows, cast1, 0)
        lax.fori_loop(0, dff // rows, cast2, 0)

    nv = nv_ref[i]

    @pl.when(nv == 0)
    def _():
        ys_ref[...] = jnp.zeros(ys_ref.shape, F32)

    @pl.when(nv > 0)
    def _():
        ridx = lax.broadcasted_iota(I32, xs_ref.shape, 0)
        x = jnp.where(ridx < nv, xs_ref[...], 0.0).astype(BF16)
        for c in range(dff // EXPERT_NB):
            cg = slice(c * EXPERT_NB, (c + 1) * EXPERT_NB)
            cl = slice(dff + c * EXPERT_NB, dff + (c + 1) * EXPERT_NB)
            gate = jnp.dot(x, w1_scr[:, cg], preferred_element_type=F32) + b1_ref[:, cg]
            lin = jnp.dot(x, w1_scr[:, cl], preferred_element_type=F32) + b1_ref[:, cl]
            gate = jnp.minimum(gate, SWIGLU_LIMIT)
            lin = jnp.clip(lin, -SWIGLU_LIMIT, SWIGLU_LIMIT)
            act = gate * jax.nn.sigmoid(SWIGLU_ALPHA * gate) * (lin + 1.0)
            act_scr[:, cg] = act.astype(BF16)
        ys_ref[...] = jnp.dot(act_scr[...], w2_scr[...], preferred_element_type=F32) + b2_ref[...]


def _experts(xs, block_e, nvalid, w1, b1, w2, b2):
    n_slots, d = xs.shape
    ne, _, two_dff = w1.shape
    dff = two_dff // 2
    n_blocks = n_slots // MOE_BLOCK
    grid_spec = pltpu.PrefetchScalarGridSpec(
        num_scalar_prefetch=2,
        grid=(n_blocks,),
        in_specs=[pl.BlockSpec((MOE_BLOCK, d), lambda i, be, nv: (i, 0)),
                  pl.BlockSpec((None, d, two_dff), lambda i, be, nv: (be[i], 0, 0)),
                  pl.BlockSpec((None, 1, two_dff), lambda i, be, nv: (be[i], 0, 0)),
                  pl.BlockSpec((None, dff, d), lambda i, be, nv: (be[i], 0, 0)),
                  pl.BlockSpec((None, 1, d), lambda i, be, nv: (be[i], 0, 0))],
        out_specs=pl.BlockSpec((MOE_BLOCK, d), lambda i, be, nv: (i, 0)),
        scratch_shapes=[pltpu.VMEM((d, two_dff), BF16), pltpu.VMEM((dff, d), BF16),
                        pltpu.VMEM((MOE_BLOCK, dff), BF16)],
    )
    vmem = (2 * (_nbytes((d, two_dff), F32) + _nbytes((dff, d), F32) + 2 * _nbytes((MOE_BLOCK, d), F32))
            + _nbytes((d, two_dff), BF16) + _nbytes((dff, d), BF16) + 4 * _nbytes((MOE_BLOCK, dff), F32))
    return pl.pallas_call(
        _expert_kernel,
        out_shape=jax.ShapeDtypeStruct((n_slots, d), F32),
        grid_spec=grid_spec,
        compiler_params=_params(("arbitrary",), vmem),
        name="experts",
    )(block_e, nvalid, xs, w1, b1.reshape(ne, 1, two_dff), w2, b2.reshape(ne, 1, d))


def _combine_kernel(dest_ref, lat_ref, w_ref, mod_ref, fnw_ref, ys_ref, o_ref, buf, sem, *, final_norm):
    n = lat_ref.shape[0]

    def row_copy(t, k, src):
        return pltpu.make_async_copy(ys_ref.at[pl.ds(src, 1), :], buf.at[k, pl.ds(t, 1), :], sem)

    def start(t, carry):
        for k in range(TOP_K):
            row_copy(t, k, dest_ref[0, TOP_K * t + k]).start()
        return carry

    def wait(t, carry):
        for k in range(TOP_K):
            row_copy(t, k, dest_ref[0, TOP_K * t + k]).wait()
        return carry

    lax.fori_loop(0, n, start, 0)
    lax.fori_loop(0, n, wait, 0)
    fl = jnp.zeros(lat_ref.shape, F32)
    for k in range(TOP_K):
        fl = fl + w_ref[:, k:k + 1] * buf[k]
    out = lat_ref[...] + mod_ref[5:6, :] * fl
    if final_norm:
        out = _rms(out, fnw_ref[...])
    o_ref[...] = out


def _combine(lat, dest_tiles, wts, modv, fnw, ys, *, seg0, tiles_per_seg, final_norm):
    t, d = lat.shape
    td = T_DMA
    vmem = 2 * (2 * _nbytes((td, d), F32) + _nbytes((td, V7X_LANES), F32)) + _nbytes((TOP_K, td, d), F32) \
        + 4 * _nbytes((td, d), F32)
    return pl.pallas_call(
        functools.partial(_combine_kernel, final_norm=final_norm),
        out_shape=jax.ShapeDtypeStruct((t, d), F32),
        grid=(t // td,),
        in_specs=[pl.BlockSpec((None, 1, TOP_K * td), lambda i: (i, 0, 0), memory_space=pltpu.SMEM),
                  pl.BlockSpec((td, d), lambda i: (i, 0)),
                  pl.BlockSpec((td, V7X_SUBLANES), lambda i: (i, 0)),
                  pl.BlockSpec((None, V7X_SUBLANES, d), lambda i: (seg0 + i // tiles_per_seg, 0, 0)),
                  pl.BlockSpec((1, d), lambda i: (0, 0)),
                  pl.BlockSpec(memory_space=pl.ANY)],
        out_specs=pl.BlockSpec((td, d), lambda i: (i, 0)),
        scratch_shapes=[pltpu.VMEM((TOP_K, td, d), F32), pltpu.SemaphoreType.DMA(())],
        compiler_params=_params(("arbitrary",), vmem),
        name="combine",
    )(dest_tiles, lat, wts, modv, fnw.astype(F32).reshape(1, d), ys)


def _routing_tables(counts, n_blocks):
    padded = (counts + MOE_BLOCK - 1) // MOE_BLOCK * MOE_BLOCK
    pends = jnp.cumsum(padded)
    pstarts = pends - padded
    bstart = jnp.arange(n_blocks, dtype=I32) * MOE_BLOCK
    block_e = jnp.minimum(jnp.sum((pends[None, :] <= bstart[:, None]).astype(I32), axis=1), N_EXPERTS - 1)
    nvalid = jnp.clip(counts[block_e] - (bstart - pstarts[block_e]), 0, MOE_BLOCK)
    nvalid = jnp.where(bstart < pends[-1], nvalid, 0)
    return pstarts, block_e.astype(I32), nvalid.astype(I32)


def _dest_tiles(idx, rank, pstarts):
    e = idx[:TOP_K]
    start = jnp.sum(jnp.where(e[:, :, None] == jnp.arange(N_EXPERTS)[None, None, :], pstarts[None, None, :], 0),
                    axis=2)
    dest = (start + rank[:TOP_K]).astype(I32).T
    return dest.reshape(-1, 1, TOP_K * T_DMA)


def kernel(x, c, ctx, c_ctx, w_ada, b_ada, norm1_w, norm2_w, w_in, pool_w, pool_scale, conv_w, conv_b,
           dt_bias, a_log, d_skip, ssm_norm_w, w_branch, w_out, router_w, router_b, moe_w1, moe_b1,
           moe_w2, moe_b2, final_norm_w):
    batch, n, d = x.shape
    n_ctx = ctx.shape[1]
    depth = w_ada.shape[0]
    d_inner = SSM_HEADS * SSM_HEAD_DIM
    conv_dim = conv_w.shape[2]
    off_four, off_z, off_xbc = d, 2 * d, 3 * d
    off_dt = off_xbc + conv_dim
    off_gate = off_dt + 2 * SSM_HEADS
    assert batch + 1 <= V7X_SUBLANES and d_inner == d and conv_dim == 2 * d

    lat = x.reshape(batch * n, d).astype(F32)
    cx = ctx.reshape(batch * n_ctx, d).astype(F32)
    cvec = jnp.concatenate([c.astype(F32), c_ctx.astype(F32)[None, :],
                            jnp.zeros((V7X_SUBLANES - batch - 1, d), F32)], axis=0)
    mods = _ada(cvec, w_ada.astype(F32), b_ada.astype(F32))
    seg_ctx = batch
    zero_state = jnp.zeros((batch, 2, d_inner, SSM_STATE), F32)
    zero_cnt = jnp.zeros((V7X_LANES, V7X_LANES), F32)

    for i in range(depth):
        last = i == depth - 1
        modv = mods[i].reshape(V7X_SUBLANES, N_ADA, d)[:batch + 1]
        modv = jnp.pad(modv, ((0, 0), (0, V7X_SUBLANES - N_ADA), (0, 0)))
        wi = w_in[i]
        w_main = jnp.concatenate([wi[:, off_xbc:off_dt], wi[:, :off_z], wi[:, off_z:off_xbc], wi[:, off_gate:]],
                                 axis=1).astype(BF16)
        w_dt = jnp.pad(wi[:, off_dt:off_gate], ((0, 0), (0, V7X_LANES - 2 * SSM_HEADS))).astype(BF16)

        u_l, dtr_l = _inproj(lat, modv, norm1_w[i], w_main, w_dt, tm=TM_LAT, seg0=0, tiles_per_seg=n // TM_LAT)
        u_c, dtr_c = _inproj(cx, modv, norm1_w[i], w_main, w_dt, tm=batch * n_ctx, seg0=seg_ctx, tiles_per_seg=1)

        conv_args = (conv_w[i], conv_b[i], dt_bias[i], a_log[i])
        cv_c = _conv(u_c, dtr_c, *conv_args, tc=n_ctx, seq_len=n_ctx, d_inner=d_inner)
        cv_l = _conv(u_l, dtr_l, *conv_args, tc=TM_ROW, seq_len=n, d_inner=d_inner)
        yf_c, yb_c, st = _ssd(cv_c, d_skip[i], zero_state, batch=batch, seq_len=n_ctx)
        yf_l, yb_l, _ = _ssd(cv_l, d_skip[i], st, batch=batch, seq_len=n)

        ya_l = _pool2d(u_l, pool_w[i], pool_scale[i], seq_len=n)
        fb_l = _fourier_lat(u_l, batch=batch, seq_len=n, d=d)

        wb = w_branch[i].astype(BF16)
        wo = w_out[i].astype(BF16)
        rw = jnp.pad(router_w[i].astype(F32), ((0, 0), (0, V7X_LANES - N_EXPERTS)))
        rb = jnp.pad(router_b[i].astype(F32), (0, V7X_LANES - N_EXPERTS), constant_values=-1e30)
        rb = jnp.broadcast_to(rb[:, None], (V7X_LANES, V7X_LANES))
        merge_w = (ssm_norm_w[i], wb, wo, norm2_w[i], rw, rb)
        lat_m, h2_l, idx_l, rank_l, wts_l, cnt = _merge(
            ya_l, fb_l, yf_l, yb_l, u_l, lat, modv, *merge_w, zero_cnt,
            tm=TM_ROW, seg0=0, tiles_per_seg=n // TM_ROW)
        n_tok = batch * n
        if not last:
            ya_c = _pool1d(u_c, pool_w[i], pool_scale[i], seq_len=n_ctx)
            fb_c = _fourier_ctx(u_c, seq_len=n_ctx, d=d)
            cx_m, h2_c, idx_c, rank_c, wts_c, cnt = _merge(
                ya_c, fb_c, yf_c, yb_c, u_c, cx, modv, *merge_w, cnt,
                tm=batch * n_ctx, seg0=seg_ctx, tiles_per_seg=1)
            n_tok += batch * n_ctx

        n_blocks = -(-(n_tok * TOP_K) // MOE_BLOCK) + N_EXPERTS
        n_slots = n_blocks * MOE_BLOCK
        counts = cnt[:N_EXPERTS, 0].astype(I32)
        pstarts, block_e, nvalid = _routing_tables(counts, n_blocks)
        dest_l = _dest_tiles(idx_l, rank_l, pstarts)
        xs = _dispatch(h2_l, dest_l, n_slots)
        if not last:
            dest_c = _dest_tiles(idx_c, rank_c, pstarts)
            xs = _dispatch(h2_c, dest_c, n_slots, xs_prev=xs)
        ys = _experts(xs, block_e, nvalid, moe_w1[i], moe_b1[i], moe_w2[i], moe_b2[i])
        lat = _combine(lat_m, dest_l, wts_l.T, modv, final_norm_w, ys, seg0=0, tiles_per_seg=n // T_DMA,
                       final_norm=last)
        if not last:
            cx = _combine(cx_m, dest_c, wts_c.T, modv, final_norm_w, ys, seg0=seg_ctx,
                          tiles_per_seg=batch * n_ctx // T_DMA, final_norm=False)
    return lat.reshape(batch, n, d).astype(x.dtype)
```

```python
import functools
import math

import numpy as np
import jax
import jax.numpy as jnp
from jax import lax
from jax.experimental import pallas as pl
from jax.experimental.pallas import tpu as pltpu

F32 = jnp.float32
BF16 = jnp.bfloat16
I32 = jnp.int32
HIGHEST = lax.Precision.HIGHEST

GRID_W = 64
N_ADA = 6
N_BRANCH = 3
MIX_GROUPS = 4
POOL_WINDOWS = (2, 4, 8, 16)
SSM_HEADS = 16
SSM_HEAD_DIM = 64
SSM_GROUPS = 4
SSM_STATE = 128
SSM_CONV = 5
SSM_CHUNK = 128
N_EXPERTS = 32
TOP_K = 4
SWIGLU_ALPHA = 1.702
SWIGLU_LIMIT = 7.0
RMS_EPS = 1e-6
MOE_BLOCK = 256

V7X_VMEM_BYTES = 64 * 2**20
V7X_LANES = 128
V7X_SUBLANES = 8
V7X_BF16_SUBLANES = 16
VMEM_COMPILER_ALLOWANCE = 6 * 2**20

TM_LAT = 1024
TM_ROW = 512
T_DMA = 512
DMA_UNROLL = 4
DMA_PRIORITIES = 2
FOURIER_N1 = 64
F1_BLOCK = 16
F2_BLOCK = 8


def _params(semantics, vmem_bytes):
    limit = min(int(vmem_bytes) + VMEM_COMPILER_ALLOWANCE, V7X_VMEM_BYTES - 2 * 2**20)
    return pltpu.CompilerParams(dimension_semantics=semantics, vmem_limit_bytes=limit)


def _nbytes(shape, dtype):
    return int(np.prod(shape)) * jnp.dtype(dtype).itemsize


def _silu(v):
    return v * jax.nn.sigmoid(v)


def _rms(v, w):
    return v * lax.rsqrt(jnp.mean(v * v, axis=-1, keepdims=True) + RMS_EPS) * w


def _ada_kernel(c_ref, w_ref, b_ref, o_ref):
    o_ref[...] = jnp.dot(_silu(c_ref[...]), w_ref[...], preferred_element_type=F32,
                         precision=HIGHEST) + b_ref[...]


def _ada(cvec, w_ada, b_ada):
    depth, d, nd = w_ada.shape
    return pl.pallas_call(
        _ada_kernel,
        out_shape=jax.ShapeDtypeStruct((depth, V7X_SUBLANES, nd), F32),
        grid=(depth, nd // d),
        in_specs=[pl.BlockSpec((V7X_SUBLANES, d), lambda l, j: (0, 0)),
                  pl.BlockSpec((None, d, d), lambda l, j: (l, 0, j)),
                  pl.BlockSpec((None, 1, d), lambda l, j: (l, 0, j))],
        out_specs=pl.BlockSpec((None, V7X_SUBLANES, d), lambda l, j: (l, 0, j)),
        compiler_params=_params(("parallel", "parallel"), 2 * _nbytes((d, d), F32)),
        name="ada",
    )(cvec, w_ada, b_ada.reshape(depth, 1, nd))


U_XBC, U_POOL, U_FOUR, U_Z, U_GATE = 0, 2, 3, 4, 5
U_BLOCKS = 8


def _inproj_kernel(x_ref, mod_ref, nw_ref, w_ref, wdt_ref, u_ref, dt_ref, h_scr):
    @pl.when(pl.program_id(1) == 0)
    def _():
        h = _rms(x_ref[...], nw_ref[...]) * (1.0 + mod_ref[1:2, :]) + mod_ref[0:1, :]
        hb = h.astype(BF16)
        h_scr[...] = hb
        dt_ref[...] = jnp.dot(hb, wdt_ref[...], preferred_element_type=F32)

    u_ref[...] = jnp.dot(h_scr[...], w_ref[...], preferred_element_type=F32).astype(BF16)


def _inproj(x, modv, nw, w_main, w_dt, *, tm, seg0, tiles_per_seg):
    t, d = x.shape
    vmem = 2 * (_nbytes((tm, d), F32) + 2 * _nbytes((tm, d), BF16) + _nbytes((d, d), BF16)
                + _nbytes((tm, V7X_LANES), F32) + _nbytes((d, V7X_LANES), BF16))
    return pl.pallas_call(
        _inproj_kernel,
        out_shape=(jax.ShapeDtypeStruct((t, U_BLOCKS * d), BF16),
                   jax.ShapeDtypeStruct((t, V7X_LANES), F32)),
        grid=(t // tm, U_BLOCKS),
        in_specs=[pl.BlockSpec((tm, d), lambda i, j: (i, 0)),
                  pl.BlockSpec((None, V7X_SUBLANES, d), lambda i, j: (seg0 + i // tiles_per_seg, 0, 0)),
                  pl.BlockSpec((1, d), lambda i, j: (0, 0)),
                  pl.BlockSpec((d, d), lambda i, j: (0, j)),
                  pl.BlockSpec((d, V7X_LANES), lambda i, j: (0, 0))],
        out_specs=(pl.BlockSpec((tm, d), lambda i, j: (i, j)),
                   pl.BlockSpec((tm, V7X_LANES), lambda i, j: (i, 0))),
        scratch_shapes=[pltpu.VMEM((tm, d), BF16)],
        compiler_params=_params(("parallel", "arbitrary"), vmem),
        name="inproj",
    )(x, modv, nw.reshape(1, d), w_main, w_dt)


CONV_HALO = V7X_BF16_SUBLANES
CONV_RB = 128
CONV_CB = 128


def _conv_kernel(cur_ref, prev_ref, next_ref, dtraw_ref, cw_ref, cb_ref, dtb_ref, alog_ref,
                 xt_ref, bc_ref, ct_ref, aux_ref, acum_ref, xp_scr, *, tc, tiles_per_seq, d_inner):
    p = pl.program_id(0) % tiles_per_seq
    xp_scr[0:CONV_HALO, :] = jnp.where(p > 0, prev_ref[...].astype(F32), 0.0)
    xp_scr[CONV_HALO:CONV_HALO + tc, :] = cur_ref[...].astype(F32)
    xp_scr[CONV_HALO + tc:, :] = jnp.where(p < tiles_per_seq - 1, next_ref[...].astype(F32), 0.0)
    n_bc = SSM_GROUPS * SSM_STATE
    for rb in range(tc // CONV_RB):
        rows = slice(rb * CONV_RB, (rb + 1) * CONV_RB)
        for cb in range(cur_ref.shape[1] // CONV_CB):
            c0 = cb * CONV_CB
            cols = slice(c0, c0 + CONV_CB)
            acc = jnp.broadcast_to(cb_ref[:, cols], (CONV_RB, CONV_CB))
            for k in range(SSM_CONV):
                r0 = CONV_HALO - SSM_CONV // 2 + k + rb * CONV_RB
                acc = acc + cw_ref[k:k + 1, cols] * xp_scr[r0:r0 + CONV_RB, cols]
            v = _silu(acc)
            if c0 < d_inner:
                xt_ref[cols, rows] = v.T.astype(BF16)
            elif c0 < d_inner + n_bc:
                bc_ref[rows, c0 - d_inner:c0 - d_inner + CONV_CB] = v.astype(BF16)
            else:
                bc_ref[rows, c0 - d_inner:c0 - d_inner + CONV_CB] = v.astype(BF16)
                cc = c0 - d_inner - n_bc
                ct_ref[cc:cc + CONV_CB, rows] = v.T.astype(BF16)

    dtr = dtraw_ref[...] + dtb_ref[...]
    dt = jnp.maximum(dtr, 0.0) + jnp.log1p(jnp.exp(-jnp.abs(dtr)))
    dta = dt * (-jnp.exp(alog_ref[...]))
    dt_t = dt.T
    dta_t = dta.T
    ki = lax.broadcasted_iota(I32, (tc, tc), 0)
    li = lax.broadcasted_iota(I32, (tc, tc), 1)
    same = (ki // SSM_CHUNK) == (li // SSM_CHUNK)
    one, zero = jnp.float32(1.0), jnp.float32(0.0)
    cumf = jnp.dot(dta_t, jnp.where(same & (ki <= li), one, zero), preferred_element_type=F32,
                   precision=HIGHEST)
    cumb = jnp.dot(dta_t, jnp.where(same & (ki >= li), one, zero), preferred_element_type=F32,
                   precision=HIGHEST)
    tot = jnp.dot(dta_t, jnp.where(same, one, zero), preferred_element_type=F32, precision=HIGHEST)
    nh = SSM_HEADS
    aux_ref[0:2 * nh, :] = dt_t[0:2 * nh, :]
    aux_ref[2 * nh:3 * nh, :] = cumf[0:nh, :]
    aux_ref[3 * nh:4 * nh, :] = cumb[nh:2 * nh, :]
    aux_ref[4 * nh:6 * nh, :] = tot[0:2 * nh, :]
    aux_ref[6 * nh:, :] = jnp.zeros((V7X_LANES - 6 * nh, tc), F32)
    hrow = lax.broadcasted_iota(I32, (V7X_LANES, tc), 0)
    acum_ref[...] = jnp.where(hrow < nh, cumf, cumb).T


def _conv(u, dt_raw, conv_w, conv_b, dt_bias, a_log, *, tc, seq_len, d_inner):
    t = u.shape[0]
    conv_dim = conv_w.shape[1]
    n_bc = SSM_GROUPS * SSM_STATE
    nt = t // tc
    hb = tc // CONV_HALO
    last_halo = t // CONV_HALO - 1
    cw = jnp.pad(conv_w.astype(F32), ((0, V7X_SUBLANES - SSM_CONV), (0, 0)))
    pad32 = lambda v: jnp.pad(v.astype(F32).reshape(1, -1), ((0, 0), (0, V7X_LANES - 2 * SSM_HEADS)))
    vmem = (2 * (_nbytes((tc, conv_dim), BF16) * 2 + _nbytes((tc, V7X_LANES), F32) * 3
                 + _nbytes((d_inner + 2 * n_bc, tc), BF16))
            + _nbytes((tc + 2 * CONV_HALO, conv_dim), F32) + 4 * _nbytes((tc, tc), F32))
    kern = functools.partial(_conv_kernel, tc=tc, tiles_per_seq=seq_len // tc, d_inner=d_inner)
    return pl.pallas_call(
        kern,
        out_shape=(jax.ShapeDtypeStruct((d_inner, t), BF16),
                   jax.ShapeDtypeStruct((t, 2 * n_bc), BF16),
                   jax.ShapeDtypeStruct((n_bc, t), BF16),
                   jax.ShapeDtypeStruct((V7X_LANES, t), F32),
                   jax.ShapeDtypeStruct((t, V7X_LANES), F32)),
        grid=(nt,),
        in_specs=[pl.BlockSpec((tc, conv_dim), lambda i: (i, U_XBC)),
                  pl.BlockSpec((CONV_HALO, conv_dim), lambda i: (jnp.maximum(i * hb - 1, 0), U_XBC)),
                  pl.BlockSpec((CONV_HALO, conv_dim), lambda i: (jnp.minimum((i + 1) * hb, last_halo), U_XBC)),
                  pl.BlockSpec((tc, V7X_LANES), lambda i: (i, 0)),
                  pl.BlockSpec((V7X_SUBLANES, conv_dim), lambda i: (0, 0)),
                  pl.BlockSpec((1, conv_dim), lambda i: (0, 0)),
                  pl.BlockSpec((1, V7X_LANES), lambda i: (0, 0)),
                  pl.BlockSpec((1, V7X_LANES), lambda i: (0, 0))],
        out_specs=(pl.BlockSpec((d_inner, tc), lambda i: (0, i)),
                   pl.BlockSpec((tc, 2 * n_bc), lambda i: (i, 0)),
                   pl.BlockSpec((n_bc, tc), lambda i: (0, i)),
                   pl.BlockSpec((V7X_LANES, tc), lambda i: (0, i)),
                   pl.BlockSpec((tc, V7X_LANES), lambda i: (i, 0))),
        scratch_shapes=[pltpu.VMEM((tc + 2 * CONV_HALO, conv_dim), F32)],
        compiler_params=_params(("parallel",), vmem),
        name="conv",
    )(u, u, u, dt_raw, cw, conv_b.astype(F32).reshape(1, -1), pad32(dt_bias), pad32(a_log))


def _ssd_direction(d, rev, xt_ref, bc_ref, ct_ref, aux_ref, acum_ref, st_scr, y_ref, dsk_ref):
    q = SSM_CHUNK
    hd = SSM_HEAD_DIM
    nh = SSM_HEADS
    si = lax.broadcasted_iota(I32, (q, q), 0)
    li = lax.broadcasted_iota(I32, (q, q), 1)
    mask = (li <= si) if rev else (li >= si)
    hpg = nh // SSM_GROUPS
    for g in range(SSM_GROUPS):
        b_g = bc_ref[:, g * SSM_STATE:(g + 1) * SSM_STATE]
        c_g = bc_ref[:, (SSM_GROUPS + g) * SSM_STATE:(SSM_GROUPS + g + 1) * SSM_STATE]
        ct_g = ct_ref[g * SSM_STATE:(g + 1) * SSM_STATE, :].astype(F32)
        cbt = lax.dot_general(b_g, c_g, (((1,), (1,)), ((), ())), preferred_element_type=F32)
        for hh in range(hpg):
            h = g * hpg + hh
            r = d * nh + h
            dt_row = aux_ref[r:r + 1, :]
            cum_row = aux_ref[2 * nh + r:2 * nh + r + 1, :]
            tot_row = aux_ref[4 * nh + r:4 * nh + r + 1, :]
            cum_col = acum_ref[:, r:r + 1]
            decay = jnp.exp(jnp.where(mask, cum_row - cum_col, -jnp.inf))
            w_top = (cbt * decay).astype(BF16)
            w_bot = (ct_g * jnp.exp(cum_row)).astype(BF16)
            w = jnp.concatenate([w_top, w_bot], axis=0)
            rows = slice(h * hd, (h + 1) * hd)
            x_t = xt_ref[rows, :].astype(F32)
            xd = x_t * dt_row
            s_old = st_scr[d, rows, :]
            lhs = jnp.concatenate([xd.astype(BF16), s_old.astype(BF16)], axis=1)
            y_t = jnp.dot(lhs, w, preferred_element_type=F32)
            if not rev:
                y_t = y_t + dsk_ref[h] * x_t
            y_ref[rows, :] = y_t.astype(y_ref.dtype)
            xdw = (xd * jnp.exp(tot_row - cum_row)).astype(BF16)
            st_scr[d, rows, :] = s_old * jnp.exp(tot_row) + jnp.dot(xdw, b_g, preferred_element_type=F32)


def _ssd_kernel(dsk_ref, xt_f, bc_f, ct_f, aux_f, ac_f, xt_b, bc_b, ct_b, aux_b, ac_b, init_ref,
                yf_ref, yb_ref, fin_ref, st_scr):
    c = pl.program_id(1)

    @pl.when(c == 0)
    def _():
        st_scr[...] = init_ref[...]

    _ssd_direction(0, False, xt_f, bc_f, ct_f, aux_f, ac_f, st_scr, yf_ref, dsk_ref)
    _ssd_direction(1, True, xt_b, bc_b, ct_b, aux_b, ac_b, st_scr, yb_ref, dsk_ref)

    @pl.when(c == pl.num_programs(1) - 1)
    def _():
        fin_ref[...] = st_scr[...]


def _ssd(conv_out, d_skip, init_state, *, batch, seq_len):
    xt, bc, ct, aux, acum = conv_out
    d_inner, t = xt.shape
    n_bc = ct.shape[0]
    q = SSM_CHUNK
    nc = seq_len // q
    fwd = lambda b, c: b * nc + c
    bwd = lambda b, c: b * nc + nc - 1 - c

    def specs(pos):
        return [pl.BlockSpec((d_inner, q), lambda b, c: (0, pos(b, c))),
                pl.BlockSpec((q, 2 * n_bc), lambda b, c: (pos(b, c), 0)),
                pl.BlockSpec((n_bc, q), lambda b, c: (0, pos(b, c))),
                pl.BlockSpec((V7X_LANES, q), lambda b, c: (0, pos(b, c))),
                pl.BlockSpec((q, V7X_LANES), lambda b, c: (pos(b, c), 0))]

    st_spec = pl.BlockSpec((None, 2, d_inner, SSM_STATE), lambda b, c: (b, 0, 0, 0))
    st_bytes = _nbytes((2, d_inner, SSM_STATE), F32)
    vmem = 5 * st_bytes + 4 * (_nbytes((d_inner, q), BF16) * 2 + _nbytes((q, 2 * n_bc), BF16)
                               + _nbytes((n_bc, q), BF16) + 2 * _nbytes((q, V7X_LANES), F32))
    return pl.pallas_call(
        _ssd_kernel,
        out_shape=(jax.ShapeDtypeStruct((d_inner, t), BF16),
                   jax.ShapeDtypeStruct((d_inner, t), BF16),
                   jax.ShapeDtypeStruct((batch, 2, d_inner, SSM_STATE), F32)),
        grid=(batch, nc),
        in_specs=[pl.BlockSpec(memory_space=pltpu.SMEM)] + specs(fwd) + specs(bwd) + [st_spec],
        out_specs=(pl.BlockSpec((d_inner, q), lambda b, c: (0, fwd(b, c))),
                   pl.BlockSpec((d_inner, q), lambda b, c: (0, bwd(b, c))),
                   st_spec),
        scratch_shapes=[pltpu.VMEM((2, d_inner, SSM_STATE), F32)],
        compiler_params=_params(("arbitrary", "arbitrary"), vmem),
        name="ssd",
    )(d_skip.astype(F32), xt, bc, ct, aux, acum, xt, bc, ct, aux, acum, init_state)


def _pool_body(slabs, cur_ref, mask_refs, icnt_ref, pw_ref, ps_ref, o_ref):
    gw = cur_ref.shape[1] // MIX_GROUPS
    for g in range(MIX_GROUPS):
        cols = slice(g * gw, (g + 1) * gw)
        xin = slabs(g, cols)
        s = jnp.dot(mask_refs[g][...], xin, preferred_element_type=F32)
        inv = icnt_ref[:, g * V7X_LANES:(g + 1) * V7X_LANES]
        p = s * jnp.concatenate([inv] * (gw // V7X_LANES), axis=1)
        dlt = p - cur_ref[:, cols].astype(F32)
        y = jnp.dot(dlt.astype(BF16), pw_ref[g], preferred_element_type=F32) * ps_ref[:, cols]
        o_ref[:, cols] = y.astype(BF16)


def _pool2d_kernel(prev_ref, cur_ref, next_ref, m0, m1, m2, m3, icnt_ref, pw_ref, ps_ref, o_ref):
    tile = cur_ref.shape[0]

    def slabs(g, cols):
        w = POOL_WINDOWS[g]
        up, dn = w // 2, w - w // 2 - 1
        parts = [prev_ref[tile - GRID_W * up:, cols], cur_ref[:, cols]]
        if dn:
            parts.append(next_ref[:GRID_W * dn, cols])
        return jnp.concatenate(parts, axis=0)

    _pool_body(slabs, cur_ref, (m0, m1, m2, m3), icnt_ref, pw_ref, ps_ref, o_ref)


def _pool1d_kernel(cur_ref, m0, m1, m2, m3, icnt_ref, pw_ref, ps_ref, o_ref):
    _pool_body(lambda g, cols: cur_ref[:, cols], cur_ref, (m0, m1, m2, m3), icnt_ref, pw_ref, ps_ref, o_ref)


def _window_ok(out_pos, in_pos, w):
    lo = out_pos[:, None] - w // 2
    return (in_pos[None, :] >= lo) & (in_pos[None, :] < lo + w)


def _pool_tables_2d(tile):
    rows = tile // GRID_W
    t = jnp.arange(tile)
    masks, invs = [], []
    for w in POOL_WINDOWS:
        up, dn = w // 2, w - w // 2 - 1
        u = jnp.arange(tile + GRID_W * (up + dn))
        r_in, c_in = u // GRID_W - up, u % GRID_W
        base = _window_ok(t // GRID_W, r_in, w) & _window_ok(t % GRID_W, c_in, w)
        var = jnp.stack([base & (r_in >= 0)[None, :], base, base & (r_in < rows)[None, :]])
        masks.append(var.astype(BF16))
        invs.append(1.0 / jnp.sum(var.astype(F32), axis=2))
    inv = jnp.stack(invs, axis=1)
    inv = jnp.broadcast_to(inv[:, :, :, None], inv.shape + (V7X_LANES,))
    inv = jnp.transpose(inv, (0, 2, 1, 3)).reshape(3, tile, MIX_GROUPS * V7X_LANES)
    return masks, inv


def _pool_tables_1d(n):
    t = jnp.arange(n)
    masks, invs = [], []
    for w in POOL_WINDOWS:
        m = _window_ok(t, t, w)
        masks.append(m.astype(BF16))
        invs.append(1.0 / jnp.sum(m.astype(F32), axis=1))
    inv = jnp.stack(invs, axis=0)
    inv = jnp.broadcast_to(inv[:, :, None], inv.shape + (V7X_LANES,))
    return masks, jnp.transpose(inv, (1, 0, 2)).reshape(n, MIX_GROUPS * V7X_LANES)


def _pool2d(u, pool_w, pool_scale, *, seq_len):
    t = u.shape[0]
    d = pool_scale.shape[0]
    tile = TM_ROW
    nt, per_img = t // tile, seq_len // tile
    masks, inv = _pool_tables_2d(tile)

    def variant(i):
        p = i % per_img
        return jnp.where(p == 0, 0, jnp.where(p == per_img - 1, 2, 1))

    gw = d // MIX_GROUPS
    vmem = 2 * (4 * _nbytes((tile, d), BF16) + sum(_nbytes(m.shape[1:], BF16) for m in masks)
                + _nbytes(inv.shape[1:], F32) + _nbytes((MIX_GROUPS, gw, gw), BF16))
    return pl.pallas_call(
        _pool2d_kernel,
        out_shape=jax.ShapeDtypeStruct((t, d), BF16),
        grid=(nt,),
        in_specs=[pl.BlockSpec((tile, d), lambda i: (jnp.maximum(i - 1, 0), U_POOL)),
                  pl.BlockSpec((tile, d), lambda i: (i, U_POOL)),
                  pl.BlockSpec((tile, d), lambda i: (jnp.minimum(i + 1, nt - 1), U_POOL))]
                 + [pl.BlockSpec((None,) + m.shape[1:], lambda i: (variant(i), 0, 0)) for m in masks]
                 + [pl.BlockSpec((None,) + inv.shape[1:], lambda i: (variant(i), 0, 0)),
                    pl.BlockSpec((MIX_GROUPS, gw, gw), lambda i: (0, 0, 0)),
                    pl.BlockSpec((1, d), lambda i: (0, 0))],
        out_specs=pl.BlockSpec((tile, d), lambda i: (i, 0)),
        compiler_params=_params(("parallel",), vmem),
        name="pool2d",
    )(u, u, u, *masks, inv, pool_w.astype(BF16), pool_scale.astype(F32).reshape(1, d))


def _pool1d(u, pool_w, pool_scale, *, seq_len):
    t = u.shape[0]
    d = pool_scale.shape[0]
    masks, inv = _pool_tables_1d(seq_len)
    gw = d // MIX_GROUPS
    vmem = 2 * (2 * _nbytes((seq_len, d), BF16) + 4 * _nbytes((seq_len, seq_len), BF16)
                + _nbytes(inv.shape, F32) + _nbytes((MIX_GROUPS, gw, gw), BF16))
    return pl.pallas_call(
        _pool1d_kernel,
        out_shape=jax.ShapeDtypeStruct((t, d), BF16),
        grid=(t // seq_len,),
        in_specs=[pl.BlockSpec((seq_len, d), lambda i: (i, U_POOL))]
                 + [pl.BlockSpec(m.shape, lambda i: (0, 0)) for m in masks]
                 + [pl.BlockSpec(inv.shape, lambda i: (0, 0)),
                    pl.BlockSpec((MIX_GROUPS, gw, gw), lambda i: (0, 0, 0)),
                    pl.BlockSpec((1, d), lambda i: (0, 0))],
        out_specs=pl.BlockSpec((seq_len, d), lambda i: (i, 0)),
        compiler_params=_params(("parallel",), vmem),
        name="pool1d",
    )(u, *masks, inv, pool_w.astype(BF16), pool_scale.astype(F32).reshape(1, d))


def _dft_tables(n_pos, gw):
    n1, n2 = FOURIER_N1, n_pos // FOURIER_N1
    c = np.arange(gw)
    ang = 2.0 * np.pi * ((c[:, None] * c[None, :]) % gw) / gw
    cs = np.concatenate([np.cos(ang), -np.sin(ang)], axis=1)
    a = np.arange(n1)
    ang1 = 2.0 * np.pi * ((a[:, None] * a[None, :]) % n1) / n1
    wc, ws = np.cos(ang1), np.sin(ang1)
    w1 = np.block([[wc, ws], [-ws, wc]])
    k = a[:, None, None] + n1 * np.arange(n2)[None, :, None]
    ang2 = 2.0 * np.pi * ((k * np.arange(n2)[None, None, :]) % n_pos) / n_pos
    m2 = np.concatenate([np.cos(ang2), np.sin(ang2)], axis=2)
    to_bf = lambda v: jnp.asarray(v, dtype=F32).astype(BF16)
    return to_bf(cs), to_bf(w1), to_bf(m2)


def _f1_kernel(u_ref, cs_ref, w1_ref, o_ref, z_scr):
    nb, n1, d = u_ref.shape
    gw = d // MIX_GROUPS
    u = u_ref[...].reshape(nb * n1, d)
    for g in range(MIX_GROUPS):
        z = jnp.dot(u[:, g * gw:(g + 1) * gw], cs_ref[...], preferred_element_type=F32)
        z_scr[:, g * gw:(g + 1) * gw] = z[:, :gw].astype(BF16)
        z_scr[:, d + g * gw:d + (g + 1) * gw] = z[:, gw:].astype(BF16)
    for j in range(nb):
        zz = jnp.concatenate([z_scr[j * n1:(j + 1) * n1, :d], z_scr[j * n1:(j + 1) * n1, d:]], axis=0)
        o_ref[j] = jnp.dot(w1_ref[...], zz, preferred_element_type=F32).astype(BF16)


def _f2_kernel(a_ref, m2_ref, o_ref, *, scale):
    for j in range(a_ref.shape[0]):
        o_ref[j] = (jnp.dot(m2_ref[j], a_ref[j], preferred_element_type=F32) * scale).astype(BF16)


def _fourier_lat(u, *, batch, seq_len, d):
    n1, n2 = FOURIER_N1, seq_len // FOURIER_N1
    gw = d // MIX_GROUPS
    cs, w1, m2 = _dft_tables(seq_len, gw)
    ut = u[:, U_FOUR * d:(U_FOUR + 1) * d].reshape(batch, n1, n2, d).transpose(0, 2, 1, 3)
    nb = F1_BLOCK
    vmem1 = 2 * (_nbytes((nb, n1, d), BF16) + _nbytes((nb, 2 * n1, d), BF16)) + _nbytes((nb * n1, 2 * d), BF16) \
        + 2 * _nbytes((nb * n1, 2 * gw), F32)
    a = pl.pallas_call(
        _f1_kernel,
        out_shape=jax.ShapeDtypeStruct((batch, n2, 2 * n1, d), BF16),
        grid=(batch, n2 // nb),
        in_specs=[pl.BlockSpec((None, nb, n1, d), lambda b, j: (b, j, 0, 0)),
                  pl.BlockSpec(cs.shape, lambda b, j: (0, 0)),
                  pl.BlockSpec(w1.shape, lambda b, j: (0, 0))],
        out_specs=pl.BlockSpec((None, nb, 2 * n1, d), lambda b, j: (b, j, 0, 0)),
        scratch_shapes=[pltpu.VMEM((nb * n1, 2 * d), BF16)],
        compiler_params=_params(("parallel", "parallel"), vmem1),
        name="fourier_stage1",
    )(ut, cs, w1)
    at = a.reshape(batch, n2, 2, n1, d).transpose(0, 3, 2, 1, 4).reshape(batch, n1, 2 * n2, d)
    kb = F2_BLOCK
    vmem2 = 2 * (_nbytes((kb, 2 * n2, d), BF16) + _nbytes((kb, n2, 2 * n2), BF16) + _nbytes((kb, n2, d), BF16))
    x2 = pl.pallas_call(
        functools.partial(_f2_kernel, scale=1.0 / math.sqrt(seq_len * gw)),
        out_shape=jax.ShapeDtypeStruct((batch, n1, n2, d), BF16),
        grid=(batch, n1 // kb),
        in_specs=[pl.BlockSpec((None, kb, 2 * n2, d), lambda b, j: (b, j, 0, 0)),
                  pl.BlockSpec((kb, n2, 2 * n2), lambda b, j: (j, 0, 0))],
        out_specs=pl.BlockSpec((None, kb, n2, d), lambda b, j: (b, j, 0, 0)),
        compiler_params=_params(("parallel", "parallel"), vmem2),
        name="fourier_stage2",
    )(at, m2)
    return x2.transpose(0, 2, 1, 3).reshape(batch * seq_len, d)


def _fourier_ctx_kernel(u_ref, cs_ref, cn_ref, o_ref, *, scale):
    d = u_ref.shape[1]
    gw = d // MIX_GROUPS
    zr, zi = [], []
    for g in range(MIX_GROUPS):
        z = jnp.dot(u_ref[:, g * gw:(g + 1) * gw], cs_ref[...], preferred_element_type=F32)
        zr.append(z[:, :gw].astype(BF16))
        zi.append(z[:, gw:].astype(BF16))
    zz = jnp.concatenate([jnp.concatenate(zr, axis=1), jnp.concatenate(zi, axis=1)], axis=0)
    o_ref[...] = (jnp.dot(cn_ref[...], zz, preferred_element_type=F32) * scale).astype(BF16)


def _fourier_ctx(u, *, seq_len, d):
    t = u.shape[0]
    gw = d // MIX_GROUPS
    cs, _, _ = _dft_tables(FOURIER_N1 * 2, gw)
    n = np.arange(seq_len)
    ang = 2.0 * np.pi * ((n[:, None] * n[None, :]) % seq_len) / seq_len
    cn = jnp.asarray(np.concatenate([np.cos(ang), np.sin(ang)], axis=1), dtype=F32).astype(BF16)
    vmem = 2 * (2 * _nbytes((seq_len, d), BF16) + _nbytes(cs.shape, BF16) + _nbytes(cn.shape, BF16)) \
        + 4 * _nbytes((seq_len, d), F32)
    return pl.pallas_call(
        functools.partial(_fourier_ctx_kernel, scale=1.0 / math.sqrt(seq_len * gw)),
        out_shape=jax.ShapeDtypeStruct((t, d), BF16),
        grid=(t // seq_len,),
        in_specs=[pl.BlockSpec((seq_len, d), lambda i: (i, U_FOUR)),
                  pl.BlockSpec(cs.shape, lambda i: (0, 0)),
                  pl.BlockSpec(cn.shape, lambda i: (0, 0))],
        out_specs=pl.BlockSpec((seq_len, d), lambda i: (i, 0)),
        compiler_params=_params(("parallel",), vmem),
        name="fourier_ctx",
    )(u, cs, cn)


def _merge_kernel(ya_ref, yb_ref, yf_ref, ybw_ref, z_ref, g0_ref, g1_ref, g2_ref, lat_ref, mod_ref,
                  snw_ref, wb_ref, wo_ref, n2w_ref, rw_ref, rb_ref, tri_ref, ones_ref, cnt0_ref,
                  lat_o, h2_o, idx_o, rank_o, wts_o, cnt_o, run_scr):
    tm = lat_ref.shape[0]

    @pl.when(pl.program_id(0) == 0)
    def _():
        run_scr[...] = cnt0_ref[...]

    y_ssm = (yf_ref[...].astype(F32) + ybw_ref[...].astype(F32)).T
    yc = _rms(y_ssm * _silu(z_ref[...].astype(F32)), snw_ref[...])
    merged = jnp.zeros(lat_ref.shape, F32)
    for k, (y, g_ref) in enumerate(((ya_ref[...], g0_ref), (yb_ref[...], g1_ref), (yc.astype(BF16), g2_ref))):
        proj = jnp.dot(y, wb_ref[k], preferred_element_type=F32)
        merged = merged + jax.nn.sigmoid(g_ref[...].astype(F32)) * proj
    ol = jnp.dot(merged.astype(BF16), wo_ref[...], preferred_element_type=F32)
    lat = lat_ref[...] + mod_ref[2:3, :] * ol
    lat_o[...] = lat
    h2 = _rms(lat, n2w_ref[...]) * (1.0 + mod_ref[4:5, :]) + mod_ref[3:4, :]
    h2_o[...] = h2

    h_hi = h2.astype(BF16)
    h_lo = (h2 - h_hi.astype(F32)).astype(BF16)
    nt_dims = (((1,), (1,)), ((), ()))
    lt = (lax.dot_general(rw_ref[0], h_hi, nt_dims, preferred_element_type=F32)
          + lax.dot_general(rw_ref[0], h_lo, nt_dims, preferred_element_type=F32)
          + lax.dot_general(rw_ref[1], h_hi, nt_dims, preferred_element_type=F32))
    lt = lt + jnp.concatenate([rb_ref[...]] * (tm // V7X_LANES), axis=1)
    ei = lax.broadcasted_iota(I32, lt.shape, 0).astype(F32)
    vals, idxs = [], []
    for _ in range(TOP_K):
        m = jnp.max(lt, axis=0, keepdims=True)
        sel = jnp.min(jnp.where(lt == m, ei, float(V7X_LANES)), axis=0, keepdims=True)
        vals.append(m)
        idxs.append(sel)
        lt = jnp.where(ei == sel, -jnp.inf, lt)
    exps = [jnp.exp(v - vals[0]) for v in vals]
    den = exps[0] + exps[1] + exps[2] + exps[3]
    onehots = [ei == s for s in idxs]
    chosen = jnp.zeros(lt.shape, F32)
    for oh in onehots:
        chosen = chosen + jnp.where(oh, 1.0, 0.0)
    chosen_b = chosen.astype(BF16)
    before = jnp.dot(chosen_b, tri_ref[...], preferred_element_type=F32) \
        + jnp.concatenate([run_scr[...]] * (tm // V7X_LANES), axis=1)
    for k in range(TOP_K):
        idx_o[k:k + 1, :] = idxs[k].astype(I32)
        rank_o[k:k + 1, :] = jnp.sum(jnp.where(onehots[k], before, 0.0), axis=0, keepdims=True).astype(I32)
        wts_o[k:k + 1, :] = exps[k] / den
    pad = V7X_SUBLANES - TOP_K
    idx_o[TOP_K:, :] = jnp.zeros((pad, tm), I32)
    rank_o[TOP_K:, :] = jnp.zeros((pad, tm), I32)
    wts_o[TOP_K:, :] = jnp.zeros((pad, tm), F32)
    run_scr[...] = run_scr[...] + jnp.dot(chosen_b, ones_ref[...], preferred_element_type=F32)
    cnt_o[...] = run_scr[...]


def _merge(ya, yb, yf_t, yb_t, u, lat, modv, ssm_norm_w, wb, wo, norm2_w, rw, rb, cnt0, *,
           tm, seg0, tiles_per_seg):
    t, d = lat.shape
    ne = V7X_LANES
    tri = (jnp.arange(tm)[:, None] < jnp.arange(tm)[None, :]).astype(BF16)
    ones = jnp.ones((tm, ne), BF16)
    row = lambda c: pl.BlockSpec((tm, d), lambda i: (i, c))
    col = pl.BlockSpec((d, tm), lambda i: (0, i))
    const = lambda shape: pl.BlockSpec(shape, lambda i: (0,) * len(shape), pipeline_mode=pl.Buffered(1))
    tok = pl.BlockSpec((V7X_SUBLANES, tm), lambda i: (0, i))
    vmem = (2 * (4 * _nbytes((tm, d), BF16) + 2 * _nbytes((d, tm), BF16) + 2 * _nbytes((tm, d), BF16)
                 + 3 * _nbytes((tm, d), F32))
            + 4 * _nbytes((d, d), BF16) + _nbytes((2, ne, d), BF16) + _nbytes((tm, tm), BF16)
            + 8 * _nbytes((tm, d), F32))
    return pl.pallas_call(
        _merge_kernel,
        out_shape=(jax.ShapeDtypeStruct((t, d), F32), jax.ShapeDtypeStruct((t, d), F32),
                   jax.ShapeDtypeStruct((V7X_SUBLANES, t), I32), jax.ShapeDtypeStruct((V7X_SUBLANES, t), I32),
                   jax.ShapeDtypeStruct((V7X_SUBLANES, t), F32), jax.ShapeDtypeStruct((ne, ne), F32)),
        grid=(t // tm,),
        in_specs=[row(0), row(0), col, col, row(U_Z), row(U_GATE), row(U_GATE + 1), row(U_GATE + 2),
                  row(0),
                  pl.BlockSpec((None, V7X_SUBLANES, d), lambda i: (seg0 + i // tiles_per_seg, 0, 0)),
                  const((1, d)), const((N_BRANCH, d, d)), const((d, d)), const((1, d)),
                  const((2, ne, d)), const((ne, ne)), const((tm, tm)), const((tm, ne)), const((ne, ne))],
        out_specs=(row(0), row(0), tok, tok, tok, pl.BlockSpec((ne, ne), lambda i: (0, 0))),
        scratch_shapes=[pltpu.VMEM((ne, ne), F32)],
        compiler_params=_params(("arbitrary",), vmem),
        name="merge",
    )(ya, yb, yf_t, yb_t, u, u, u, u, lat, modv, ssm_norm_w.astype(F32).reshape(1, d), wb, wo,
      norm2_w.astype(F32).reshape(1, d), rw, rb, tri, ones, cnt0)


def _dispatch_kernel(dest_ref, h_ref, *rest):
    xs_ref, sem = rest[-2], rest[-1]
    n = h_ref.shape[0]

    def row_copy(t, dst):
        return pltpu.make_async_copy(h_ref.at[pl.ds(t, 1), :], xs_ref.at[pl.ds(dst, 1), :], sem)

    def start(t, carry):
        for k in range(TOP_K):
            row_copy(t, dest_ref[0, TOP_K * t + k]).start(priority=k % DMA_PRIORITIES)
        return carry

    lax.fori_loop(0, n, start, 0, unroll=DMA_UNROLL)
    for k in range(TOP_K):
        pltpu.make_async_copy(h_ref, xs_ref.at[pl.ds(0, n), :], sem).wait()


def _dispatch(h2, dest_tiles, n_slots, xs_prev=None):
    t, d = h2.shape
    td = T_DMA
    in_specs = [pl.BlockSpec((None, 1, TOP_K * td), lambda i: (i, 0, 0), memory_space=pltpu.SMEM),
                pl.BlockSpec((td, d), lambda i: (i, 0))]
    args = [dest_tiles, h2]
    aliases = {}
    if xs_prev is not None:
        in_specs.append(pl.BlockSpec(memory_space=pl.ANY))
        args.append(xs_prev)
        aliases = {2: 0}
    return pl.pallas_call(
        _dispatch_kernel,
        out_shape=jax.ShapeDtypeStruct((n_slots, d), F32),
        grid=(t // td,),
        in_specs=in_specs,
        out_specs=pl.BlockSpec(memory_space=pl.ANY),
        scratch_shapes=[pltpu.SemaphoreType.DMA(())],
        input_output_aliases=aliases,
        compiler_params=_params(("arbitrary",), 2 * _nbytes((td, d), F32)),
        name="dispatch",
    )(*args)


EXPERT_NB = 256


def _expert_kernel(be_ref, nv_ref, xs_ref, w1_ref, b1_ref, w2_ref, b2_ref, ys_ref, w1_scr, w2_scr, act_scr):
    i = pl.program_id(0)
    d = w1_ref.shape[0]
    dff = w2_ref.shape[0]
    changed = (i == 0) | (be_ref[i] != be_ref[jnp.maximum(i - 1, 0)])

    @pl.when(changed)
    def _():
        rows = 128

        def cast1(r, carry):
            sl = pl.ds(pl.multiple_of(r * rows, rows), rows)
            w1_scr[sl, :] = w1_ref[sl, :].astype(BF16)
            return carry

        def cast2(r, carry):
            sl = pl.ds(pl.multiple_of(r * rows, rows), rows)
            w2_scr[sl, :] = w2_ref[sl, :].astype(BF16)
            return carry

        lax.fori_loop(0, d // rows, cast1, 0)
        lax.fori_loop(0, dff // rows, cast2, 0)

    nv = nv_ref[i]

    @pl.when(nv == 0)
    def _():
        ys_ref[...] = jnp.zeros(ys_ref.shape, F32)

    @pl.when(nv > 0)
    def _():
        ridx = lax.broadcasted_iota(I32, xs_ref.shape, 0)
        x = jnp.where(ridx < nv, xs_ref[...], 0.0).astype(BF16)
        for c in range(dff // EXPERT_NB):
            cg = slice(c * EXPERT_NB, (c + 1) * EXPERT_NB)
            cl = slice(dff + c * EXPERT_NB, dff + (c + 1) * EXPERT_NB)
            gate = jnp.dot(x, w1_scr[:, cg], preferred_element_type=F32) + b1_ref[:, cg]
            lin = jnp.dot(x, w1_scr[:, cl], preferred_element_type=F32) + b1_ref[:, cl]
            gate = jnp.minimum(gate, SWIGLU_LIMIT)
            lin = jnp.clip(lin, -SWIGLU_LIMIT, SWIGLU_LIMIT)
            act = gate * jax.nn.sigmoid(SWIGLU_ALPHA * gate) * (lin + 1.0)
            act_scr[:, cg] = act.astype(BF16)
        ys_ref[...] = jnp.dot(act_scr[...], w2_scr[...], preferred_element_type=F32) + b2_ref[...]


def _experts(xs, block_e, nvalid, w1, b1, w2, b2, *, layer):
    n_slots, dp = xs.shape
    depth, ne, d, two_dff = w1.shape
    dff = two_dff // 2
    n_blocks = n_slots // MOE_BLOCK
    grid_spec = pltpu.PrefetchScalarGridSpec(
        num_scalar_prefetch=2,
        grid=(n_blocks,),
        in_specs=[pl.BlockSpec((MOE_BLOCK, dp), lambda i, be, nv: (i, 0)),
                  pl.BlockSpec((None, None, d, two_dff), lambda i, be, nv: (layer, be[i], 0, 0)),
                  pl.BlockSpec((None, None, 1, two_dff), lambda i, be, nv: (layer, be[i], 0, 0)),
                  pl.BlockSpec((None, None, dff, d), lambda i, be, nv: (layer, be[i], 0, 0)),
                  pl.BlockSpec((None, None, 1, d), lambda i, be, nv: (layer, be[i], 0, 0))],
        out_specs=pl.BlockSpec((MOE_BLOCK, dp), lambda i, be, nv: (i, 0)),
        scratch_shapes=[pltpu.VMEM((d, two_dff), BF16), pltpu.VMEM((dff, d), BF16),
                        pltpu.VMEM((MOE_BLOCK, dff), BF16)],
    )
    vmem = (2 * (_nbytes((d, two_dff), F32) + _nbytes((dff, d), F32) + 2 * _nbytes((MOE_BLOCK, dp), F32))
            + _nbytes((d, two_dff), BF16) + _nbytes((dff, d), BF16) + 6 * _nbytes((MOE_BLOCK, dff), F32))
    return pl.pallas_call(
        _expert_kernel,
        out_shape=jax.ShapeDtypeStruct((n_slots, dp), F32),
        grid_spec=grid_spec,
        compiler_params=_params(("arbitrary",), vmem),
        name="experts",
    )(block_e, nvalid, xs, w1, b1.reshape(depth, ne, 1, two_dff), w2, b2.reshape(depth, ne, 1, d))


def _combine_kernel(dest_ref, dnext_ref, lat_ref, w_ref, mod_ref, fnw_ref, ys_ref, o_ref, buf, sems, *,
                    final_norm):
    i = pl.program_id(0)
    n = lat_ref.shape[0]
    slot = i % 2

    def issue(d_ref, s):
        def start(t, carry):
            for k in range(TOP_K):
                pltpu.make_async_copy(ys_ref.at[pl.ds(d_ref[0, TOP_K * t + k], 1), :],
                                      buf.at[s, k, pl.ds(t, 1), :], sems.at[s]).start(priority=k % DMA_PRIORITIES)
            return carry

        lax.fori_loop(0, n, start, 0, unroll=DMA_UNROLL)

    @pl.when(i == 0)
    def _():
        issue(dest_ref, slot)

    @pl.when(i + 1 < pl.num_programs(0))
    def _():
        issue(dnext_ref, 1 - slot)

    for k in range(TOP_K):
        pltpu.make_async_copy(ys_ref.at[pl.ds(0, n), :], buf.at[slot, k], sems.at[slot]).wait()
    fl = jnp.zeros(lat_ref.shape, F32)
    for k in range(TOP_K):
        fl = fl + w_ref[:, k:k + 1] * buf[slot, k]
    out = lat_ref[...] + mod_ref[5:6, :] * fl
    if final_norm:
        out = _rms(out, fnw_ref[...])
    o_ref[...] = out


def _combine(lat, dest_tiles, wts, modv, fnw, ys, *, seg0, tiles_per_seg, final_norm):
    t, d = lat.shape
    td = T_DMA
    nt = t // td
    vmem = 2 * (2 * _nbytes((td, d), F32) + _nbytes((td, V7X_LANES), F32)) + 2 * _nbytes((TOP_K, td, d), F32) \
        + 4 * _nbytes((td, d), F32)
    return pl.pallas_call(
        functools.partial(_combine_kernel, final_norm=final_norm),
        out_shape=jax.ShapeDtypeStruct((t, d), F32),
        grid=(nt,),
        in_specs=[pl.BlockSpec((None, 1, TOP_K * td), lambda i: (i, 0, 0), memory_space=pltpu.SMEM),
                  pl.BlockSpec((None, 1, TOP_K * td), lambda i: (jnp.minimum(i + 1, nt - 1), 0, 0),
                               memory_space=pltpu.SMEM),
                  pl.BlockSpec((td, d), lambda i: (i, 0)),
                  pl.BlockSpec((td, V7X_SUBLANES), lambda i: (i, 0)),
                  pl.BlockSpec((None, V7X_SUBLANES, d), lambda i: (seg0 + i // tiles_per_seg, 0, 0)),
                  pl.BlockSpec((1, d), lambda i: (0, 0)),
                  pl.BlockSpec(memory_space=pl.ANY)],
        out_specs=pl.BlockSpec((td, d), lambda i: (i, 0)),
        scratch_shapes=[pltpu.VMEM((2, TOP_K, td, d), F32), pltpu.SemaphoreType.DMA((2,))],
        compiler_params=_params(("arbitrary",), vmem),
        name="combine",
    )(dest_tiles, dest_tiles, lat, wts, modv, fnw.astype(F32).reshape(1, d), ys)


def _routing_tables(counts, n_blocks):
    padded = (counts + MOE_BLOCK - 1) // MOE_BLOCK * MOE_BLOCK
    pends = jnp.cumsum(padded)
    pstarts = pends - padded
    bstart = jnp.arange(n_blocks, dtype=I32) * MOE_BLOCK
    block_e = jnp.minimum(jnp.sum((pends[None, :] <= bstart[:, None]).astype(I32), axis=1), N_EXPERTS - 1)
    nvalid = jnp.clip(counts[block_e] - (bstart - pstarts[block_e]), 0, MOE_BLOCK)
    nvalid = jnp.where(bstart < pends[-1], nvalid, 0)
    return pstarts, block_e.astype(I32), nvalid.astype(I32)


def _dest_tiles(idx, rank, pstarts):
    e = idx[:TOP_K]
    start = jnp.sum(jnp.where(e[:, :, None] == jnp.arange(N_EXPERTS)[None, None, :], pstarts[None, None, :], 0),
                    axis=2)
    dest = (start + rank[:TOP_K]).astype(I32).T
    return dest.reshape(-1, 1, TOP_K * T_DMA)


def kernel(x, c, ctx, c_ctx, w_ada, b_ada, norm1_w, norm2_w, w_in, pool_w, pool_scale, conv_w, conv_b,
           dt_bias, a_log, d_skip, ssm_norm_w, w_branch, w_out, router_w, router_b, moe_w1, moe_b1,
           moe_w2, moe_b2, final_norm_w):
    batch, n, d = x.shape
    n_ctx = ctx.shape[1]
    depth = w_ada.shape[0]
    d_inner = SSM_HEADS * SSM_HEAD_DIM
    conv_dim = conv_w.shape[2]
    off_four, off_z, off_xbc = d, 2 * d, 3 * d
    off_dt = off_xbc + conv_dim
    off_gate = off_dt + 2 * SSM_HEADS
    assert batch + 1 <= V7X_SUBLANES and d_inner == d and conv_dim == 2 * d

    lat = x.reshape(batch * n, d).astype(F32)
    cx = ctx.reshape(batch * n_ctx, d).astype(F32)
    cvec = jnp.concatenate([c.astype(F32), c_ctx.astype(F32)[None, :],
                            jnp.zeros((V7X_SUBLANES - batch - 1, d), F32)], axis=0)
    mods = _ada(cvec, w_ada.astype(F32), b_ada.astype(F32))
    seg_ctx = batch
    zero_state = jnp.zeros((batch, 2, d_inner, SSM_STATE), F32)
    zero_cnt = jnp.zeros((V7X_LANES, V7X_LANES), F32)

    for i in range(depth):
        last = i == depth - 1
        modv = mods[i].reshape(V7X_SUBLANES, N_ADA, d)[:batch + 1]
        modv = jnp.pad(modv, ((0, 0), (0, V7X_SUBLANES - N_ADA), (0, 0)))
        wi = w_in[i]
        w_main = jnp.concatenate([wi[:, off_xbc:off_dt], wi[:, :off_z], wi[:, off_z:off_xbc], wi[:, off_gate:]],
                                 axis=1).astype(BF16)
        w_dt = jnp.pad(wi[:, off_dt:off_gate], ((0, 0), (0, V7X_LANES - 2 * SSM_HEADS))).astype(BF16)

        u_l, dtr_l = _inproj(lat, modv, norm1_w[i], w_main, w_dt, tm=TM_LAT, seg0=0, tiles_per_seg=n // TM_LAT)
        u_c, dtr_c = _inproj(cx, modv, norm1_w[i], w_main, w_dt, tm=batch * n_ctx, seg0=seg_ctx, tiles_per_seg=1)

        conv_args = (conv_w[i], conv_b[i], dt_bias[i], a_log[i])
        cv_c = _conv(u_c, dtr_c, *conv_args, tc=n_ctx, seq_len=n_ctx, d_inner=d_inner)
        cv_l = _conv(u_l, dtr_l, *conv_args, tc=TM_ROW, seq_len=n, d_inner=d_inner)
        yf_c, yb_c, st = _ssd(cv_c, d_skip[i], zero_state, batch=batch, seq_len=n_ctx)
        yf_l, yb_l, _ = _ssd(cv_l, d_skip[i], st, batch=batch, seq_len=n)

        ya_l = _pool2d(u_l, pool_w[i], pool_scale[i], seq_len=n)
        fb_l = _fourier_lat(u_l, batch=batch, seq_len=n, d=d)

        wb = w_branch[i].astype(BF16)
        wo = w_out[i].astype(BF16)
        rw_t = jnp.pad(router_w[i].astype(F32).T, ((0, V7X_LANES - N_EXPERTS), (0, 0)))
        rw_hi = rw_t.astype(BF16)
        rw = jnp.stack([rw_hi, (rw_t - rw_hi.astype(F32)).astype(BF16)])
        rb = jnp.pad(router_b[i].astype(F32), (0, V7X_LANES - N_EXPERTS), constant_values=-1e30)
        rb = jnp.broadcast_to(rb[:, None], (V7X_LANES, V7X_LANES))
        merge_w = (ssm_norm_w[i], wb, wo, norm2_w[i], rw, rb)
        lat_m, h2_l, idx_l, rank_l, wts_l, cnt = _merge(
            ya_l, fb_l, yf_l, yb_l, u_l, lat, modv, *merge_w, zero_cnt,
            tm=TM_ROW, seg0=0, tiles_per_seg=n // TM_ROW)
        n_tok = batch * n
        if not last:
            ya_c = _pool1d(u_c, pool_w[i], pool_scale[i], seq_len=n_ctx)
            fb_c = _fourier_ctx(u_c, seq_len=n_ctx, d=d)
            cx_m, h2_c, idx_c, rank_c, wts_c, cnt = _merge(
                ya_c, fb_c, yf_c, yb_c, u_c, cx, modv, *merge_w, cnt,
                tm=batch * n_ctx, seg0=seg_ctx, tiles_per_seg=1)
            n_tok += batch * n_ctx

        n_blocks = -(-(n_tok * TOP_K) // MOE_BLOCK) + N_EXPERTS
        n_slots = n_blocks * MOE_BLOCK
        counts = cnt[:N_EXPERTS, 0].astype(I32)
        pstarts, block_e, nvalid = _routing_tables(counts, n_blocks)
        dest_l = _dest_tiles(idx_l, rank_l, pstarts)
        xs = _dispatch(h2_l, dest_l, n_slots)
        if not last:
            dest_c = _dest_tiles(idx_c, rank_c, pstarts)
            xs = _dispatch(h2_c, dest_c, n_slots, xs_prev=xs)
        ys = _experts(xs, block_e, nvalid, moe_w1, moe_b1, moe_w2, moe_b2, layer=i)
        lat = _combine(lat_m, dest_l, wts_l.T, modv, final_norm_w, ys, seg0=0, tiles_per_seg=n // T_DMA,
                       final_norm=last)
        if not last:
            cx = _combine(cx_m, dest_c, wts_c.T, modv, final_norm_w, ys, seg0=seg_ctx,
                          tiles_per_seg=batch * n_ctx // T_DMA, final_norm=False)
    return lat.reshape(batch, n, d).astype(x.dtype)
```

```python
import functools
import math

import numpy as np
import jax
import jax.numpy as jnp
from jax import lax
from jax.experimental import pallas as pl
from jax.experimental.pallas import tpu as pltpu

F32 = jnp.float32
BF16 = jnp.bfloat16
I32 = jnp.int32
HIGHEST = lax.Precision.HIGHEST

GRID_W = 64
N_ADA = 6
N_BRANCH = 3
MIX_GROUPS = 4
POOL_WINDOWS = (2, 4, 8, 16)
SSM_HEADS = 16
SSM_HEAD_DIM = 64
SSM_GROUPS = 4
SSM_STATE = 128
SSM_CONV = 5
SSM_CHUNK = 128
N_EXPERTS = 32
TOP_K = 4
SWIGLU_ALPHA = 1.702
SWIGLU_LIMIT = 7.0
RMS_EPS = 1e-6
MOE_BLOCK = 256

V7X_VMEM_BYTES = 64 * 2**20
V7X_LANES = 128
V7X_SUBLANES = 8
V7X_BF16_SUBLANES = 16
VMEM_COMPILER_ALLOWANCE = 6 * 2**20

TM_LAT = 1024
INPROJ_COLS = 2048
TM_ROW = 512
T_DMA = 512
DMA_UNROLL = 4
DMA_PRIORITIES = 2
FOURIER_N1 = 64
F1_BLOCK = 16
F2_BLOCK = 8


def _params(semantics, vmem_bytes):
    limit = min(int(vmem_bytes) + VMEM_COMPILER_ALLOWANCE, V7X_VMEM_BYTES - 2 * 2**20)
    return pltpu.CompilerParams(dimension_semantics=semantics, vmem_limit_bytes=limit)


def _nbytes(shape, dtype):
    return int(np.prod(shape)) * jnp.dtype(dtype).itemsize


def _silu(v):
    return v * jax.nn.sigmoid(v)


def _rms(v, w):
    return v * lax.rsqrt(jnp.mean(v * v, axis=-1, keepdims=True) + RMS_EPS) * w


def _store_tile_rows(ref, v):
    n = v.shape[0]
    for j in range(V7X_SUBLANES):
        ref[pl.ds(j, n, stride=V7X_SUBLANES), :] = v[:, j * V7X_LANES:(j + 1) * V7X_LANES]


def _load_tile_rows(ref, n):
    return [ref[pl.ds(j, n, stride=V7X_SUBLANES), :] for j in range(V7X_SUBLANES)]


def _ada_kernel(c_ref, w_ref, b_ref, o_ref):
    o_ref[...] = jnp.dot(_silu(c_ref[...]), w_ref[...], preferred_element_type=F32,
                         precision=HIGHEST) + b_ref[...]


def _ada(cvec, w_ada, b_ada):
    depth, d, nd = w_ada.shape
    return pl.pallas_call(
        _ada_kernel,
        out_shape=jax.ShapeDtypeStruct((depth, V7X_SUBLANES, nd), F32),
        grid=(depth, nd // d),
        in_specs=[pl.BlockSpec((V7X_SUBLANES, d), lambda l, j: (0, 0)),
                  pl.BlockSpec((None, d, d), lambda l, j: (l, 0, j)),
                  pl.BlockSpec((None, 1, d), lambda l, j: (l, 0, j))],
        out_specs=pl.BlockSpec((None, V7X_SUBLANES, d), lambda l, j: (l, 0, j)),
        compiler_params=_params(("parallel", "parallel"), 2 * _nbytes((d, d), F32)),
        name="ada",
    )(cvec, w_ada, b_ada.reshape(depth, 1, nd))


U_XBC, U_POOL, U_FOUR, U_Z, U_GATE = 0, 2, 3, 4, 5
U_BLOCKS = 8


def _inproj_kernel(x_ref, mod_ref, nw_ref, w_ref, wdt_ref, u_ref, dt_ref, h_scr):
    @pl.when(pl.program_id(1) == 0)
    def _():
        h = _rms(x_ref[...], nw_ref[...]) * (1.0 + mod_ref[1:2, :]) + mod_ref[0:1, :]
        hb = h.astype(BF16)
        h_scr[...] = hb
        dt_ref[...] = jnp.dot(hb, wdt_ref[...], preferred_element_type=F32)

    u_ref[...] = jnp.dot(h_scr[...], w_ref[...], preferred_element_type=F32).astype(BF16)


def _inproj(x, modv, nw, w_main, w_dt, *, tm, seg0, tiles_per_seg):
    t, d = x.shape
    tn = INPROJ_COLS
    vmem = 2 * (_nbytes((tm, d), F32) + _nbytes((tm, tn), BF16) + _nbytes((d, tn), BF16)
                + _nbytes((tm, V7X_LANES), F32) + _nbytes((d, V7X_LANES), BF16)) + _nbytes((tm, d), BF16) \
        + _nbytes((tm, tn), F32)
    return pl.pallas_call(
        _inproj_kernel,
        out_shape=(jax.ShapeDtypeStruct((t, U_BLOCKS * d), BF16),
                   jax.ShapeDtypeStruct((t, V7X_LANES), F32)),
        grid=(t // tm, U_BLOCKS * d // tn),
        in_specs=[pl.BlockSpec((tm, d), lambda i, j: (i, 0)),
                  pl.BlockSpec((None, V7X_SUBLANES, d), lambda i, j: (seg0 + i // tiles_per_seg, 0, 0)),
                  pl.BlockSpec((1, d), lambda i, j: (0, 0)),
                  pl.BlockSpec((d, tn), lambda i, j: (0, j)),
                  pl.BlockSpec((d, V7X_LANES), lambda i, j: (0, 0))],
        out_specs=(pl.BlockSpec((tm, tn), lambda i, j: (i, j)),
                   pl.BlockSpec((tm, V7X_LANES), lambda i, j: (i, 0))),
        scratch_shapes=[pltpu.VMEM((tm, d), BF16)],
        compiler_params=_params(("parallel", "arbitrary"), vmem),
        name="inproj",
    )(x, modv, nw.reshape(1, d), w_main, w_dt)


CONV_HALO = V7X_BF16_SUBLANES
CONV_RB = 128
CONV_CB = 128


def _conv_kernel(cur_ref, prev_ref, next_ref, dtraw_ref, cw_ref, cb_ref, dtb_ref, alog_ref,
                 xt_ref, bc_ref, ct_ref, aux_ref, acum_ref, xp_scr, *, tc, tiles_per_seq, d_inner):
    p = pl.program_id(0) % tiles_per_seq
    xp_scr[0:CONV_HALO, :] = jnp.where(p > 0, prev_ref[...].astype(F32), 0.0)
    xp_scr[CONV_HALO:CONV_HALO + tc, :] = cur_ref[...].astype(F32)
    xp_scr[CONV_HALO + tc:, :] = jnp.where(p < tiles_per_seq - 1, next_ref[...].astype(F32), 0.0)
    n_bc = SSM_GROUPS * SSM_STATE
    for rb in range(tc // CONV_RB):
        rows = slice(rb * CONV_RB, (rb + 1) * CONV_RB)
        for cb in range(cur_ref.shape[1] // CONV_CB):
            c0 = cb * CONV_CB
            cols = slice(c0, c0 + CONV_CB)
            acc = jnp.broadcast_to(cb_ref[:, cols], (CONV_RB, CONV_CB))
            for k in range(SSM_CONV):
                r0 = CONV_HALO - SSM_CONV // 2 + k + rb * CONV_RB
                acc = acc + cw_ref[k:k + 1, cols] * xp_scr[r0:r0 + CONV_RB, cols]
            v = _silu(acc)
            if c0 < d_inner:
                xt_ref[cols, rows] = v.T.astype(BF16)
            elif c0 < d_inner + n_bc:
                bc_ref[rows, c0 - d_inner:c0 - d_inner + CONV_CB] = v.astype(BF16)
            else:
                bc_ref[rows, c0 - d_inner:c0 - d_inner + CONV_CB] = v.astype(BF16)
                cc = c0 - d_inner - n_bc
                ct_ref[cc:cc + CONV_CB, rows] = v.T.astype(BF16)

    dtr = dtraw_ref[...] + dtb_ref[...]
    dt = jnp.maximum(dtr, 0.0) + jnp.log1p(jnp.exp(-jnp.abs(dtr)))
    dta = dt * (-jnp.exp(alog_ref[...]))
    dt_t = dt.T
    dta_t = dta.T
    ki = lax.broadcasted_iota(I32, (tc, tc), 0)
    li = lax.broadcasted_iota(I32, (tc, tc), 1)
    same = (ki // SSM_CHUNK) == (li // SSM_CHUNK)
    one, zero = jnp.float32(1.0), jnp.float32(0.0)
    cumf = jnp.dot(dta_t, jnp.where(same & (ki <= li), one, zero), preferred_element_type=F32,
                   precision=HIGHEST)
    cumb = jnp.dot(dta_t, jnp.where(same & (ki >= li), one, zero), preferred_element_type=F32,
                   precision=HIGHEST)
    tot = jnp.dot(dta_t, jnp.where(same, one, zero), preferred_element_type=F32, precision=HIGHEST)
    nh = SSM_HEADS
    aux_ref[0:2 * nh, :] = dt_t[0:2 * nh, :]
    aux_ref[2 * nh:3 * nh, :] = cumf[0:nh, :]
    aux_ref[3 * nh:4 * nh, :] = cumb[nh:2 * nh, :]
    aux_ref[4 * nh:6 * nh, :] = tot[0:2 * nh, :]
    aux_ref[6 * nh:, :] = jnp.zeros((V7X_LANES - 6 * nh, tc), F32)
    hrow = lax.broadcasted_iota(I32, (V7X_LANES, tc), 0)
    acum_ref[...] = jnp.where(hrow < nh, cumf, cumb).T


def _conv(u, dt_raw, conv_w, conv_b, dt_bias, a_log, *, tc, seq_len, d_inner):
    t = u.shape[0]
    conv_dim = conv_w.shape[1]
    n_bc = SSM_GROUPS * SSM_STATE
    nt = t // tc
    hb = tc // CONV_HALO
    last_halo = t // CONV_HALO - 1
    cw = jnp.pad(conv_w.astype(F32), ((0, V7X_SUBLANES - SSM_CONV), (0, 0)))
    pad32 = lambda v: jnp.pad(v.astype(F32).reshape(1, -1), ((0, 0), (0, V7X_LANES - 2 * SSM_HEADS)))
    vmem = (2 * (_nbytes((tc, conv_dim), BF16) * 2 + _nbytes((tc, V7X_LANES), F32) * 3
                 + _nbytes((d_inner + 2 * n_bc, tc), BF16))
            + _nbytes((tc + 2 * CONV_HALO, conv_dim), F32) + 4 * _nbytes((tc, tc), F32))
    kern = functools.partial(_conv_kernel, tc=tc, tiles_per_seq=seq_len // tc, d_inner=d_inner)
    return pl.pallas_call(
        kern,
        out_shape=(jax.ShapeDtypeStruct((d_inner, t), BF16),
                   jax.ShapeDtypeStruct((t, 2 * n_bc), BF16),
                   jax.ShapeDtypeStruct((n_bc, t), BF16),
                   jax.ShapeDtypeStruct((V7X_LANES, t), F32),
                   jax.ShapeDtypeStruct((t, V7X_LANES), F32)),
        grid=(nt,),
        in_specs=[pl.BlockSpec((tc, conv_dim), lambda i: (i, U_XBC)),
                  pl.BlockSpec((CONV_HALO, conv_dim), lambda i: (jnp.maximum(i * hb - 1, 0), U_XBC)),
                  pl.BlockSpec((CONV_HALO, conv_dim), lambda i: (jnp.minimum((i + 1) * hb, last_halo), U_XBC)),
                  pl.BlockSpec((tc, V7X_LANES), lambda i: (i, 0)),
                  pl.BlockSpec((V7X_SUBLANES, conv_dim), lambda i: (0, 0)),
                  pl.BlockSpec((1, conv_dim), lambda i: (0, 0)),
                  pl.BlockSpec((1, V7X_LANES), lambda i: (0, 0)),
                  pl.BlockSpec((1, V7X_LANES), lambda i: (0, 0))],
        out_specs=(pl.BlockSpec((d_inner, tc), lambda i: (0, i)),
                   pl.BlockSpec((tc, 2 * n_bc), lambda i: (i, 0)),
                   pl.BlockSpec((n_bc, tc), lambda i: (0, i)),
                   pl.BlockSpec((V7X_LANES, tc), lambda i: (0, i)),
                   pl.BlockSpec((tc, V7X_LANES), lambda i: (i, 0))),
        scratch_shapes=[pltpu.VMEM((tc + 2 * CONV_HALO, conv_dim), F32)],
        compiler_params=_params(("parallel",), vmem),
        name="conv",
    )(u, u, u, dt_raw, cw, conv_b.astype(F32).reshape(1, -1), pad32(dt_bias), pad32(a_log))


def _ssd_direction(d, rev, xt_ref, bc_ref, ct_ref, aux_ref, acum_ref, st_scr, y_ref, dsk_ref):
    q = SSM_CHUNK
    hd = SSM_HEAD_DIM
    nh = SSM_HEADS
    si = lax.broadcasted_iota(I32, (q, q), 0)
    li = lax.broadcasted_iota(I32, (q, q), 1)
    mask = (li <= si) if rev else (li >= si)
    hpg = nh // SSM_GROUPS
    for g in range(SSM_GROUPS):
        b_g = bc_ref[:, g * SSM_STATE:(g + 1) * SSM_STATE]
        c_g = bc_ref[:, (SSM_GROUPS + g) * SSM_STATE:(SSM_GROUPS + g + 1) * SSM_STATE]
        ct_g = ct_ref[g * SSM_STATE:(g + 1) * SSM_STATE, :].astype(F32)
        cbt = lax.dot_general(b_g, c_g, (((1,), (1,)), ((), ())), preferred_element_type=F32)
        for hh in range(hpg):
            h = g * hpg + hh
            r = d * nh + h
            dt_row = aux_ref[r:r + 1, :]
            cum_row = aux_ref[2 * nh + r:2 * nh + r + 1, :]
            tot_row = aux_ref[4 * nh + r:4 * nh + r + 1, :]
            cum_col = acum_ref[:, r:r + 1]
            decay = jnp.exp(jnp.where(mask, cum_row - cum_col, -jnp.inf))
            w_top = (cbt * decay).astype(BF16)
            w_bot = (ct_g * jnp.exp(cum_row)).astype(BF16)
            w = jnp.concatenate([w_top, w_bot], axis=0)
            rows = slice(h * hd, (h + 1) * hd)
            x_t = xt_ref[rows, :].astype(F32)
            xd = x_t * dt_row
            s_old = st_scr[d, rows, :]
            lhs = jnp.concatenate([xd.astype(BF16), s_old.astype(BF16)], axis=1)
            y_t = jnp.dot(lhs, w, preferred_element_type=F32)
            if not rev:
                y_t = y_t + dsk_ref[h] * x_t
            y_ref[rows, :] = y_t.astype(y_ref.dtype)
            xdw = (xd * jnp.exp(tot_row - cum_row)).astype(BF16)
            st_scr[d, rows, :] = s_old * jnp.exp(tot_row) + jnp.dot(xdw, b_g, preferred_element_type=F32)


def _ssd_kernel(dsk_ref, xt_f, bc_f, ct_f, aux_f, ac_f, xt_b, bc_b, ct_b, aux_b, ac_b, init_ref,
                yf_ref, yb_ref, fin_ref, st_scr):
    c = pl.program_id(1)

    @pl.when(c == 0)
    def _():
        st_scr[...] = init_ref[...]

    _ssd_direction(0, False, xt_f, bc_f, ct_f, aux_f, ac_f, st_scr, yf_ref, dsk_ref)
    _ssd_direction(1, True, xt_b, bc_b, ct_b, aux_b, ac_b, st_scr, yb_ref, dsk_ref)

    @pl.when(c == pl.num_programs(1) - 1)
    def _():
        fin_ref[...] = st_scr[...]


def _ssd(conv_out, d_skip, init_state, *, batch, seq_len):
    xt, bc, ct, aux, acum = conv_out
    d_inner, t = xt.shape
    n_bc = ct.shape[0]
    q = SSM_CHUNK
    nc = seq_len // q
    fwd = lambda b, c: b * nc + c
    bwd = lambda b, c: b * nc + nc - 1 - c

    def specs(pos):
        return [pl.BlockSpec((d_inner, q), lambda b, c: (0, pos(b, c))),
                pl.BlockSpec((q, 2 * n_bc), lambda b, c: (pos(b, c), 0)),
                pl.BlockSpec((n_bc, q), lambda b, c: (0, pos(b, c))),
                pl.BlockSpec((V7X_LANES, q), lambda b, c: (0, pos(b, c))),
                pl.BlockSpec((q, V7X_LANES), lambda b, c: (pos(b, c), 0))]

    st_spec = pl.BlockSpec((None, 2, d_inner, SSM_STATE), lambda b, c: (b, 0, 0, 0))
    st_bytes = _nbytes((2, d_inner, SSM_STATE), F32)
    vmem = 5 * st_bytes + 4 * (_nbytes((d_inner, q), BF16) * 2 + _nbytes((q, 2 * n_bc), BF16)
                               + _nbytes((n_bc, q), BF16) + 2 * _nbytes((q, V7X_LANES), F32))
    return pl.pallas_call(
        _ssd_kernel,
        out_shape=(jax.ShapeDtypeStruct((d_inner, t), BF16),
                   jax.ShapeDtypeStruct((d_inner, t), BF16),
                   jax.ShapeDtypeStruct((batch, 2, d_inner, SSM_STATE), F32)),
        grid=(batch, nc),
        in_specs=[pl.BlockSpec(memory_space=pltpu.SMEM)] + specs(fwd) + specs(bwd) + [st_spec],
        out_specs=(pl.BlockSpec((d_inner, q), lambda b, c: (0, fwd(b, c))),
                   pl.BlockSpec((d_inner, q), lambda b, c: (0, bwd(b, c))),
                   st_spec),
        scratch_shapes=[pltpu.VMEM((2, d_inner, SSM_STATE), F32)],
        compiler_params=_params(("arbitrary", "arbitrary"), vmem),
        name="ssd",
    )(d_skip.astype(F32), xt, bc, ct, aux, acum, xt, bc, ct, aux, acum, init_state)


def _pool_body(slabs, cur_ref, mask_refs, icnt_ref, pw_ref, ps_ref, o_ref):
    gw = cur_ref.shape[1] // MIX_GROUPS
    for g in range(MIX_GROUPS):
        cols = slice(g * gw, (g + 1) * gw)
        xin = slabs(g, cols)
        s = jnp.dot(mask_refs[g][...], xin, preferred_element_type=F32)
        inv = icnt_ref[:, g * V7X_LANES:(g + 1) * V7X_LANES]
        p = s * jnp.concatenate([inv] * (gw // V7X_LANES), axis=1)
        dlt = p - cur_ref[:, cols].astype(F32)
        y = jnp.dot(dlt.astype(BF16), pw_ref[g], preferred_element_type=F32) * ps_ref[:, cols]
        o_ref[:, cols] = y.astype(BF16)


def _pool2d_kernel(prev_ref, cur_ref, next_ref, m0, m1, m2, m3, icnt_ref, pw_ref, ps_ref, o_ref):
    tile = cur_ref.shape[0]

    def slabs(g, cols):
        w = POOL_WINDOWS[g]
        up, dn = w // 2, w - w // 2 - 1
        parts = [prev_ref[tile - GRID_W * up:, cols], cur_ref[:, cols]]
        if dn:
            parts.append(next_ref[:GRID_W * dn, cols])
        return jnp.concatenate(parts, axis=0)

    _pool_body(slabs, cur_ref, (m0, m1, m2, m3), icnt_ref, pw_ref, ps_ref, o_ref)


def _pool1d_kernel(cur_ref, m0, m1, m2, m3, icnt_ref, pw_ref, ps_ref, o_ref):
    _pool_body(lambda g, cols: cur_ref[:, cols], cur_ref, (m0, m1, m2, m3), icnt_ref, pw_ref, ps_ref, o_ref)


def _window_ok(out_pos, in_pos, w):
    lo = out_pos[:, None] - w // 2
    return (in_pos[None, :] >= lo) & (in_pos[None, :] < lo + w)


def _pool_tables_2d(tile):
    rows = tile // GRID_W
    t = jnp.arange(tile)
    masks, invs = [], []
    for w in POOL_WINDOWS:
        up, dn = w // 2, w - w // 2 - 1
        u = jnp.arange(tile + GRID_W * (up + dn))
        r_in, c_in = u // GRID_W - up, u % GRID_W
        base = _window_ok(t // GRID_W, r_in, w) & _window_ok(t % GRID_W, c_in, w)
        var = jnp.stack([base & (r_in >= 0)[None, :], base, base & (r_in < rows)[None, :]])
        masks.append(var.astype(BF16))
        invs.append(1.0 / jnp.sum(var.astype(F32), axis=2))
    inv = jnp.stack(invs, axis=1)
    inv = jnp.broadcast_to(inv[:, :, :, None], inv.shape + (V7X_LANES,))
    inv = jnp.transpose(inv, (0, 2, 1, 3)).reshape(3, tile, MIX_GROUPS * V7X_LANES)
    return masks, inv


def _pool_tables_1d(n):
    t = jnp.arange(n)
    masks, invs = [], []
    for w in POOL_WINDOWS:
        m = _window_ok(t, t, w)
        masks.append(m.astype(BF16))
        invs.append(1.0 / jnp.sum(m.astype(F32), axis=1))
    inv = jnp.stack(invs, axis=0)
    inv = jnp.broadcast_to(inv[:, :, None], inv.shape + (V7X_LANES,))
    return masks, jnp.transpose(inv, (1, 0, 2)).reshape(n, MIX_GROUPS * V7X_LANES)


def _pool2d(u, pool_w, pool_scale, *, seq_len):
    t = u.shape[0]
    d = pool_scale.shape[0]
    tile = TM_ROW
    nt, per_img = t // tile, seq_len // tile
    masks, inv = _pool_tables_2d(tile)

    def variant(i):
        p = i % per_img
        return jnp.where(p == 0, 0, jnp.where(p == per_img - 1, 2, 1))

    gw = d // MIX_GROUPS
    vmem = 2 * (4 * _nbytes((tile, d), BF16) + sum(_nbytes(m.shape[1:], BF16) for m in masks)
                + _nbytes(inv.shape[1:], F32) + _nbytes((MIX_GROUPS, gw, gw), BF16))
    return pl.pallas_call(
        _pool2d_kernel,
        out_shape=jax.ShapeDtypeStruct((t, d), BF16),
        grid=(nt,),
        in_specs=[pl.BlockSpec((tile, d), lambda i: (jnp.maximum(i - 1, 0), U_POOL)),
                  pl.BlockSpec((tile, d), lambda i: (i, U_POOL)),
                  pl.BlockSpec((tile, d), lambda i: (jnp.minimum(i + 1, nt - 1), U_POOL))]
                 + [pl.BlockSpec((None,) + m.shape[1:], lambda i: (variant(i), 0, 0)) for m in masks]
                 + [pl.BlockSpec((None,) + inv.shape[1:], lambda i: (variant(i), 0, 0)),
                    pl.BlockSpec((MIX_GROUPS, gw, gw), lambda i: (0, 0, 0)),
                    pl.BlockSpec((1, d), lambda i: (0, 0))],
        out_specs=pl.BlockSpec((tile, d), lambda i: (i, 0)),
        compiler_params=_params(("parallel",), vmem),
        name="pool2d",
    )(u, u, u, *masks, inv, pool_w.astype(BF16), pool_scale.astype(F32).reshape(1, d))


def _pool1d(u, pool_w, pool_scale, *, seq_len):
    t = u.shape[0]
    d = pool_scale.shape[0]
    masks, inv = _pool_tables_1d(seq_len)
    gw = d // MIX_GROUPS
    vmem = 2 * (2 * _nbytes((seq_len, d), BF16) + 4 * _nbytes((seq_len, seq_len), BF16)
                + _nbytes(inv.shape, F32) + _nbytes((MIX_GROUPS, gw, gw), BF16))
    return pl.pallas_call(
        _pool1d_kernel,
        out_shape=jax.ShapeDtypeStruct((t, d), BF16),
        grid=(t // seq_len,),
        in_specs=[pl.BlockSpec((seq_len, d), lambda i: (i, U_POOL))]
                 + [pl.BlockSpec(m.shape, lambda i: (0, 0)) for m in masks]
                 + [pl.BlockSpec(inv.shape, lambda i: (0, 0)),
                    pl.BlockSpec((MIX_GROUPS, gw, gw), lambda i: (0, 0, 0)),
                    pl.BlockSpec((1, d), lambda i: (0, 0))],
        out_specs=pl.BlockSpec((seq_len, d), lambda i: (i, 0)),
        compiler_params=_params(("parallel",), vmem),
        name="pool1d",
    )(u, *masks, inv, pool_w.astype(BF16), pool_scale.astype(F32).reshape(1, d))


def _dft_tables(n_pos, gw):
    n1, n2 = FOURIER_N1, n_pos // FOURIER_N1
    c = np.arange(gw)
    ang = 2.0 * np.pi * ((c[:, None] * c[None, :]) % gw) / gw
    cs = np.concatenate([np.cos(ang), -np.sin(ang)], axis=1)
    a = np.arange(n1)
    ang1 = 2.0 * np.pi * ((a[:, None] * a[None, :]) % n1) / n1
    wc, ws = np.cos(ang1), np.sin(ang1)
    w1 = np.block([[wc, ws], [-ws, wc]])
    k = a[:, None, None] + n1 * np.arange(n2)[None, :, None]
    ang2 = 2.0 * np.pi * ((k * np.arange(n2)[None, None, :]) % n_pos) / n_pos
    m2 = np.concatenate([np.cos(ang2), np.sin(ang2)], axis=2)
    to_bf = lambda v: jnp.asarray(v, dtype=F32).astype(BF16)
    return to_bf(cs), to_bf(w1), to_bf(m2)


def _f1_kernel(u_ref, cs_ref, w1_ref, o_ref, z_scr):
    nb, n1, d = u_ref.shape
    gw = d // MIX_GROUPS
    u = u_ref[...].reshape(nb * n1, d)
    for g in range(MIX_GROUPS):
        z = jnp.dot(u[:, g * gw:(g + 1) * gw], cs_ref[...], preferred_element_type=F32)
        z_scr[:, g * gw:(g + 1) * gw] = z[:, :gw].astype(BF16)
        z_scr[:, d + g * gw:d + (g + 1) * gw] = z[:, gw:].astype(BF16)
    for j in range(nb):
        zz = jnp.concatenate([z_scr[j * n1:(j + 1) * n1, :d], z_scr[j * n1:(j + 1) * n1, d:]], axis=0)
        o_ref[j] = jnp.dot(w1_ref[...], zz, preferred_element_type=F32).astype(BF16)


def _f2_kernel(a_ref, m2_ref, o_ref, *, scale):
    for j in range(a_ref.shape[0]):
        o_ref[j] = (jnp.dot(m2_ref[j], a_ref[j], preferred_element_type=F32) * scale).astype(BF16)


def _fourier_lat(u, *, batch, seq_len, d):
    n1, n2 = FOURIER_N1, seq_len // FOURIER_N1
    gw = d // MIX_GROUPS
    cs, w1, m2 = _dft_tables(seq_len, gw)
    ut = u[:, U_FOUR * d:(U_FOUR + 1) * d].reshape(batch, n1, n2, d).transpose(0, 2, 1, 3)
    nb = F1_BLOCK
    vmem1 = 2 * (_nbytes((nb, n1, d), BF16) + _nbytes((nb, 2 * n1, d), BF16)) + _nbytes((nb * n1, 2 * d), BF16) \
        + 2 * _nbytes((nb * n1, 2 * gw), F32)
    a = pl.pallas_call(
        _f1_kernel,
        out_shape=jax.ShapeDtypeStruct((batch, n2, 2 * n1, d), BF16),
        grid=(batch, n2 // nb),
        in_specs=[pl.BlockSpec((None, nb, n1, d), lambda b, j: (b, j, 0, 0)),
                  pl.BlockSpec(cs.shape, lambda b, j: (0, 0)),
                  pl.BlockSpec(w1.shape, lambda b, j: (0, 0))],
        out_specs=pl.BlockSpec((None, nb, 2 * n1, d), lambda b, j: (b, j, 0, 0)),
        scratch_shapes=[pltpu.VMEM((nb * n1, 2 * d), BF16)],
        compiler_params=_params(("parallel", "parallel"), vmem1),
        name="fourier_stage1",
    )(ut, cs, w1)
    at = a.reshape(batch, n2, 2, n1, d).transpose(0, 3, 2, 1, 4).reshape(batch, n1, 2 * n2, d)
    kb = F2_BLOCK
    vmem2 = 2 * (_nbytes((kb, 2 * n2, d), BF16) + _nbytes((kb, n2, 2 * n2), BF16) + _nbytes((kb, n2, d), BF16))
    x2 = pl.pallas_call(
        functools.partial(_f2_kernel, scale=1.0 / math.sqrt(seq_len * gw)),
        out_shape=jax.ShapeDtypeStruct((batch, n1, n2, d), BF16),
        grid=(batch, n1 // kb),
        in_specs=[pl.BlockSpec((None, kb, 2 * n2, d), lambda b, j: (b, j, 0, 0)),
                  pl.BlockSpec((kb, n2, 2 * n2), lambda b, j: (j, 0, 0))],
        out_specs=pl.BlockSpec((None, kb, n2, d), lambda b, j: (b, j, 0, 0)),
        compiler_params=_params(("parallel", "parallel"), vmem2),
        name="fourier_stage2",
    )(at, m2)
    return x2.transpose(0, 2, 1, 3).reshape(batch * seq_len, d)


def _fourier_ctx_kernel(u_ref, cs_ref, cn_ref, o_ref, *, scale):
    d = u_ref.shape[1]
    gw = d // MIX_GROUPS
    zr, zi = [], []
    for g in range(MIX_GROUPS):
        z = jnp.dot(u_ref[:, g * gw:(g + 1) * gw], cs_ref[...], preferred_element_type=F32)
        zr.append(z[:, :gw].astype(BF16))
        zi.append(z[:, gw:].astype(BF16))
    zz = jnp.concatenate([jnp.concatenate(zr, axis=1), jnp.concatenate(zi, axis=1)], axis=0)
    o_ref[...] = (jnp.dot(cn_ref[...], zz, preferred_element_type=F32) * scale).astype(BF16)


def _fourier_ctx(u, *, seq_len, d):
    t = u.shape[0]
    gw = d // MIX_GROUPS
    cs, _, _ = _dft_tables(FOURIER_N1 * 2, gw)
    n = np.arange(seq_len)
    ang = 2.0 * np.pi * ((n[:, None] * n[None, :]) % seq_len) / seq_len
    cn = jnp.asarray(np.concatenate([np.cos(ang), np.sin(ang)], axis=1), dtype=F32).astype(BF16)
    vmem = 2 * (2 * _nbytes((seq_len, d), BF16) + _nbytes(cs.shape, BF16) + _nbytes(cn.shape, BF16)) \
        + 4 * _nbytes((seq_len, d), F32)
    return pl.pallas_call(
        functools.partial(_fourier_ctx_kernel, scale=1.0 / math.sqrt(seq_len * gw)),
        out_shape=jax.ShapeDtypeStruct((t, d), BF16),
        grid=(t // seq_len,),
        in_specs=[pl.BlockSpec((seq_len, d), lambda i: (i, U_FOUR)),
                  pl.BlockSpec(cs.shape, lambda i: (0, 0)),
                  pl.BlockSpec(cn.shape, lambda i: (0, 0))],
        out_specs=pl.BlockSpec((seq_len, d), lambda i: (i, 0)),
        compiler_params=_params(("parallel",), vmem),
        name="fourier_ctx",
    )(u, cs, cn)


def _merge_kernel(ya_ref, yb_ref, yf_ref, ybw_ref, z_ref, g0_ref, g1_ref, g2_ref, lat_ref, mod_ref,
                  snw_ref, wb_ref, wo_ref, n2w_ref, rw_ref, rb_ref, tri_ref, ones_ref, cnt0_ref,
                  lat_o, h2_o, idx_o, rank_o, wts_o, cnt_o, run_scr):
    tm = lat_ref.shape[0]

    @pl.when(pl.program_id(0) == 0)
    def _():
        run_scr[...] = cnt0_ref[...]

    y_ssm = (yf_ref[...].astype(F32) + ybw_ref[...].astype(F32)).T
    yc = _rms(y_ssm * _silu(z_ref[...].astype(F32)), snw_ref[...])
    merged = jnp.zeros(lat_ref.shape, F32)
    for k, (y, g_ref) in enumerate(((ya_ref[...], g0_ref), (yb_ref[...], g1_ref), (yc.astype(BF16), g2_ref))):
        proj = jnp.dot(y, wb_ref[k], preferred_element_type=F32)
        merged = merged + jax.nn.sigmoid(g_ref[...].astype(F32)) * proj
    ol = jnp.dot(merged.astype(BF16), wo_ref[...], preferred_element_type=F32)
    lat = lat_ref[...] + mod_ref[2:3, :] * ol
    lat_o[...] = lat
    h2 = _rms(lat, n2w_ref[...]) * (1.0 + mod_ref[4:5, :]) + mod_ref[3:4, :]
    _store_tile_rows(h2_o, h2)

    h_hi = h2.astype(BF16)
    h_lo = (h2 - h_hi.astype(F32)).astype(BF16)
    nt_dims = (((1,), (1,)), ((), ()))
    lt = (lax.dot_general(rw_ref[0], h_hi, nt_dims, preferred_element_type=F32)
          + lax.dot_general(rw_ref[0], h_lo, nt_dims, preferred_element_type=F32)
          + lax.dot_general(rw_ref[1], h_hi, nt_dims, preferred_element_type=F32))
    lt = lt + jnp.concatenate([rb_ref[...]] * (tm // V7X_LANES), axis=1)
    ei = lax.broadcasted_iota(I32, lt.shape, 0).astype(F32)
    vals, idxs = [], []
    for _ in range(TOP_K):
        m = jnp.max(lt, axis=0, keepdims=True)
        sel = jnp.min(jnp.where(lt == m, ei, float(V7X_LANES)), axis=0, keepdims=True)
        vals.append(m)
        idxs.append(sel)
        lt = jnp.where(ei == sel, -jnp.inf, lt)
    exps = [jnp.exp(v - vals[0]) for v in vals]
    den = exps[0] + exps[1] + exps[2] + exps[3]
    onehots = [ei == s for s in idxs]
    chosen = jnp.zeros(lt.shape, F32)
    for oh in onehots:
        chosen = chosen + jnp.where(oh, 1.0, 0.0)
    chosen_b = chosen.astype(BF16)
    before = jnp.dot(chosen_b, tri_ref[...], preferred_element_type=F32) \
        + jnp.concatenate([run_scr[...]] * (tm // V7X_LANES), axis=1)
    for k in range(TOP_K):
        idx_o[k:k + 1, :] = idxs[k].astype(I32)
        rank_o[k:k + 1, :] = jnp.sum(jnp.where(onehots[k], before, 0.0), axis=0, keepdims=True).astype(I32)
        wts_o[k:k + 1, :] = exps[k] / den
    pad = V7X_SUBLANES - TOP_K
    idx_o[TOP_K:, :] = jnp.zeros((pad, tm), I32)
    rank_o[TOP_K:, :] = jnp.zeros((pad, tm), I32)
    wts_o[TOP_K:, :] = jnp.zeros((pad, tm), F32)
    run_scr[...] = run_scr[...] + jnp.dot(chosen_b, ones_ref[...], preferred_element_type=F32)
    cnt_o[...] = run_scr[...]


def _merge(ya, yb, yf_t, yb_t, u, lat, modv, ssm_norm_w, wb, wo, norm2_w, rw, rb, cnt0, *,
           tm, seg0, tiles_per_seg):
    t, d = lat.shape
    ne = V7X_LANES
    tri = (jnp.arange(tm)[:, None] < jnp.arange(tm)[None, :]).astype(BF16)
    ones = jnp.ones((tm, ne), BF16)
    row = lambda c: pl.BlockSpec((tm, d), lambda i: (i, c))
    col = pl.BlockSpec((d, tm), lambda i: (0, i))
    const = lambda shape: pl.BlockSpec(shape, lambda i: (0,) * len(shape), pipeline_mode=pl.Buffered(1))
    tok = pl.BlockSpec((V7X_SUBLANES, tm), lambda i: (0, i))
    vmem = (2 * (4 * _nbytes((tm, d), BF16) + 2 * _nbytes((d, tm), BF16) + 2 * _nbytes((tm, d), BF16)
                 + 3 * _nbytes((tm, d), F32))
            + 4 * _nbytes((d, d), BF16) + _nbytes((2, ne, d), BF16) + _nbytes((tm, tm), BF16)
            + 8 * _nbytes((tm, d), F32))
    return pl.pallas_call(
        _merge_kernel,
        out_shape=(jax.ShapeDtypeStruct((t, d), F32), jax.ShapeDtypeStruct((t * V7X_SUBLANES, V7X_LANES), F32),
                   jax.ShapeDtypeStruct((V7X_SUBLANES, t), I32), jax.ShapeDtypeStruct((V7X_SUBLANES, t), I32),
                   jax.ShapeDtypeStruct((V7X_SUBLANES, t), F32), jax.ShapeDtypeStruct((ne, ne), F32)),
        grid=(t // tm,),
        in_specs=[row(0), row(0), col, col, row(U_Z), row(U_GATE), row(U_GATE + 1), row(U_GATE + 2),
                  row(0),
                  pl.BlockSpec((None, V7X_SUBLANES, d), lambda i: (seg0 + i // tiles_per_seg, 0, 0)),
                  const((1, d)), const((N_BRANCH, d, d)), const((d, d)), const((1, d)),
                  const((2, ne, d)), const((ne, ne)), const((tm, tm)), const((tm, ne)), const((ne, ne))],
        out_specs=(row(0), pl.BlockSpec((tm * V7X_SUBLANES, V7X_LANES), lambda i: (i, 0)), tok, tok, tok,
                   pl.BlockSpec((ne, ne), lambda i: (0, 0))),
        scratch_shapes=[pltpu.VMEM((ne, ne), F32)],
        compiler_params=_params(("arbitrary",), vmem),
        name="merge",
    )(ya, yb, yf_t, yb_t, u, u, u, u, lat, modv, ssm_norm_w.astype(F32).reshape(1, d), wb, wo,
      norm2_w.astype(F32).reshape(1, d), rw, rb, tri, ones, cnt0)


def _tile_row(r):
    return pl.ds(pl.multiple_of(r * V7X_SUBLANES, V7X_SUBLANES), V7X_SUBLANES)


def _dispatch_kernel(dest_ref, h_ref, *rest):
    xs_ref, sem = rest[-2], rest[-1]
    rows = h_ref.shape[0]
    n = rows // V7X_SUBLANES

    def start(t, carry):
        for k in range(TOP_K):
            pltpu.make_async_copy(h_ref.at[_tile_row(t), :], xs_ref.at[_tile_row(dest_ref[0, TOP_K * t + k]), :],
                                  sem).start(priority=k % DMA_PRIORITIES)
        return carry

    lax.fori_loop(0, n, start, 0, unroll=DMA_UNROLL)
    for k in range(TOP_K):
        pltpu.make_async_copy(h_ref, xs_ref.at[pl.ds(0, rows), :], sem).wait()


def _dispatch(h2, dest_tiles, n_slots, xs_prev=None):
    t, d = h2.shape[0] // V7X_SUBLANES, V7X_LANES
    td = T_DMA
    in_specs = [pl.BlockSpec((None, 1, TOP_K * td), lambda i: (i, 0, 0), memory_space=pltpu.SMEM),
                pl.BlockSpec((td * V7X_SUBLANES, d), lambda i: (i, 0))]
    args = [dest_tiles, h2]
    aliases = {}
    if xs_prev is not None:
        in_specs.append(pl.BlockSpec(memory_space=pl.ANY))
        args.append(xs_prev)
        aliases = {2: 0}
    return pl.pallas_call(
        _dispatch_kernel,
        out_shape=jax.ShapeDtypeStruct((n_slots * V7X_SUBLANES, d), F32),
        grid=(t // td,),
        in_specs=in_specs,
        out_specs=pl.BlockSpec(memory_space=pl.ANY),
        scratch_shapes=[pltpu.SemaphoreType.DMA(())],
        input_output_aliases=aliases,
        compiler_params=_params(("arbitrary",), 2 * _nbytes((td * V7X_SUBLANES, d), F32)),
        name="dispatch",
    )(*args)


EXPERT_NB = 256


def _expert_kernel(be_ref, nv_ref, nxt_ref, xs_ref, w1_hbm, b1_ref, w2_hbm, b2_ref, ys_ref,
                   w1_stage, w2_stage, w1_scr, w2_scr, act_scr, sems, *, layer):
    i = pl.program_id(0)
    d, dff = w1_stage.shape[0], w2_stage.shape[0]
    e = be_ref[i]
    changed = (i == 0) | (e != be_ref[jnp.maximum(i - 1, 0)])

    def weight_copies(ex):
        return (pltpu.make_async_copy(w1_hbm.at[layer, ex], w1_stage, sems.at[0]),
                pltpu.make_async_copy(w2_hbm.at[layer, ex], w2_stage, sems.at[1]))

    @pl.when(i == 0)
    def _():
        for cp in weight_copies(e):
            cp.start()

    @pl.when(changed)
    def _():
        for cp in weight_copies(e):
            cp.wait()
        rows = 128

        def cast1(r, carry):
            sl = pl.ds(pl.multiple_of(r * rows, rows), rows)
            w1_scr[sl, :] = w1_stage[sl, :].astype(BF16)
            return carry

        def cast2(r, carry):
            sl = pl.ds(pl.multiple_of(r * rows, rows), rows)
            w2_scr[sl, :] = w2_stage[sl, :].astype(BF16)
            return carry

        lax.fori_loop(0, d // rows, cast1, 0)
        lax.fori_loop(0, dff // rows, cast2, 0)
        nxt = nxt_ref[i]

        @pl.when(nxt >= 0)
        def _():
            for cp in weight_copies(nxt):
                cp.start()

    nv = nv_ref[i]
    nb = ys_ref.shape[0] // V7X_SUBLANES

    @pl.when(nv == 0)
    def _():
        ys_ref[...] = jnp.zeros(ys_ref.shape, F32)

    @pl.when(nv > 0)
    def _():
        x = jnp.concatenate(_load_tile_rows(xs_ref, nb), axis=1)
        ridx = lax.broadcasted_iota(I32, x.shape, 0)
        x = jnp.where(ridx < nv, x, 0.0).astype(BF16)
        for c in range(dff // EXPERT_NB):
            cg = slice(c * EXPERT_NB, (c + 1) * EXPERT_NB)
            cl = slice(dff + c * EXPERT_NB, dff + (c + 1) * EXPERT_NB)
            gate = jnp.dot(x, w1_scr[:, cg], preferred_element_type=F32) + b1_ref[:, cg]
            lin = jnp.dot(x, w1_scr[:, cl], preferred_element_type=F32) + b1_ref[:, cl]
            gate = jnp.minimum(gate, SWIGLU_LIMIT)
            lin = jnp.clip(lin, -SWIGLU_LIMIT, SWIGLU_LIMIT)
            act = gate * jax.nn.sigmoid(SWIGLU_ALPHA * gate) * (lin + 1.0)
            act_scr[:, cg] = act.astype(BF16)
        _store_tile_rows(ys_ref, jnp.dot(act_scr[...], w2_scr[...], preferred_element_type=F32) + b2_ref[...])


def _experts(xs, block_e, nvalid, next_e, w1, b1, w2, b2, *, layer):
    n_slots = xs.shape[0] // V7X_SUBLANES
    depth, ne, d, two_dff = w1.shape
    dff = two_dff // 2
    n_blocks = n_slots // MOE_BLOCK
    blk = pl.BlockSpec((MOE_BLOCK * V7X_SUBLANES, V7X_LANES), lambda i, be, nv, nx: (i, 0))
    grid_spec = pltpu.PrefetchScalarGridSpec(
        num_scalar_prefetch=3,
        grid=(n_blocks,),
        in_specs=[blk,
                  pl.BlockSpec(memory_space=pl.ANY),
                  pl.BlockSpec((None, None, 1, two_dff), lambda i, be, nv, nx: (layer, be[i], 0, 0)),
                  pl.BlockSpec(memory_space=pl.ANY),
                  pl.BlockSpec((None, None, 1, d), lambda i, be, nv, nx: (layer, be[i], 0, 0))],
        out_specs=blk,
        scratch_shapes=[pltpu.VMEM((d, two_dff), F32), pltpu.VMEM((dff, d), F32),
                        pltpu.VMEM((d, two_dff), BF16), pltpu.VMEM((dff, d), BF16),
                        pltpu.VMEM((MOE_BLOCK, dff), BF16), pltpu.SemaphoreType.DMA((2,))],
    )
    vmem = (_nbytes((d, two_dff), F32) + _nbytes((dff, d), F32) + 4 * _nbytes((MOE_BLOCK, d), F32)
            + _nbytes((d, two_dff), BF16) + _nbytes((dff, d), BF16) + 6 * _nbytes((MOE_BLOCK, dff), F32))
    return pl.pallas_call(
        functools.partial(_expert_kernel, layer=layer),
        out_shape=jax.ShapeDtypeStruct(xs.shape, F32),
        grid_spec=grid_spec,
        compiler_params=_params(("arbitrary",), vmem),
        name="experts",
    )(block_e, nvalid, next_e, xs, w1, b1.reshape(depth, ne, 1, two_dff), w2, b2.reshape(depth, ne, 1, d))


def _combine_kernel(dest_ref, dnext_ref, lat_ref, w_ref, mod_ref, fnw_ref, ys_ref, o_ref, buf, sems, *,
                    final_norm):
    i = pl.program_id(0)
    n = lat_ref.shape[0]
    slot = i % 2

    def issue(d_ref, s):
        def start(t, carry):
            for k in range(TOP_K):
                pltpu.make_async_copy(ys_ref.at[_tile_row(d_ref[0, TOP_K * t + k]), :],
                                      buf.at[s, k, _tile_row(t), :], sems.at[s]).start(priority=k % DMA_PRIORITIES)
            return carry

        lax.fori_loop(0, n, start, 0, unroll=DMA_UNROLL)

    @pl.when(i == 0)
    def _():
        issue(dest_ref, slot)

    @pl.when(i + 1 < pl.num_programs(0))
    def _():
        issue(dnext_ref, 1 - slot)

    for k in range(TOP_K):
        pltpu.make_async_copy(ys_ref.at[pl.ds(0, n * V7X_SUBLANES), :], buf.at[slot, k], sems.at[slot]).wait()
    parts = [jnp.zeros((n, V7X_LANES), F32)] * V7X_SUBLANES
    for k in range(TOP_K):
        wk = w_ref[:, k:k + 1]
        rows = _load_tile_rows(buf.at[slot, k], n)
        parts = [p + wk * r for p, r in zip(parts, rows)]
    out = lat_ref[...] + mod_ref[5:6, :] * jnp.concatenate(parts, axis=1)
    if final_norm:
        out = _rms(out, fnw_ref[...])
    o_ref[...] = out


def _combine(lat, dest_tiles, wts, modv, fnw, ys, *, seg0, tiles_per_seg, final_norm):
    t, d = lat.shape
    td = T_DMA
    nt = t // td
    vmem = 2 * (2 * _nbytes((td, d), F32) + _nbytes((td, V7X_LANES), F32)) + 2 * _nbytes((TOP_K, td, d), F32) \
        + 4 * _nbytes((td, d), F32)
    return pl.pallas_call(
        functools.partial(_combine_kernel, final_norm=final_norm),
        out_shape=jax.ShapeDtypeStruct((t, d), F32),
        grid=(nt,),
        in_specs=[pl.BlockSpec((None, 1, TOP_K * td), lambda i: (i, 0, 0), memory_space=pltpu.SMEM),
                  pl.BlockSpec((None, 1, TOP_K * td), lambda i: (jnp.minimum(i + 1, nt - 1), 0, 0),
                               memory_space=pltpu.SMEM),
                  pl.BlockSpec((td, d), lambda i: (i, 0)),
                  pl.BlockSpec((td, V7X_SUBLANES), lambda i: (i, 0)),
                  pl.BlockSpec((None, V7X_SUBLANES, d), lambda i: (seg0 + i // tiles_per_seg, 0, 0)),
                  pl.BlockSpec((1, d), lambda i: (0, 0)),
                  pl.BlockSpec(memory_space=pl.ANY)],
        out_specs=pl.BlockSpec((td, d), lambda i: (i, 0)),
        scratch_shapes=[pltpu.VMEM((2, TOP_K, td * V7X_SUBLANES, V7X_LANES), F32),
                        pltpu.SemaphoreType.DMA((2,))],
        compiler_params=_params(("arbitrary",), vmem),
        name="combine",
    )(dest_tiles, dest_tiles, lat, wts, modv, fnw.astype(F32).reshape(1, d), ys)


def _routing_tables(counts, n_blocks):
    padded = (counts + MOE_BLOCK - 1) // MOE_BLOCK * MOE_BLOCK
    pends = jnp.cumsum(padded)
    pstarts = pends - padded
    bstart = jnp.arange(n_blocks, dtype=I32) * MOE_BLOCK
    block_e = jnp.minimum(jnp.sum((pends[None, :] <= bstart[:, None]).astype(I32), axis=1), N_EXPERTS - 1)
    nvalid = jnp.clip(counts[block_e] - (bstart - pstarts[block_e]), 0, MOE_BLOCK)
    nvalid = jnp.where(bstart < pends[-1], nvalid, 0)
    blk = jnp.arange(n_blocks, dtype=I32)
    is_change = jnp.concatenate([jnp.zeros((1,), bool), block_e[1:] != block_e[:-1]])
    change_at = jnp.where(is_change, blk, n_blocks)
    next_change = lax.cummin(change_at, axis=0, reverse=True)
    after = jnp.concatenate([next_change[1:], jnp.full((1,), n_blocks, I32)])
    next_e = jnp.where(after < n_blocks, block_e[jnp.minimum(after, n_blocks - 1)], -1)
    return pstarts, block_e.astype(I32), nvalid.astype(I32), next_e.astype(I32)


def _dest_tiles(idx, rank, pstarts):
    e = idx[:TOP_K]
    start = jnp.sum(jnp.where(e[:, :, None] == jnp.arange(N_EXPERTS)[None, None, :], pstarts[None, None, :], 0),
                    axis=2)
    dest = (start + rank[:TOP_K]).astype(I32).T
    return dest.reshape(-1, 1, TOP_K * T_DMA)


def kernel(x, c, ctx, c_ctx, w_ada, b_ada, norm1_w, norm2_w, w_in, pool_w, pool_scale, conv_w, conv_b,
           dt_bias, a_log, d_skip, ssm_norm_w, w_branch, w_out, router_w, router_b, moe_w1, moe_b1,
           moe_w2, moe_b2, final_norm_w):
    batch, n, d = x.shape
    n_ctx = ctx.shape[1]
    depth = w_ada.shape[0]
    d_inner = SSM_HEADS * SSM_HEAD_DIM
    conv_dim = conv_w.shape[2]
    off_four, off_z, off_xbc = d, 2 * d, 3 * d
    off_dt = off_xbc + conv_dim
    off_gate = off_dt + 2 * SSM_HEADS
    assert batch + 1 <= V7X_SUBLANES and d_inner == d and conv_dim == 2 * d
    assert d == V7X_SUBLANES * V7X_LANES

    lat = x.reshape(batch * n, d).astype(F32)
    cx = ctx.reshape(batch * n_ctx, d).astype(F32)
    cvec = jnp.concatenate([c.astype(F32), c_ctx.astype(F32)[None, :],
                            jnp.zeros((V7X_SUBLANES - batch - 1, d), F32)], axis=0)
    mods = _ada(cvec, w_ada.astype(F32), b_ada.astype(F32))
    seg_ctx = batch
    zero_state = jnp.zeros((batch, 2, d_inner, SSM_STATE), F32)
    zero_cnt = jnp.zeros((V7X_LANES, V7X_LANES), F32)

    for i in range(depth):
        last = i == depth - 1
        modv = mods[i].reshape(V7X_SUBLANES, N_ADA, d)[:batch + 1]
        modv = jnp.pad(modv, ((0, 0), (0, V7X_SUBLANES - N_ADA), (0, 0)))
        wi = w_in[i]
        w_main = jnp.concatenate([wi[:, off_xbc:off_dt], wi[:, :off_z], wi[:, off_z:off_xbc], wi[:, off_gate:]],
                                 axis=1).astype(BF16)
        w_dt = jnp.pad(wi[:, off_dt:off_gate], ((0, 0), (0, V7X_LANES - 2 * SSM_HEADS))).astype(BF16)

        u_l, dtr_l = _inproj(lat, modv, norm1_w[i], w_main, w_dt, tm=TM_LAT, seg0=0, tiles_per_seg=n // TM_LAT)
        u_c, dtr_c = _inproj(cx, modv, norm1_w[i], w_main, w_dt, tm=batch * n_ctx, seg0=seg_ctx, tiles_per_seg=1)

        conv_args = (conv_w[i], conv_b[i], dt_bias[i], a_log[i])
        cv_c = _conv(u_c, dtr_c, *conv_args, tc=n_ctx, seq_len=n_ctx, d_inner=d_inner)
        cv_l = _conv(u_l, dtr_l, *conv_args, tc=TM_ROW, seq_len=n, d_inner=d_inner)
        yf_c, yb_c, st = _ssd(cv_c, d_skip[i], zero_state, batch=batch, seq_len=n_ctx)
        yf_l, yb_l, _ = _ssd(cv_l, d_skip[i], st, batch=batch, seq_len=n)

        ya_l = _pool2d(u_l, pool_w[i], pool_scale[i], seq_len=n)
        fb_l = _fourier_lat(u_l, batch=batch, seq_len=n, d=d)

        wb = w_branch[i].astype(BF16)
        wo = w_out[i].astype(BF16)
        rw_t = jnp.pad(router_w[i].astype(F32).T, ((0, V7X_LANES - N_EXPERTS), (0, 0)))
        rw_hi = rw_t.astype(BF16)
        rw = jnp.stack([rw_hi, (rw_t - rw_hi.astype(F32)).astype(BF16)])
        rb = jnp.pad(router_b[i].astype(F32), (0, V7X_LANES - N_EXPERTS), constant_values=-1e30)
        rb = jnp.broadcast_to(rb[:, None], (V7X_LANES, V7X_LANES))
        merge_w = (ssm_norm_w[i], wb, wo, norm2_w[i], rw, rb)
        lat_m, h2_l, idx_l, rank_l, wts_l, cnt = _merge(
            ya_l, fb_l, yf_l, yb_l, u_l, lat, modv, *merge_w, zero_cnt,
            tm=TM_ROW, seg0=0, tiles_per_seg=n // TM_ROW)
        n_tok = batch * n
        if not last:
            ya_c = _pool1d(u_c, pool_w[i], pool_scale[i], seq_len=n_ctx)
            fb_c = _fourier_ctx(u_c, seq_len=n_ctx, d=d)
            cx_m, h2_c, idx_c, rank_c, wts_c, cnt = _merge(
                ya_c, fb_c, yf_c, yb_c, u_c, cx, modv, *merge_w, cnt,
                tm=batch * n_ctx, seg0=seg_ctx, tiles_per_seg=1)
            n_tok += batch * n_ctx

        n_blocks = -(-(n_tok * TOP_K) // MOE_BLOCK) + N_EXPERTS
        n_slots = n_blocks * MOE_BLOCK
        counts = cnt[:N_EXPERTS, 0].astype(I32)
        pstarts, block_e, nvalid, next_e = _routing_tables(counts, n_blocks)
        dest_l = _dest_tiles(idx_l, rank_l, pstarts)
        xs = _dispatch(h2_l, dest_l, n_slots)
        if not last:
            dest_c = _dest_tiles(idx_c, rank_c, pstarts)
            xs = _dispatch(h2_c, dest_c, n_slots, xs_prev=xs)
        ys = _experts(xs, block_e, nvalid, next_e, moe_w1, moe_b1, moe_w2, moe_b2, layer=i)
        lat = _combine(lat_m, dest_l, wts_l.T, modv, final_norm_w, ys, seg0=0, tiles_per_seg=n // T_DMA,
                       final_norm=last)
        if not last:
            cx = _combine(cx_m, dest_c, wts_c.T, modv, final_norm_w, ys, seg0=seg_ctx,
                          tiles_per_seg=batch * n_ctx // T_DMA, final_norm=False)
    return lat.reshape(batch, n, d).astype(x.dtype)
```

```python
import functools
import math

import numpy as np
import jax
import jax.numpy as jnp
from jax import lax
from jax.experimental import pallas as pl
from jax.experimental.pallas import tpu as pltpu

F32 = jnp.float32
BF16 = jnp.bfloat16
I32 = jnp.int32
HIGHEST = lax.Precision.HIGHEST

GRID_W = 64
N_ADA = 6
N_BRANCH = 3
MIX_GROUPS = 4
POOL_WINDOWS = (2, 4, 8, 16)
SSM_HEADS = 16
SSM_HEAD_DIM = 64
SSM_GROUPS = 4
SSM_STATE = 128
SSM_CONV = 5
SSM_CHUNK = 128
N_EXPERTS = 32
TOP_K = 4
SWIGLU_ALPHA = 1.702
SWIGLU_LIMIT = 7.0
RMS_EPS = 1e-6
MOE_BLOCK = 512

V7X_VMEM_BYTES = 64 * 2**20
V7X_LANES = 128
V7X_SUBLANES = 8
V7X_BF16_SUBLANES = 16
VMEM_COMPILER_ALLOWANCE = 6 * 2**20

TM_LAT = 1024
INPROJ_COLS = 2048
TM_ROW = 512
T_DMA = 512
DMA_UNROLL = 4
DMA_PRIORITIES = 2
FOURIER_N1 = 64
F1_BLOCK = 16
F2_BLOCK = 8


def _params(semantics, vmem_bytes):
    limit = min(int(vmem_bytes) + VMEM_COMPILER_ALLOWANCE, V7X_VMEM_BYTES - 2 * 2**20)
    return pltpu.CompilerParams(dimension_semantics=semantics, vmem_limit_bytes=limit)


def _nbytes(shape, dtype):
    return int(np.prod(shape)) * jnp.dtype(dtype).itemsize


def _silu(v):
    return v * jax.nn.sigmoid(v)


def _rms(v, w):
    return v * lax.rsqrt(jnp.mean(v * v, axis=-1, keepdims=True) + RMS_EPS) * w


def _store_tile_rows(ref, v):
    n = v.shape[0]
    for j in range(V7X_SUBLANES):
        ref[pl.ds(j, n, stride=V7X_SUBLANES), :] = v[:, j * V7X_LANES:(j + 1) * V7X_LANES]


def _load_tile_rows(ref, n):
    return [ref[pl.ds(j, n, stride=V7X_SUBLANES), :] for j in range(V7X_SUBLANES)]


def _ada_kernel(c_ref, w_ref, b_ref, o_ref):
    o_ref[...] = jnp.dot(_silu(c_ref[...]), w_ref[...], preferred_element_type=F32,
                         precision=HIGHEST) + b_ref[...]


def _ada(cvec, w_ada, b_ada):
    depth, d, nd = w_ada.shape
    return pl.pallas_call(
        _ada_kernel,
        out_shape=jax.ShapeDtypeStruct((depth, V7X_SUBLANES, nd), F32),
        grid=(depth, nd // d),
        in_specs=[pl.BlockSpec((V7X_SUBLANES, d), lambda l, j: (0, 0)),
                  pl.BlockSpec((None, d, d), lambda l, j: (l, 0, j)),
                  pl.BlockSpec((None, 1, d), lambda l, j: (l, 0, j))],
        out_specs=pl.BlockSpec((None, V7X_SUBLANES, d), lambda l, j: (l, 0, j)),
        compiler_params=_params(("parallel", "parallel"), 2 * _nbytes((d, d), F32)),
        name="ada",
    )(cvec, w_ada, b_ada.reshape(depth, 1, nd))


U_XBC, U_POOL, U_FOUR, U_Z, U_GATE = 0, 2, 3, 4, 5
U_BLOCKS = 8


def _inproj_kernel(x_ref, mod_ref, nw_ref, w_ref, wdt_ref, u_ref, dt_ref, h_scr):
    @pl.when(pl.program_id(1) == 0)
    def _():
        h = _rms(x_ref[...], nw_ref[...]) * (1.0 + mod_ref[1:2, :]) + mod_ref[0:1, :]
        hb = h.astype(BF16)
        h_scr[...] = hb
        dt_ref[...] = jnp.dot(hb, wdt_ref[...], preferred_element_type=F32)

    u_ref[...] = jnp.dot(h_scr[...], w_ref[...], preferred_element_type=F32).astype(BF16)


def _inproj(x, modv, nw, w_main, w_dt, *, tm, seg0, tiles_per_seg):
    t, d = x.shape
    tn = INPROJ_COLS
    vmem = 2 * (_nbytes((tm, d), F32) + _nbytes((tm, tn), BF16) + _nbytes((d, tn), BF16)
                + _nbytes((tm, V7X_LANES), F32) + _nbytes((d, V7X_LANES), BF16)) + _nbytes((tm, d), BF16) \
        + _nbytes((tm, tn), F32)
    return pl.pallas_call(
        _inproj_kernel,
        out_shape=(jax.ShapeDtypeStruct((t, U_BLOCKS * d), BF16),
                   jax.ShapeDtypeStruct((t, V7X_LANES), F32)),
        grid=(t // tm, U_BLOCKS * d // tn),
        in_specs=[pl.BlockSpec((tm, d), lambda i, j: (i, 0)),
                  pl.BlockSpec((None, V7X_SUBLANES, d), lambda i, j: (seg0 + i // tiles_per_seg, 0, 0)),
                  pl.BlockSpec((1, d), lambda i, j: (0, 0)),
                  pl.BlockSpec((d, tn), lambda i, j: (0, j)),
                  pl.BlockSpec((d, V7X_LANES), lambda i, j: (0, 0))],
        out_specs=(pl.BlockSpec((tm, tn), lambda i, j: (i, j)),
                   pl.BlockSpec((tm, V7X_LANES), lambda i, j: (i, 0))),
        scratch_shapes=[pltpu.VMEM((tm, d), BF16)],
        compiler_params=_params(("parallel", "arbitrary"), vmem),
        name="inproj",
    )(x, modv, nw.reshape(1, d), w_main, w_dt)


CONV_HALO = V7X_BF16_SUBLANES
CONV_RB = 128
CONV_CB = 128


def _conv_kernel(cur_ref, prev_ref, next_ref, dtraw_ref, cw_ref, cb_ref, dtb_ref, alog_ref,
                 xt_ref, bc_ref, ct_ref, aux_ref, acum_ref, xp_scr, *, tc, tiles_per_seq, d_inner):
    p = pl.program_id(0) % tiles_per_seq
    xp_scr[0:CONV_HALO, :] = jnp.where(p > 0, prev_ref[...].astype(F32), 0.0)
    xp_scr[CONV_HALO:CONV_HALO + tc, :] = cur_ref[...].astype(F32)
    xp_scr[CONV_HALO + tc:, :] = jnp.where(p < tiles_per_seq - 1, next_ref[...].astype(F32), 0.0)
    n_bc = SSM_GROUPS * SSM_STATE
    for rb in range(tc // CONV_RB):
        rows = slice(rb * CONV_RB, (rb + 1) * CONV_RB)
        for cb in range(cur_ref.shape[1] // CONV_CB):
            c0 = cb * CONV_CB
            cols = slice(c0, c0 + CONV_CB)
            acc = jnp.broadcast_to(cb_ref[:, cols], (CONV_RB, CONV_CB))
            a0 = CONV_HALO - V7X_SUBLANES + rb * CONV_RB
            win = xp_scr[a0:a0 + CONV_RB + 2 * V7X_SUBLANES, cols]
            for k in range(SSM_CONV):
                s = SSM_CONV // 2 - k
                tap = win if s == 0 else pltpu.roll(win, s % win.shape[0], axis=0)
                acc = acc + cw_ref[k:k + 1, cols] * tap[V7X_SUBLANES:V7X_SUBLANES + CONV_RB, :]
            v = _silu(acc)
            if c0 < d_inner:
                xt_ref[cols, rows] = v.T.astype(BF16)
            elif c0 < d_inner + n_bc:
                bc_ref[rows, c0 - d_inner:c0 - d_inner + CONV_CB] = v.astype(BF16)
            else:
                bc_ref[rows, c0 - d_inner:c0 - d_inner + CONV_CB] = v.astype(BF16)
                cc = c0 - d_inner - n_bc
                ct_ref[cc:cc + CONV_CB, rows] = v.T.astype(BF16)

    dtr = dtraw_ref[...] + dtb_ref[...]
    dt = jnp.maximum(dtr, 0.0) + jnp.log1p(jnp.exp(-jnp.abs(dtr)))
    dta = dt * (-jnp.exp(alog_ref[...]))
    dt_t = dt.T
    dta_t = dta.T
    ki = lax.broadcasted_iota(I32, (tc, tc), 0)
    li = lax.broadcasted_iota(I32, (tc, tc), 1)
    same = (ki // SSM_CHUNK) == (li // SSM_CHUNK)
    one, zero = jnp.float32(1.0), jnp.float32(0.0)
    cumf = jnp.dot(dta_t, jnp.where(same & (ki <= li), one, zero), preferred_element_type=F32,
                   precision=HIGHEST)
    cumb = jnp.dot(dta_t, jnp.where(same & (ki >= li), one, zero), preferred_element_type=F32,
                   precision=HIGHEST)
    tot = jnp.dot(dta_t, jnp.where(same, one, zero), preferred_element_type=F32, precision=HIGHEST)
    nh = SSM_HEADS
    aux_ref[0:2 * nh, :] = dt_t[0:2 * nh, :]
    aux_ref[2 * nh:3 * nh, :] = cumf[0:nh, :]
    aux_ref[3 * nh:4 * nh, :] = cumb[nh:2 * nh, :]
    aux_ref[4 * nh:6 * nh, :] = tot[0:2 * nh, :]
    aux_ref[6 * nh:, :] = jnp.zeros((V7X_LANES - 6 * nh, tc), F32)
    hrow = lax.broadcasted_iota(I32, (V7X_LANES, tc), 0)
    acum_ref[...] = jnp.where(hrow < nh, cumf, cumb).T


def _conv(u, dt_raw, conv_w, conv_b, dt_bias, a_log, *, tc, seq_len, d_inner):
    t = u.shape[0]
    conv_dim = conv_w.shape[1]
    n_bc = SSM_GROUPS * SSM_STATE
    nt = t // tc
    hb = tc // CONV_HALO
    last_halo = t // CONV_HALO - 1
    cw = jnp.pad(conv_w.astype(F32), ((0, V7X_SUBLANES - SSM_CONV), (0, 0)))
    pad32 = lambda v: jnp.pad(v.astype(F32).reshape(1, -1), ((0, 0), (0, V7X_LANES - 2 * SSM_HEADS)))
    vmem = (2 * (_nbytes((tc, conv_dim), BF16) * 2 + _nbytes((tc, V7X_LANES), F32) * 3
                 + _nbytes((d_inner + 2 * n_bc, tc), BF16))
            + _nbytes((tc + 2 * CONV_HALO, conv_dim), F32) + 4 * _nbytes((tc, tc), F32))
    kern = functools.partial(_conv_kernel, tc=tc, tiles_per_seq=seq_len // tc, d_inner=d_inner)
    return pl.pallas_call(
        kern,
        out_shape=(jax.ShapeDtypeStruct((d_inner, t), BF16),
                   jax.ShapeDtypeStruct((t, 2 * n_bc), BF16),
                   jax.ShapeDtypeStruct((n_bc, t), BF16),
                   jax.ShapeDtypeStruct((V7X_LANES, t), F32),
                   jax.ShapeDtypeStruct((t, V7X_LANES), F32)),
        grid=(nt,),
        in_specs=[pl.BlockSpec((tc, conv_dim), lambda i: (i, U_XBC)),
                  pl.BlockSpec((CONV_HALO, conv_dim), lambda i: (jnp.maximum(i * hb - 1, 0), U_XBC)),
                  pl.BlockSpec((CONV_HALO, conv_dim), lambda i: (jnp.minimum((i + 1) * hb, last_halo), U_XBC)),
                  pl.BlockSpec((tc, V7X_LANES), lambda i: (i, 0)),
                  pl.BlockSpec((V7X_SUBLANES, conv_dim), lambda i: (0, 0)),
                  pl.BlockSpec((1, conv_dim), lambda i: (0, 0)),
                  pl.BlockSpec((1, V7X_LANES), lambda i: (0, 0)),
                  pl.BlockSpec((1, V7X_LANES), lambda i: (0, 0))],
        out_specs=(pl.BlockSpec((d_inner, tc), lambda i: (0, i)),
                   pl.BlockSpec((tc, 2 * n_bc), lambda i: (i, 0)),
                   pl.BlockSpec((n_bc, tc), lambda i: (0, i)),
                   pl.BlockSpec((V7X_LANES, tc), lambda i: (0, i)),
                   pl.BlockSpec((tc, V7X_LANES), lambda i: (i, 0))),
        scratch_shapes=[pltpu.VMEM((tc + 2 * CONV_HALO, conv_dim), F32)],
        compiler_params=_params(("parallel",), vmem),
        name="conv",
    )(u, u, u, dt_raw, cw, conv_b.astype(F32).reshape(1, -1), pad32(dt_bias), pad32(a_log))


def _ssd_direction(d, rev, xt_ref, bc_ref, ct_ref, aux_ref, acum_ref, st_scr, y_ref, dsk_ref):
    q = SSM_CHUNK
    hd = SSM_HEAD_DIM
    nh = SSM_HEADS
    si = lax.broadcasted_iota(I32, (q, q), 0)
    li = lax.broadcasted_iota(I32, (q, q), 1)
    mask = (li <= si) if rev else (li >= si)
    hpg = nh // SSM_GROUPS
    for g in range(SSM_GROUPS):
        b_g = bc_ref[:, g * SSM_STATE:(g + 1) * SSM_STATE]
        c_g = bc_ref[:, (SSM_GROUPS + g) * SSM_STATE:(SSM_GROUPS + g + 1) * SSM_STATE]
        ct_g = ct_ref[g * SSM_STATE:(g + 1) * SSM_STATE, :].astype(F32)
        cbt = lax.dot_general(b_g, c_g, (((1,), (1,)), ((), ())), preferred_element_type=F32)
        for hh in range(hpg):
            h = g * hpg + hh
            r = d * nh + h
            dt_row = aux_ref[r:r + 1, :]
            cum_row = aux_ref[2 * nh + r:2 * nh + r + 1, :]
            tot_row = aux_ref[4 * nh + r:4 * nh + r + 1, :]
            cum_col = acum_ref[:, r:r + 1]
            decay = jnp.exp(jnp.where(mask, cum_row - cum_col, -jnp.inf))
            w_top = (cbt * decay).astype(BF16)
            w_bot = (ct_g * jnp.exp(cum_row)).astype(BF16)
            w = jnp.concatenate([w_top, w_bot], axis=0)
            rows = slice(h * hd, (h + 1) * hd)
            x_t = xt_ref[rows, :].astype(F32)
            xd = x_t * dt_row
            s_old = st_scr[d, rows, :]
            lhs = jnp.concatenate([xd.astype(BF16), s_old.astype(BF16)], axis=1)
            y_t = jnp.dot(lhs, w, preferred_element_type=F32)
            if not rev:
                y_t = y_t + dsk_ref[h] * x_t
            y_ref[rows, :] = y_t.astype(y_ref.dtype)
            xdw = (xd * jnp.exp(tot_row - cum_row)).astype(BF16)
            st_scr[d, rows, :] = s_old * jnp.exp(tot_row) + jnp.dot(xdw, b_g, preferred_element_type=F32)


def _ssd_kernel(dsk_ref, xt_f, bc_f, ct_f, aux_f, ac_f, xt_b, bc_b, ct_b, aux_b, ac_b, init_ref,
                yf_ref, yb_ref, fin_ref, st_scr):
    c = pl.program_id(1)

    @pl.when(c == 0)
    def _():
        st_scr[...] = init_ref[...]

    _ssd_direction(0, False, xt_f, bc_f, ct_f, aux_f, ac_f, st_scr, yf_ref, dsk_ref)
    _ssd_direction(1, True, xt_b, bc_b, ct_b, aux_b, ac_b, st_scr, yb_ref, dsk_ref)

    @pl.when(c == pl.num_programs(1) - 1)
    def _():
        fin_ref[...] = st_scr[...]


def _ssd(conv_out, d_skip, init_state, *, batch, seq_len):
    xt, bc, ct, aux, acum = conv_out
    d_inner, t = xt.shape
    n_bc = ct.shape[0]
    q = SSM_CHUNK
    nc = seq_len // q
    fwd = lambda b, c: b * nc + c
    bwd = lambda b, c: b * nc + nc - 1 - c

    def specs(pos):
        return [pl.BlockSpec((d_inner, q), lambda b, c: (0, pos(b, c))),
                pl.BlockSpec((q, 2 * n_bc), lambda b, c: (pos(b, c), 0)),
                pl.BlockSpec((n_bc, q), lambda b, c: (0, pos(b, c))),
                pl.BlockSpec((V7X_LANES, q), lambda b, c: (0, pos(b, c))),
                pl.BlockSpec((q, V7X_LANES), lambda b, c: (pos(b, c), 0))]

    st_spec = pl.BlockSpec((None, 2, d_inner, SSM_STATE), lambda b, c: (b, 0, 0, 0))
    st_bytes = _nbytes((2, d_inner, SSM_STATE), F32)
    vmem = 5 * st_bytes + 4 * (_nbytes((d_inner, q), BF16) * 2 + _nbytes((q, 2 * n_bc), BF16)
                               + _nbytes((n_bc, q), BF16) + 2 * _nbytes((q, V7X_LANES), F32))
    return pl.pallas_call(
        _ssd_kernel,
        out_shape=(jax.ShapeDtypeStruct((d_inner, t), BF16),
                   jax.ShapeDtypeStruct((d_inner, t), BF16),
                   jax.ShapeDtypeStruct((batch, 2, d_inner, SSM_STATE), F32)),
        grid=(batch, nc),
        in_specs=[pl.BlockSpec(memory_space=pltpu.SMEM)] + specs(fwd) + specs(bwd) + [st_spec],
        out_specs=(pl.BlockSpec((d_inner, q), lambda b, c: (0, fwd(b, c))),
                   pl.BlockSpec((d_inner, q), lambda b, c: (0, bwd(b, c))),
                   st_spec),
        scratch_shapes=[pltpu.VMEM((2, d_inner, SSM_STATE), F32)],
        compiler_params=_params(("arbitrary", "arbitrary"), vmem),
        name="ssd",
    )(d_skip.astype(F32), xt, bc, ct, aux, acum, xt, bc, ct, aux, acum, init_state)


def _pool_body(slabs, cur_ref, mask_refs, icnt_ref, pw_ref, ps_ref, o_ref):
    gw = cur_ref.shape[1] // MIX_GROUPS
    for g in range(MIX_GROUPS):
        cols = slice(g * gw, (g + 1) * gw)
        xin = slabs(g, cols)
        s = jnp.dot(mask_refs[g][...], xin, preferred_element_type=F32)
        inv = icnt_ref[:, g * V7X_LANES:(g + 1) * V7X_LANES]
        p = s * jnp.concatenate([inv] * (gw // V7X_LANES), axis=1)
        dlt = p - cur_ref[:, cols].astype(F32)
        y = jnp.dot(dlt.astype(BF16), pw_ref[g], preferred_element_type=F32) * ps_ref[:, cols]
        o_ref[:, cols] = y.astype(BF16)


def _pool2d_kernel(prev_ref, cur_ref, next_ref, m0, m1, m2, m3, icnt_ref, pw_ref, ps_ref, o_ref):
    tile = cur_ref.shape[0]

    def slabs(g, cols):
        w = POOL_WINDOWS[g]
        up, dn = w // 2, w - w // 2 - 1
        parts = [prev_ref[tile - GRID_W * up:, cols], cur_ref[:, cols]]
        if dn:
            parts.append(next_ref[:GRID_W * dn, cols])
        return jnp.concatenate(parts, axis=0)

    _pool_body(slabs, cur_ref, (m0, m1, m2, m3), icnt_ref, pw_ref, ps_ref, o_ref)


def _pool1d_kernel(cur_ref, m0, m1, m2, m3, icnt_ref, pw_ref, ps_ref, o_ref):
    _pool_body(lambda g, cols: cur_ref[:, cols], cur_ref, (m0, m1, m2, m3), icnt_ref, pw_ref, ps_ref, o_ref)


def _window_ok(out_pos, in_pos, w):
    lo = out_pos[:, None] - w // 2
    return (in_pos[None, :] >= lo) & (in_pos[None, :] < lo + w)


def _pool_tables_2d(tile):
    rows = tile // GRID_W
    t = jnp.arange(tile)
    masks, invs = [], []
    for w in POOL_WINDOWS:
        up, dn = w // 2, w - w // 2 - 1
        u = jnp.arange(tile + GRID_W * (up + dn))
        r_in, c_in = u // GRID_W - up, u % GRID_W
        base = _window_ok(t // GRID_W, r_in, w) & _window_ok(t % GRID_W, c_in, w)
        var = jnp.stack([base & (r_in >= 0)[None, :], base, base & (r_in < rows)[None, :]])
        masks.append(var.astype(BF16))
        invs.append(1.0 / jnp.sum(var.astype(F32), axis=2))
    inv = jnp.stack(invs, axis=1)
    inv = jnp.broadcast_to(inv[:, :, :, None], inv.shape + (V7X_LANES,))
    inv = jnp.transpose(inv, (0, 2, 1, 3)).reshape(3, tile, MIX_GROUPS * V7X_LANES)
    return masks, inv


def _pool_tables_1d(n):
    t = jnp.arange(n)
    masks, invs = [], []
    for w in POOL_WINDOWS:
        m = _window_ok(t, t, w)
        masks.append(m.astype(BF16))
        invs.append(1.0 / jnp.sum(m.astype(F32), axis=1))
    inv = jnp.stack(invs, axis=0)
    inv = jnp.broadcast_to(inv[:, :, None], inv.shape + (V7X_LANES,))
    return masks, jnp.transpose(inv, (1, 0, 2)).reshape(n, MIX_GROUPS * V7X_LANES)


def _pool2d(u, pool_w, pool_scale, *, seq_len):
    t = u.shape[0]
    d = pool_scale.shape[0]
    tile = TM_ROW
    nt, per_img = t // tile, seq_len // tile
    masks, inv = _pool_tables_2d(tile)

    def variant(i):
        p = i % per_img
        return jnp.where(p == 0, 0, jnp.where(p == per_img - 1, 2, 1))

    gw = d // MIX_GROUPS
    vmem = 2 * (4 * _nbytes((tile, d), BF16) + sum(_nbytes(m.shape[1:], BF16) for m in masks)
                + _nbytes(inv.shape[1:], F32) + _nbytes((MIX_GROUPS, gw, gw), BF16))
    return pl.pallas_call(
        _pool2d_kernel,
        out_shape=jax.ShapeDtypeStruct((t, d), BF16),
        grid=(nt,),
        in_specs=[pl.BlockSpec((tile, d), lambda i: (jnp.maximum(i - 1, 0), U_POOL)),
                  pl.BlockSpec((tile, d), lambda i: (i, U_POOL)),
                  pl.BlockSpec((tile, d), lambda i: (jnp.minimum(i + 1, nt - 1), U_POOL))]
                 + [pl.BlockSpec((None,) + m.shape[1:], lambda i: (variant(i), 0, 0)) for m in masks]
                 + [pl.BlockSpec((None,) + inv.shape[1:], lambda i: (variant(i), 0, 0)),
                    pl.BlockSpec((MIX_GROUPS, gw, gw), lambda i: (0, 0, 0)),
                    pl.BlockSpec((1, d), lambda i: (0, 0))],
        out_specs=pl.BlockSpec((tile, d), lambda i: (i, 0)),
        compiler_params=_params(("parallel",), vmem),
        name="pool2d",
    )(u, u, u, *masks, inv, pool_w.astype(BF16), pool_scale.astype(F32).reshape(1, d))


def _pool1d(u, pool_w, pool_scale, *, seq_len):
    t = u.shape[0]
    d = pool_scale.shape[0]
    masks, inv = _pool_tables_1d(seq_len)
    gw = d // MIX_GROUPS
    vmem = 2 * (2 * _nbytes((seq_len, d), BF16) + 4 * _nbytes((seq_len, seq_len), BF16)
                + _nbytes(inv.shape, F32) + _nbytes((MIX_GROUPS, gw, gw), BF16))
    return pl.pallas_call(
        _pool1d_kernel,
        out_shape=jax.ShapeDtypeStruct((t, d), BF16),
        grid=(t // seq_len,),
        in_specs=[pl.BlockSpec((seq_len, d), lambda i: (i, U_POOL))]
                 + [pl.BlockSpec(m.shape, lambda i: (0, 0)) for m in masks]
                 + [pl.BlockSpec(inv.shape, lambda i: (0, 0)),
                    pl.BlockSpec((MIX_GROUPS, gw, gw), lambda i: (0, 0, 0)),
                    pl.BlockSpec((1, d), lambda i: (0, 0))],
        out_specs=pl.BlockSpec((seq_len, d), lambda i: (i, 0)),
        compiler_params=_params(("parallel",), vmem),
        name="pool1d",
    )(u, *masks, inv, pool_w.astype(BF16), pool_scale.astype(F32).reshape(1, d))


def _dft_tables(n_pos, gw):
    n1, n2 = FOURIER_N1, n_pos // FOURIER_N1
    c = np.arange(gw)
    ang = 2.0 * np.pi * ((c[:, None] * c[None, :]) % gw) / gw
    cs = np.concatenate([np.cos(ang), -np.sin(ang)], axis=1)
    a = np.arange(n1)
    ang1 = 2.0 * np.pi * ((a[:, None] * a[None, :]) % n1) / n1
    wc, ws = np.cos(ang1), np.sin(ang1)
    w1 = np.block([[wc, ws], [-ws, wc]])
    k = a[:, None, None] + n1 * np.arange(n2)[None, :, None]
    ang2 = 2.0 * np.pi * ((k * np.arange(n2)[None, None, :]) % n_pos) / n_pos
    m2 = np.concatenate([np.cos(ang2), np.sin(ang2)], axis=2)
    to_bf = lambda v: jnp.asarray(v, dtype=F32).astype(BF16)
    return to_bf(cs), to_bf(w1), to_bf(m2)


def _f1_kernel(u_ref, cs_ref, w1_ref, o_ref, z_scr):
    nb, n1, d = u_ref.shape
    gw = d // MIX_GROUPS
    u = u_ref[...].reshape(nb * n1, d)
    for g in range(MIX_GROUPS):
        z = jnp.dot(u[:, g * gw:(g + 1) * gw], cs_ref[...], preferred_element_type=F32)
        z_scr[:, g * gw:(g + 1) * gw] = z[:, :gw].astype(BF16)
        z_scr[:, d + g * gw:d + (g + 1) * gw] = z[:, gw:].astype(BF16)
    for j in range(nb):
        zz = jnp.concatenate([z_scr[j * n1:(j + 1) * n1, :d], z_scr[j * n1:(j + 1) * n1, d:]], axis=0)
        o_ref[j] = jnp.dot(w1_ref[...], zz, preferred_element_type=F32).astype(BF16)


def _f2_kernel(a_ref, m2_ref, o_ref, *, scale):
    for j in range(a_ref.shape[0]):
        o_ref[j] = (jnp.dot(m2_ref[j], a_ref[j], preferred_element_type=F32) * scale).astype(BF16)


def _fourier_lat(u, *, batch, seq_len, d):
    n1, n2 = FOURIER_N1, seq_len // FOURIER_N1
    gw = d // MIX_GROUPS
    cs, w1, m2 = _dft_tables(seq_len, gw)
    ut = u[:, U_FOUR * d:(U_FOUR + 1) * d].reshape(batch, n1, n2, d).transpose(0, 2, 1, 3)
    nb = F1_BLOCK
    vmem1 = 2 * (_nbytes((nb, n1, d), BF16) + _nbytes((nb, 2 * n1, d), BF16)) + _nbytes((nb * n1, 2 * d), BF16) \
        + 2 * _nbytes((nb * n1, 2 * gw), F32)
    a = pl.pallas_call(
        _f1_kernel,
        out_shape=jax.ShapeDtypeStruct((batch, n2, 2 * n1, d), BF16),
        grid=(batch, n2 // nb),
        in_specs=[pl.BlockSpec((None, nb, n1, d), lambda b, j: (b, j, 0, 0)),
                  pl.BlockSpec(cs.shape, lambda b, j: (0, 0)),
                  pl.BlockSpec(w1.shape, lambda b, j: (0, 0))],
        out_specs=pl.BlockSpec((None, nb, 2 * n1, d), lambda b, j: (b, j, 0, 0)),
        scratch_shapes=[pltpu.VMEM((nb * n1, 2 * d), BF16)],
        compiler_params=_params(("parallel", "parallel"), vmem1),
        name="fourier_stage1",
    )(ut, cs, w1)
    at = a.reshape(batch, n2, 2, n1, d).transpose(0, 3, 2, 1, 4).reshape(batch, n1, 2 * n2, d)
    kb = F2_BLOCK
    vmem2 = 2 * (_nbytes((kb, 2 * n2, d), BF16) + _nbytes((kb, n2, 2 * n2), BF16) + _nbytes((kb, n2, d), BF16))
    x2 = pl.pallas_call(
        functools.partial(_f2_kernel, scale=1.0 / math.sqrt(seq_len * gw)),
        out_shape=jax.ShapeDtypeStruct((batch, n1, n2, d), BF16),
        grid=(batch, n1 // kb),
        in_specs=[pl.BlockSpec((None, kb, 2 * n2, d), lambda b, j: (b, j, 0, 0)),
                  pl.BlockSpec((kb, n2, 2 * n2), lambda b, j: (j, 0, 0))],
        out_specs=pl.BlockSpec((None, kb, n2, d), lambda b, j: (b, j, 0, 0)),
        compiler_params=_params(("parallel", "parallel"), vmem2),
        name="fourier_stage2",
    )(at, m2)
    return x2.transpose(0, 2, 1, 3).reshape(batch * seq_len, d)


def _fourier_ctx_kernel(u_ref, cs_ref, cn_ref, o_ref, *, scale):
    d = u_ref.shape[1]
    gw = d // MIX_GROUPS
    zr, zi = [], []
    for g in range(MIX_GROUPS):
        z = jnp.dot(u_ref[:, g * gw:(g + 1) * gw], cs_ref[...], preferred_element_type=F32)
        zr.append(z[:, :gw].astype(BF16))
        zi.append(z[:, gw:].astype(BF16))
    zz = jnp.concatenate([jnp.concatenate(zr, axis=1), jnp.concatenate(zi, axis=1)], axis=0)
    o_ref[...] = (jnp.dot(cn_ref[...], zz, preferred_element_type=F32) * scale).astype(BF16)


def _fourier_ctx(u, *, seq_len, d):
    t = u.shape[0]
    gw = d // MIX_GROUPS
    cs, _, _ = _dft_tables(FOURIER_N1 * 2, gw)
    n = np.arange(seq_len)
    ang = 2.0 * np.pi * ((n[:, None] * n[None, :]) % seq_len) / seq_len
    cn = jnp.asarray(np.concatenate([np.cos(ang), np.sin(ang)], axis=1), dtype=F32).astype(BF16)
    vmem = 2 * (2 * _nbytes((seq_len, d), BF16) + _nbytes(cs.shape, BF16) + _nbytes(cn.shape, BF16)) \
        + 4 * _nbytes((seq_len, d), F32)
    return pl.pallas_call(
        functools.partial(_fourier_ctx_kernel, scale=1.0 / math.sqrt(seq_len * gw)),
        out_shape=jax.ShapeDtypeStruct((t, d), BF16),
        grid=(t // seq_len,),
        in_specs=[pl.BlockSpec((seq_len, d), lambda i: (i, U_FOUR)),
                  pl.BlockSpec(cs.shape, lambda i: (0, 0)),
                  pl.BlockSpec(cn.shape, lambda i: (0, 0))],
        out_specs=pl.BlockSpec((seq_len, d), lambda i: (i, 0)),
        compiler_params=_params(("parallel",), vmem),
        name="fourier_ctx",
    )(u, cs, cn)


def _merge_kernel(ya_ref, yb_ref, yf_ref, ybw_ref, z_ref, g0_ref, g1_ref, g2_ref, lat_ref, mod_ref,
                  snw_ref, wb_ref, wo_ref, n2w_ref, rw_ref, rb_ref, tri_ref, ones_ref, cnt0_ref,
                  lat_o, h2_o, idx_o, rank_o, wts_o, cnt_o, run_scr):
    tm = lat_ref.shape[0]

    @pl.when(pl.program_id(0) == 0)
    def _():
        run_scr[...] = cnt0_ref[...]

    y_ssm = (yf_ref[...].astype(F32) + ybw_ref[...].astype(F32)).T
    yc = _rms(y_ssm * _silu(z_ref[...].astype(F32)), snw_ref[...])
    merged = jnp.zeros(lat_ref.shape, F32)
    for k, (y, g_ref) in enumerate(((ya_ref[...], g0_ref), (yb_ref[...], g1_ref), (yc.astype(BF16), g2_ref))):
        proj = jnp.dot(y, wb_ref[k], preferred_element_type=F32)
        merged = merged + jax.nn.sigmoid(g_ref[...].astype(F32)) * proj
    ol = jnp.dot(merged.astype(BF16), wo_ref[...], preferred_element_type=F32)
    lat = lat_ref[...] + mod_ref[2:3, :] * ol
    lat_o[...] = lat
    h2 = _rms(lat, n2w_ref[...]) * (1.0 + mod_ref[4:5, :]) + mod_ref[3:4, :]
    _store_tile_rows(h2_o, h2)

    h_hi = h2.astype(BF16)
    h_lo = (h2 - h_hi.astype(F32)).astype(BF16)
    nt_dims = (((1,), (1,)), ((), ()))
    lt = (lax.dot_general(rw_ref[0], h_hi, nt_dims, preferred_element_type=F32)
          + lax.dot_general(rw_ref[0], h_lo, nt_dims, preferred_element_type=F32)
          + lax.dot_general(rw_ref[1], h_hi, nt_dims, preferred_element_type=F32))
    lt = lt + jnp.concatenate([rb_ref[...]] * (tm // V7X_LANES), axis=1)
    ei = lax.broadcasted_iota(I32, lt.shape, 0).astype(F32)
    vals, idxs = [], []
    for _ in range(TOP_K):
        m = jnp.max(lt, axis=0, keepdims=True)
        sel = jnp.min(jnp.where(lt == m, ei, float(V7X_LANES)), axis=0, keepdims=True)
        vals.append(m)
        idxs.append(sel)
        lt = jnp.where(ei == sel, -jnp.inf, lt)
    exps = [jnp.exp(v - vals[0]) for v in vals]
    den = exps[0] + exps[1] + exps[2] + exps[3]
    onehots = [ei == s for s in idxs]
    chosen = jnp.zeros(lt.shape, F32)
    for oh in onehots:
        chosen = chosen + jnp.where(oh, 1.0, 0.0)
    chosen_b = chosen.astype(BF16)
    before = jnp.dot(chosen_b, tri_ref[...], preferred_element_type=F32) \
        + jnp.concatenate([run_scr[...]] * (tm // V7X_LANES), axis=1)
    for k in range(TOP_K):
        idx_o[k:k + 1, :] = idxs[k].astype(I32)
        rank_o[k:k + 1, :] = jnp.sum(jnp.where(onehots[k], before, 0.0), axis=0, keepdims=True).astype(I32)
        wts_o[k:k + 1, :] = exps[k] / den
    pad = V7X_SUBLANES - TOP_K
    idx_o[TOP_K:, :] = jnp.zeros((pad, tm), I32)
    rank_o[TOP_K:, :] = jnp.zeros((pad, tm), I32)
    wts_o[TOP_K:, :] = jnp.zeros((pad, tm), F32)
    run_scr[...] = run_scr[...] + jnp.dot(chosen_b, ones_ref[...], preferred_element_type=F32)
    cnt_o[...] = run_scr[...]


def _merge(ya, yb, yf_t, yb_t, u, lat, modv, ssm_norm_w, wb, wo, norm2_w, rw, rb, cnt0, *,
           tm, seg0, tiles_per_seg):
    t, d = lat.shape
    ne = V7X_LANES
    tri = (jnp.arange(tm)[:, None] < jnp.arange(tm)[None, :]).astype(BF16)
    ones = jnp.ones((tm, ne), BF16)
    row = lambda c: pl.BlockSpec((tm, d), lambda i: (i, c))
    col = pl.BlockSpec((d, tm), lambda i: (0, i))
    const = lambda shape: pl.BlockSpec(shape, lambda i: (0,) * len(shape), pipeline_mode=pl.Buffered(1))
    tok = pl.BlockSpec((V7X_SUBLANES, tm), lambda i: (0, i))
    vmem = (2 * (4 * _nbytes((tm, d), BF16) + 2 * _nbytes((d, tm), BF16) + 2 * _nbytes((tm, d), BF16)
                 + 3 * _nbytes((tm, d), F32))
            + 4 * _nbytes((d, d), BF16) + _nbytes((2, ne, d), BF16) + _nbytes((tm, tm), BF16)
            + 8 * _nbytes((tm, d), F32))
    return pl.pallas_call(
        _merge_kernel,
        out_shape=(jax.ShapeDtypeStruct((t, d), F32), jax.ShapeDtypeStruct((t * V7X_SUBLANES, V7X_LANES), F32),
                   jax.ShapeDtypeStruct((V7X_SUBLANES, t), I32), jax.ShapeDtypeStruct((V7X_SUBLANES, t), I32),
                   jax.ShapeDtypeStruct((V7X_SUBLANES, t), F32), jax.ShapeDtypeStruct((ne, ne), F32)),
        grid=(t // tm,),
        in_specs=[row(0), row(0), col, col, row(U_Z), row(U_GATE), row(U_GATE + 1), row(U_GATE + 2),
                  row(0),
                  pl.BlockSpec((None, V7X_SUBLANES, d), lambda i: (seg0 + i // tiles_per_seg, 0, 0)),
                  const((1, d)), const((N_BRANCH, d, d)), const((d, d)), const((1, d)),
                  const((2, ne, d)), const((ne, ne)), const((tm, tm)), const((tm, ne)), const((ne, ne))],
        out_specs=(row(0), pl.BlockSpec((tm * V7X_SUBLANES, V7X_LANES), lambda i: (i, 0)), tok, tok, tok,
                   pl.BlockSpec((ne, ne), lambda i: (0, 0))),
        scratch_shapes=[pltpu.VMEM((ne, ne), F32)],
        compiler_params=_params(("arbitrary",), vmem),
        name="merge",
    )(ya, yb, yf_t, yb_t, u, u, u, u, lat, modv, ssm_norm_w.astype(F32).reshape(1, d), wb, wo,
      norm2_w.astype(F32).reshape(1, d), rw, rb, tri, ones, cnt0)


def _tile_row(r):
    return pl.ds(pl.multiple_of(r * V7X_SUBLANES, V7X_SUBLANES), V7X_SUBLANES)


def _dispatch_kernel(dest_ref, h_ref, *rest):
    xs_ref, sem = rest[-2], rest[-1]
    rows = h_ref.shape[0]
    n = rows // V7X_SUBLANES

    def start(t, carry):
        for k in range(TOP_K):
            pltpu.make_async_copy(h_ref.at[_tile_row(t), :], xs_ref.at[_tile_row(dest_ref[0, TOP_K * t + k]), :],
                                  sem).start(priority=k % DMA_PRIORITIES)
        return carry

    lax.fori_loop(0, n, start, 0, unroll=DMA_UNROLL)
    for k in range(TOP_K):
        pltpu.make_async_copy(h_ref, xs_ref.at[pl.ds(0, rows), :], sem).wait()


def _dispatch(h2, dest_tiles, n_slots, xs_prev=None):
    t, d = h2.shape[0] // V7X_SUBLANES, V7X_LANES
    td = T_DMA
    in_specs = [pl.BlockSpec((None, 1, TOP_K * td), lambda i: (i, 0, 0), memory_space=pltpu.SMEM),
                pl.BlockSpec((td * V7X_SUBLANES, d), lambda i: (i, 0))]
    args = [dest_tiles, h2]
    aliases = {}
    if xs_prev is not None:
        in_specs.append(pl.BlockSpec(memory_space=pl.ANY))
        args.append(xs_prev)
        aliases = {2: 0}
    return pl.pallas_call(
        _dispatch_kernel,
        out_shape=jax.ShapeDtypeStruct((n_slots * V7X_SUBLANES, d), F32),
        grid=(t // td,),
        in_specs=in_specs,
        out_specs=pl.BlockSpec(memory_space=pl.ANY),
        scratch_shapes=[pltpu.SemaphoreType.DMA(())],
        input_output_aliases=aliases,
        compiler_params=_params(("arbitrary",), 2 * _nbytes((td * V7X_SUBLANES, d), F32)),
        name="dispatch",
    )(*args)


EXPERT_NB = 256
EXPERT_SUB = 256


def _expert_kernel(be_ref, nv_ref, nxt_ref, xs_ref, w1_hbm, b1_ref, w2_hbm, b2_ref, ys_ref,
                   w1_stage, w2_stage, w1_scr, w2_scr, act_scr, sems, *, layer):
    i = pl.program_id(0)
    d, dff = w1_stage.shape[0], w2_stage.shape[0]
    e = be_ref[i]
    changed = (i == 0) | (e != be_ref[jnp.maximum(i - 1, 0)])

    def weight_copies(ex):
        return (pltpu.make_async_copy(w1_hbm.at[layer, ex], w1_stage, sems.at[0]),
                pltpu.make_async_copy(w2_hbm.at[layer, ex], w2_stage, sems.at[1]))

    @pl.when(i == 0)
    def _():
        for cp in weight_copies(e):
            cp.start()

    @pl.when(changed)
    def _():
        for cp in weight_copies(e):
            cp.wait()
        rows = 128

        def cast1(r, carry):
            sl = pl.ds(pl.multiple_of(r * rows, rows), rows)
            w1_scr[sl, :] = w1_stage[sl, :].astype(BF16)
            return carry

        def cast2(r, carry):
            sl = pl.ds(pl.multiple_of(r * rows, rows), rows)
            w2_scr[sl, :] = w2_stage[sl, :].astype(BF16)
            return carry

        lax.fori_loop(0, d // rows, cast1, 0)
        lax.fori_loop(0, dff // rows, cast2, 0)
        nxt = nxt_ref[i]

        @pl.when(nxt >= 0)
        def _():
            for cp in weight_copies(nxt):
                cp.start()

    nv = nv_ref[i]
    sub = act_scr.shape[0]
    for h in range(MOE_BLOCK // sub):
        nvh = nv - h * sub
        xs_h = xs_ref.at[pl.ds(h * sub * V7X_SUBLANES, sub * V7X_SUBLANES), :]
        ys_h = ys_ref.at[pl.ds(h * sub * V7X_SUBLANES, sub * V7X_SUBLANES), :]

        @pl.when(nvh <= 0)
        def _():
            ys_h[...] = jnp.zeros(ys_h.shape, F32)

        @pl.when(nvh > 0)
        def _():
            x = jnp.concatenate(_load_tile_rows(xs_h, sub), axis=1)
            ridx = lax.broadcasted_iota(I32, x.shape, 0)
            x = jnp.where(ridx < nvh, x, 0.0).astype(BF16)
            for c in range(dff // EXPERT_NB):
                cg = slice(c * EXPERT_NB, (c + 1) * EXPERT_NB)
                cl = slice(dff + c * EXPERT_NB, dff + (c + 1) * EXPERT_NB)
                gate = jnp.dot(x, w1_scr[:, cg], preferred_element_type=F32) + b1_ref[:, cg]
                lin = jnp.dot(x, w1_scr[:, cl], preferred_element_type=F32) + b1_ref[:, cl]
                gate = jnp.minimum(gate, SWIGLU_LIMIT)
                lin = jnp.clip(lin, -SWIGLU_LIMIT, SWIGLU_LIMIT)
                act = gate * jax.nn.sigmoid(SWIGLU_ALPHA * gate) * (lin + 1.0)
                act_scr[:, cg] = act.astype(BF16)
            _store_tile_rows(ys_h, jnp.dot(act_scr[...], w2_scr[...], preferred_element_type=F32) + b2_ref[...])


def _experts(xs, block_e, nvalid, next_e, w1, b1, w2, b2, *, layer):
    n_slots = xs.shape[0] // V7X_SUBLANES
    depth, ne, d, two_dff = w1.shape
    dff = two_dff // 2
    n_blocks = n_slots // MOE_BLOCK
    blk = pl.BlockSpec((MOE_BLOCK * V7X_SUBLANES, V7X_LANES), lambda i, be, nv, nx: (i, 0))
    grid_spec = pltpu.PrefetchScalarGridSpec(
        num_scalar_prefetch=3,
        grid=(n_blocks,),
        in_specs=[blk,
                  pl.BlockSpec(memory_space=pl.ANY),
                  pl.BlockSpec((None, None, 1, two_dff), lambda i, be, nv, nx: (layer, be[i], 0, 0)),
                  pl.BlockSpec(memory_space=pl.ANY),
                  pl.BlockSpec((None, None, 1, d), lambda i, be, nv, nx: (layer, be[i], 0, 0))],
        out_specs=blk,
        scratch_shapes=[pltpu.VMEM((d, two_dff), F32), pltpu.VMEM((dff, d), F32),
                        pltpu.VMEM((d, two_dff), BF16), pltpu.VMEM((dff, d), BF16),
                        pltpu.VMEM((EXPERT_SUB, dff), BF16), pltpu.SemaphoreType.DMA((2,))],
    )
    vmem = (_nbytes((d, two_dff), F32) + _nbytes((dff, d), F32) + 4 * _nbytes((MOE_BLOCK, d), F32)
            + _nbytes((d, two_dff), BF16) + _nbytes((dff, d), BF16) + 6 * _nbytes((MOE_BLOCK, dff), F32))
    return pl.pallas_call(
        functools.partial(_expert_kernel, layer=layer),
        out_shape=jax.ShapeDtypeStruct(xs.shape, F32),
        grid_spec=grid_spec,
        compiler_params=_params(("arbitrary",), vmem),
        name="experts",
    )(block_e, nvalid, next_e, xs, w1, b1.reshape(depth, ne, 1, two_dff), w2, b2.reshape(depth, ne, 1, d))


def _combine_kernel(dest_ref, dnext_ref, lat_ref, w_ref, mod_ref, fnw_ref, ys_ref, o_ref, buf, sems, *,
                    final_norm):
    i = pl.program_id(0)
    n = lat_ref.shape[0]
    slot = i % 2

    def issue(d_ref, s):
        def start(t, carry):
            for k in range(TOP_K):
                pltpu.make_async_copy(ys_ref.at[_tile_row(d_ref[0, TOP_K * t + k]), :],
                                      buf.at[s, k, _tile_row(t), :], sems.at[s]).start(priority=k % DMA_PRIORITIES)
            return carry

        lax.fori_loop(0, n, start, 0, unroll=DMA_UNROLL)

    @pl.when(i == 0)
    def _():
        issue(dest_ref, slot)

    @pl.when(i + 1 < pl.num_programs(0))
    def _():
        issue(dnext_ref, 1 - slot)

    for k in range(TOP_K):
        pltpu.make_async_copy(ys_ref.at[pl.ds(0, n * V7X_SUBLANES), :], buf.at[slot, k], sems.at[slot]).wait()
    parts = [jnp.zeros((n, V7X_LANES), F32)] * V7X_SUBLANES
    for k in range(TOP_K):
        wk = w_ref[:, k:k + 1]
        rows = _load_tile_rows(buf.at[slot, k], n)
        parts = [p + wk * r for p, r in zip(parts, rows)]
    out = lat_ref[...] + mod_ref[5:6, :] * jnp.concatenate(parts, axis=1)
    if final_norm:
        out = _rms(out, fnw_ref[...])
    o_ref[...] = out


def _combine(lat, dest_tiles, wts, modv, fnw, ys, *, seg0, tiles_per_seg, final_norm):
    t, d = lat.shape
    td = T_DMA
    nt = t // td
    vmem = 2 * (2 * _nbytes((td, d), F32) + _nbytes((td, V7X_LANES), F32)) + 2 * _nbytes((TOP_K, td, d), F32) \
        + 4 * _nbytes((td, d), F32)
    return pl.pallas_call(
        functools.partial(_combine_kernel, final_norm=final_norm),
        out_shape=jax.ShapeDtypeStruct((t, d), F32),
        grid=(nt,),
        in_specs=[pl.BlockSpec((None, 1, TOP_K * td), lambda i: (i, 0, 0), memory_space=pltpu.SMEM),
                  pl.BlockSpec((None, 1, TOP_K * td), lambda i: (jnp.minimum(i + 1, nt - 1), 0, 0),
                               memory_space=pltpu.SMEM),
                  pl.BlockSpec((td, d), lambda i: (i, 0)),
                  pl.BlockSpec((td, V7X_SUBLANES), lambda i: (i, 0)),
                  pl.BlockSpec((None, V7X_SUBLANES, d), lambda i: (seg0 + i // tiles_per_seg, 0, 0)),
                  pl.BlockSpec((1, d), lambda i: (0, 0)),
                  pl.BlockSpec(memory_space=pl.ANY)],
        out_specs=pl.BlockSpec((td, d), lambda i: (i, 0)),
        scratch_shapes=[pltpu.VMEM((2, TOP_K, td * V7X_SUBLANES, V7X_LANES), F32),
                        pltpu.SemaphoreType.DMA((2,))],
        compiler_params=_params(("arbitrary",), vmem),
        name="combine",
    )(dest_tiles, dest_tiles, lat, wts, modv, fnw.astype(F32).reshape(1, d), ys)


def _routing_tables(counts, n_blocks):
    padded = (counts + MOE_BLOCK - 1) // MOE_BLOCK * MOE_BLOCK
    pends = jnp.cumsum(padded)
    pstarts = pends - padded
    bstart = jnp.arange(n_blocks, dtype=I32) * MOE_BLOCK
    block_e = jnp.minimum(jnp.sum((pends[None, :] <= bstart[:, None]).astype(I32), axis=1), N_EXPERTS - 1)
    nvalid = jnp.clip(counts[block_e] - (bstart - pstarts[block_e]), 0, MOE_BLOCK)
    nvalid = jnp.where(bstart < pends[-1], nvalid, 0)
    blk = jnp.arange(n_blocks, dtype=I32)
    is_change = jnp.concatenate([jnp.zeros((1,), bool), block_e[1:] != block_e[:-1]])
    change_at = jnp.where(is_change, blk, n_blocks)
    next_change = lax.cummin(change_at, axis=0, reverse=True)
    after = jnp.concatenate([next_change[1:], jnp.full((1,), n_blocks, I32)])
    next_e = jnp.where(after < n_blocks, block_e[jnp.minimum(after, n_blocks - 1)], -1)
    return pstarts, block_e.astype(I32), nvalid.astype(I32), next_e.astype(I32)


def _dest_tiles(idx, rank, pstarts):
    e = idx[:TOP_K]
    start = jnp.sum(jnp.where(e[:, :, None] == jnp.arange(N_EXPERTS)[None, None, :], pstarts[None, None, :], 0),
                    axis=2)
    dest = (start + rank[:TOP_K]).astype(I32).T
    return dest.reshape(-1, 1, TOP_K * T_DMA)


def kernel(x, c, ctx, c_ctx, w_ada, b_ada, norm1_w, norm2_w, w_in, pool_w, pool_scale, conv_w, conv_b,
           dt_bias, a_log, d_skip, ssm_norm_w, w_branch, w_out, router_w, router_b, moe_w1, moe_b1,
           moe_w2, moe_b2, final_norm_w):
    batch, n, d = x.shape
    n_ctx = ctx.shape[1]
    depth = w_ada.shape[0]
    d_inner = SSM_HEADS * SSM_HEAD_DIM
    conv_dim = conv_w.shape[2]
    off_four, off_z, off_xbc = d, 2 * d, 3 * d
    off_dt = off_xbc + conv_dim
    off_gate = off_dt + 2 * SSM_HEADS
    assert batch + 1 <= V7X_SUBLANES and d_inner == d and conv_dim == 2 * d
    assert d == V7X_SUBLANES * V7X_LANES

    lat = x.reshape(batch * n, d).astype(F32)
    cx = ctx.reshape(batch * n_ctx, d).astype(F32)
    cvec = jnp.concatenate([c.astype(F32), c_ctx.astype(F32)[None, :],
                            jnp.zeros((V7X_SUBLANES - batch - 1, d), F32)], axis=0)
    mods = _ada(cvec, w_ada.astype(F32), b_ada.astype(F32))
    seg_ctx = batch
    zero_state = jnp.zeros((batch, 2, d_inner, SSM_STATE), F32)
    zero_cnt = jnp.zeros((V7X_LANES, V7X_LANES), F32)

    for i in range(depth):
        last = i == depth - 1
        modv = mods[i].reshape(V7X_SUBLANES, N_ADA, d)[:batch + 1]
        modv = jnp.pad(modv, ((0, 0), (0, V7X_SUBLANES - N_ADA), (0, 0)))
        wi = w_in[i]
        w_main = jnp.concatenate([wi[:, off_xbc:off_dt], wi[:, :off_z], wi[:, off_z:off_xbc], wi[:, off_gate:]],
                                 axis=1).astype(BF16)
        w_dt = jnp.pad(wi[:, off_dt:off_gate], ((0, 0), (0, V7X_LANES - 2 * SSM_HEADS))).astype(BF16)

        u_l, dtr_l = _inproj(lat, modv, norm1_w[i], w_main, w_dt, tm=TM_LAT, seg0=0, tiles_per_seg=n // TM_LAT)
        u_c, dtr_c = _inproj(cx, modv, norm1_w[i], w_main, w_dt, tm=batch * n_ctx, seg0=seg_ctx, tiles_per_seg=1)

        conv_args = (conv_w[i], conv_b[i], dt_bias[i], a_log[i])
        cv_c = _conv(u_c, dtr_c, *conv_args, tc=n_ctx, seq_len=n_ctx, d_inner=d_inner)
        cv_l = _conv(u_l, dtr_l, *conv_args, tc=TM_ROW, seq_len=n, d_inner=d_inner)
        yf_c, yb_c, st = _ssd(cv_c, d_skip[i], zero_state, batch=batch, seq_len=n_ctx)
        yf_l, yb_l, _ = _ssd(cv_l, d_skip[i], st, batch=batch, seq_len=n)

        ya_l = _pool2d(u_l, pool_w[i], pool_scale[i], seq_len=n)
        fb_l = _fourier_lat(u_l, batch=batch, seq_len=n, d=d)

        wb = w_branch[i].astype(BF16)
        wo = w_out[i].astype(BF16)
        rw_t = jnp.pad(router_w[i].astype(F32).T, ((0, V7X_LANES - N_EXPERTS), (0, 0)))
        rw_hi = rw_t.astype(BF16)
        rw = jnp.stack([rw_hi, (rw_t - rw_hi.astype(F32)).astype(BF16)])
        rb = jnp.pad(router_b[i].astype(F32), (0, V7X_LANES - N_EXPERTS), constant_values=-1e30)
        rb = jnp.broadcast_to(rb[:, None], (V7X_LANES, V7X_LANES))
        merge_w = (ssm_norm_w[i], wb, wo, norm2_w[i], rw, rb)
        lat_m, h2_l, idx_l, rank_l, wts_l, cnt = _merge(
            ya_l, fb_l, yf_l, yb_l, u_l, lat, modv, *merge_w, zero_cnt,
            tm=TM_ROW, seg0=0, tiles_per_seg=n // TM_ROW)
        n_tok = batch * n
        if not last:
            ya_c = _pool1d(u_c, pool_w[i], pool_scale[i], seq_len=n_ctx)
            fb_c = _fourier_ctx(u_c, seq_len=n_ctx, d=d)
            cx_m, h2_c, idx_c, rank_c, wts_c, cnt = _merge(
                ya_c, fb_c, yf_c, yb_c, u_c, cx, modv, *merge_w, cnt,
                tm=batch * n_ctx, seg0=seg_ctx, tiles_per_seg=1)
            n_tok += batch * n_ctx

        n_blocks = -(-(n_tok * TOP_K) // MOE_BLOCK) + N_EXPERTS
        n_slots = n_blocks * MOE_BLOCK
        counts = cnt[:N_EXPERTS, 0].astype(I32)
        pstarts, block_e, nvalid, next_e = _routing_tables(counts, n_blocks)
        dest_l = _dest_tiles(idx_l, rank_l, pstarts)
        xs = _dispatch(h2_l, dest_l, n_slots)
        if not last:
            dest_c = _dest_tiles(idx_c, rank_c, pstarts)
            xs = _dispatch(h2_c, dest_c, n_slots, xs_prev=xs)
        ys = _experts(xs, block_e, nvalid, next_e, moe_w1, moe_b1, moe_w2, moe_b2, layer=i)
        lat = _combine(lat_m, dest_l, wts_l.T, modv, final_norm_w, ys, seg0=0, tiles_per_seg=n // T_DMA,
                       final_norm=last)
        if not last:
            cx = _combine(cx_m, dest_c, wts_c.T, modv, final_norm_w, ys, seg0=seg_ctx,
                          tiles_per_seg=batch * n_ctx // T_DMA, final_norm=False)
    return lat.reshape(batch, n, d).astype(x.dtype)
```

```python
import functools
import math

import numpy as np
import jax
import jax.numpy as jnp
from jax import lax
from jax.experimental import pallas as pl
from jax.experimental.pallas import tpu as pltpu

F32 = jnp.float32
BF16 = jnp.bfloat16
I32 = jnp.int32
HIGHEST = lax.Precision.HIGHEST

GRID_W = 64
N_ADA = 6
N_BRANCH = 3
MIX_GROUPS = 4
POOL_WINDOWS = (2, 4, 8, 16)
SSM_HEADS = 16
SSM_HEAD_DIM = 64
SSM_GROUPS = 4
SSM_STATE = 128
SSM_CONV = 5
SSM_CHUNK = 128
N_EXPERTS = 32
TOP_K = 4
SWIGLU_ALPHA = 1.702
SWIGLU_LIMIT = 7.0
RMS_EPS = 1e-6
MOE_BLOCK = 512

V7X_VMEM_BYTES = 64 * 2**20
V7X_LANES = 128
V7X_SUBLANES = 8
V7X_BF16_SUBLANES = 16
VMEM_COMPILER_ALLOWANCE = 6 * 2**20

TM_LAT = 1024
INPROJ_COLS = 2048
TM_ROW = 512
T_DMA = 512
DMA_UNROLL = 4
DMA_PRIORITIES = 2
FOURIER_N1 = 64
F1_BLOCK = 16
F2_BLOCK = 8


def _params(semantics, vmem_bytes):
    limit = min(int(vmem_bytes) + VMEM_COMPILER_ALLOWANCE, V7X_VMEM_BYTES - 2 * 2**20)
    return pltpu.CompilerParams(dimension_semantics=semantics, vmem_limit_bytes=limit)


def _nbytes(shape, dtype):
    return int(np.prod(shape)) * jnp.dtype(dtype).itemsize


def _silu(v):
    return v * jax.nn.sigmoid(v)


def _rms(v, w):
    return v * lax.rsqrt(jnp.mean(v * v, axis=-1, keepdims=True) + RMS_EPS) * w


def _store_tile_rows(ref, v):
    n = v.shape[0]
    for j in range(V7X_SUBLANES):
        ref[pl.ds(j, n, stride=V7X_SUBLANES), :] = v[:, j * V7X_LANES:(j + 1) * V7X_LANES]


def _load_tile_rows(ref, n):
    return [ref[pl.ds(j, n, stride=V7X_SUBLANES), :] for j in range(V7X_SUBLANES)]


def _ada_kernel(c_ref, w_ref, b_ref, o_ref):
    o_ref[...] = jnp.dot(_silu(c_ref[...]), w_ref[...], preferred_element_type=F32,
                         precision=HIGHEST) + b_ref[...]


def _ada(cvec, w_ada, b_ada):
    depth, d, nd = w_ada.shape
    return pl.pallas_call(
        _ada_kernel,
        out_shape=jax.ShapeDtypeStruct((depth, V7X_SUBLANES, nd), F32),
        grid=(depth, nd // d),
        in_specs=[pl.BlockSpec((V7X_SUBLANES, d), lambda l, j: (0, 0)),
                  pl.BlockSpec((None, d, d), lambda l, j: (l, 0, j)),
                  pl.BlockSpec((None, 1, d), lambda l, j: (l, 0, j))],
        out_specs=pl.BlockSpec((None, V7X_SUBLANES, d), lambda l, j: (l, 0, j)),
        compiler_params=_params(("parallel", "parallel"), 2 * _nbytes((d, d), F32)),
        name="ada",
    )(cvec, w_ada, b_ada.reshape(depth, 1, nd))


U_XBC, U_POOL, U_FOUR, U_Z, U_GATE = 0, 2, 3, 4, 5
U_BLOCKS = 8


def _inproj_kernel(x_ref, mod_ref, nw_ref, w_ref, wdt_ref, u_ref, dt_ref, h_scr):
    @pl.when(pl.program_id(1) == 0)
    def _():
        h = _rms(x_ref[...], nw_ref[...]) * (1.0 + mod_ref[1:2, :]) + mod_ref[0:1, :]
        hb = h.astype(BF16)
        h_scr[...] = hb
        dt_ref[...] = jnp.dot(hb, wdt_ref[...], preferred_element_type=F32)

    u_ref[...] = jnp.dot(h_scr[...], w_ref[...], preferred_element_type=F32).astype(BF16)


def _inproj(x, modv, nw, w_main, w_dt, *, tm, seg0, tiles_per_seg):
    t, d = x.shape
    tn = INPROJ_COLS
    vmem = 2 * (_nbytes((tm, d), F32) + _nbytes((tm, tn), BF16) + _nbytes((d, tn), BF16)
                + _nbytes((tm, V7X_LANES), F32) + _nbytes((d, V7X_LANES), BF16)) + _nbytes((tm, d), BF16) \
        + _nbytes((tm, tn), F32)
    return pl.pallas_call(
        _inproj_kernel,
        out_shape=(jax.ShapeDtypeStruct((t, U_BLOCKS * d), BF16),
                   jax.ShapeDtypeStruct((t, V7X_LANES), F32)),
        grid=(t // tm, U_BLOCKS * d // tn),
        in_specs=[pl.BlockSpec((tm, d), lambda i, j: (i, 0)),
                  pl.BlockSpec((None, V7X_SUBLANES, d), lambda i, j: (seg0 + i // tiles_per_seg, 0, 0)),
                  pl.BlockSpec((1, d), lambda i, j: (0, 0)),
                  pl.BlockSpec((d, tn), lambda i, j: (0, j)),
                  pl.BlockSpec((d, V7X_LANES), lambda i, j: (0, 0))],
        out_specs=(pl.BlockSpec((tm, tn), lambda i, j: (i, j)),
                   pl.BlockSpec((tm, V7X_LANES), lambda i, j: (i, 0))),
        scratch_shapes=[pltpu.VMEM((tm, d), BF16)],
        compiler_params=_params(("parallel", "arbitrary"), vmem),
        name="inproj",
    )(x, modv, nw.reshape(1, d), w_main, w_dt)


CONV_HALO = V7X_BF16_SUBLANES
CONV_RB = 128
CONV_CB = 128


def _conv_kernel(cur_ref, prev_ref, next_ref, dtraw_ref, cw_ref, cb_ref, dtb_ref, alog_ref,
                 xt_ref, bc_ref, ct_ref, aux_ref, acum_ref, xp_scr, *, tc, tiles_per_seq, d_inner):
    p = pl.program_id(0) % tiles_per_seq
    xp_scr[0:CONV_HALO, :] = jnp.where(p > 0, prev_ref[...].astype(F32), 0.0)
    xp_scr[CONV_HALO:CONV_HALO + tc, :] = cur_ref[...].astype(F32)
    xp_scr[CONV_HALO + tc:, :] = jnp.where(p < tiles_per_seq - 1, next_ref[...].astype(F32), 0.0)
    n_bc = SSM_GROUPS * SSM_STATE
    for rb in range(tc // CONV_RB):
        rows = slice(rb * CONV_RB, (rb + 1) * CONV_RB)
        for cb in range(cur_ref.shape[1] // CONV_CB):
            c0 = cb * CONV_CB
            cols = slice(c0, c0 + CONV_CB)
            acc = jnp.broadcast_to(cb_ref[:, cols], (CONV_RB, CONV_CB))
            a0 = CONV_HALO - V7X_SUBLANES + rb * CONV_RB
            win = xp_scr[a0:a0 + CONV_RB + 2 * V7X_SUBLANES, cols]
            for k in range(SSM_CONV):
                s = SSM_CONV // 2 - k
                tap = win if s == 0 else pltpu.roll(win, s % win.shape[0], axis=0)
                acc = acc + cw_ref[k:k + 1, cols] * tap[V7X_SUBLANES:V7X_SUBLANES + CONV_RB, :]
            v = _silu(acc)
            if c0 < d_inner:
                xt_ref[cols, rows] = v.T.astype(BF16)
            elif c0 < d_inner + n_bc:
                bc_ref[rows, c0 - d_inner:c0 - d_inner + CONV_CB] = v.astype(BF16)
            else:
                bc_ref[rows, c0 - d_inner:c0 - d_inner + CONV_CB] = v.astype(BF16)
                cc = c0 - d_inner - n_bc
                ct_ref[cc:cc + CONV_CB, rows] = v.T.astype(BF16)

    dtr = dtraw_ref[...] + dtb_ref[...]
    dt = jnp.maximum(dtr, 0.0) + jnp.log1p(jnp.exp(-jnp.abs(dtr)))
    dta = dt * (-jnp.exp(alog_ref[...]))
    dt_t = dt.T
    dta_t = dta.T
    ki = lax.broadcasted_iota(I32, (tc, tc), 0)
    li = lax.broadcasted_iota(I32, (tc, tc), 1)
    same = (ki // SSM_CHUNK) == (li // SSM_CHUNK)
    one, zero = jnp.float32(1.0), jnp.float32(0.0)
    cumf = jnp.dot(dta_t, jnp.where(same & (ki <= li), one, zero), preferred_element_type=F32,
                   precision=HIGHEST)
    cumb = jnp.dot(dta_t, jnp.where(same & (ki >= li), one, zero), preferred_element_type=F32,
                   precision=HIGHEST)
    tot = jnp.dot(dta_t, jnp.where(same, one, zero), preferred_element_type=F32, precision=HIGHEST)
    nh = SSM_HEADS
    aux_ref[0:2 * nh, :] = dt_t[0:2 * nh, :]
    aux_ref[2 * nh:3 * nh, :] = cumf[0:nh, :]
    aux_ref[3 * nh:4 * nh, :] = cumb[nh:2 * nh, :]
    aux_ref[4 * nh:6 * nh, :] = tot[0:2 * nh, :]
    aux_ref[6 * nh:, :] = jnp.zeros((V7X_LANES - 6 * nh, tc), F32)
    hrow = lax.broadcasted_iota(I32, (V7X_LANES, tc), 0)
    acum_ref[...] = jnp.where(hrow < nh, cumf, cumb).T


def _conv(u, dt_raw, conv_w, conv_b, dt_bias, a_log, *, tc, seq_len, d_inner):
    t = u.shape[0]
    conv_dim = conv_w.shape[1]
    n_bc = SSM_GROUPS * SSM_STATE
    nt = t // tc
    hb = tc // CONV_HALO
    last_halo = t // CONV_HALO - 1
    cw = jnp.pad(conv_w.astype(F32), ((0, V7X_SUBLANES - SSM_CONV), (0, 0)))
    pad32 = lambda v: jnp.pad(v.astype(F32).reshape(1, -1), ((0, 0), (0, V7X_LANES - 2 * SSM_HEADS)))
    vmem = (2 * (_nbytes((tc, conv_dim), BF16) * 2 + _nbytes((tc, V7X_LANES), F32) * 3
                 + _nbytes((d_inner + 2 * n_bc, tc), BF16))
            + _nbytes((tc + 2 * CONV_HALO, conv_dim), F32) + 4 * _nbytes((tc, tc), F32))
    kern = functools.partial(_conv_kernel, tc=tc, tiles_per_seq=seq_len // tc, d_inner=d_inner)
    return pl.pallas_call(
        kern,
        out_shape=(jax.ShapeDtypeStruct((d_inner, t), BF16),
                   jax.ShapeDtypeStruct((t, 2 * n_bc), BF16),
                   jax.ShapeDtypeStruct((n_bc, t), BF16),
                   jax.ShapeDtypeStruct((V7X_LANES, t), F32),
                   jax.ShapeDtypeStruct((t, V7X_LANES), F32)),
        grid=(nt,),
        in_specs=[pl.BlockSpec((tc, conv_dim), lambda i: (i, U_XBC)),
                  pl.BlockSpec((CONV_HALO, conv_dim), lambda i: (jnp.maximum(i * hb - 1, 0), U_XBC)),
                  pl.BlockSpec((CONV_HALO, conv_dim), lambda i: (jnp.minimum((i + 1) * hb, last_halo), U_XBC)),
                  pl.BlockSpec((tc, V7X_LANES), lambda i: (i, 0)),
                  pl.BlockSpec((V7X_SUBLANES, conv_dim), lambda i: (0, 0)),
                  pl.BlockSpec((1, conv_dim), lambda i: (0, 0)),
                  pl.BlockSpec((1, V7X_LANES), lambda i: (0, 0)),
                  pl.BlockSpec((1, V7X_LANES), lambda i: (0, 0))],
        out_specs=(pl.BlockSpec((d_inner, tc), lambda i: (0, i)),
                   pl.BlockSpec((tc, 2 * n_bc), lambda i: (i, 0)),
                   pl.BlockSpec((n_bc, tc), lambda i: (0, i)),
                   pl.BlockSpec((V7X_LANES, tc), lambda i: (0, i)),
                   pl.BlockSpec((tc, V7X_LANES), lambda i: (i, 0))),
        scratch_shapes=[pltpu.VMEM((tc + 2 * CONV_HALO, conv_dim), F32)],
        compiler_params=_params(("parallel",), vmem),
        name="conv",
    )(u, u, u, dt_raw, cw, conv_b.astype(F32).reshape(1, -1), pad32(dt_bias), pad32(a_log))


def _ssd_direction(d, rev, xt_ref, bc_ref, ct_ref, aux_ref, acum_ref, st_scr, y_ref, dsk_ref):
    q = SSM_CHUNK
    hd = SSM_HEAD_DIM
    nh = SSM_HEADS
    si = lax.broadcasted_iota(I32, (q, q), 0)
    li = lax.broadcasted_iota(I32, (q, q), 1)
    mask = (li <= si) if rev else (li >= si)
    hpg = nh // SSM_GROUPS
    for g in range(SSM_GROUPS):
        b_g = bc_ref[:, g * SSM_STATE:(g + 1) * SSM_STATE]
        c_g = bc_ref[:, (SSM_GROUPS + g) * SSM_STATE:(SSM_GROUPS + g + 1) * SSM_STATE]
        ct_g = ct_ref[g * SSM_STATE:(g + 1) * SSM_STATE, :].astype(F32)
        cbt = lax.dot_general(b_g, c_g, (((1,), (1,)), ((), ())), preferred_element_type=F32)
        xdw_g, keep_g = [], []
        for hh in range(hpg):
            h = g * hpg + hh
            r = d * nh + h
            dt_row = aux_ref[r:r + 1, :]
            cum_row = aux_ref[2 * nh + r:2 * nh + r + 1, :]
            tot_row = aux_ref[4 * nh + r:4 * nh + r + 1, :]
            cum_col = acum_ref[:, r:r + 1]
            decay = jnp.exp(jnp.where(mask, cum_row - cum_col, -jnp.inf))
            w_top = (cbt * decay).astype(BF16)
            w_bot = (ct_g * jnp.exp(cum_row)).astype(BF16)
            w = jnp.concatenate([w_top, w_bot], axis=0)
            rows = slice(h * hd, (h + 1) * hd)
            x_t = xt_ref[rows, :].astype(F32)
            xd = x_t * dt_row
            s_old = st_scr[d, rows, :]
            lhs = jnp.concatenate([xd.astype(BF16), s_old.astype(BF16)], axis=1)
            y_t = jnp.dot(lhs, w, preferred_element_type=F32)
            if not rev:
                y_t = y_t + dsk_ref[h] * x_t
            y_ref[rows, :] = y_t.astype(y_ref.dtype)
            xdw_g.append((xd * jnp.exp(tot_row - cum_row)).astype(BF16))
            keep_g.append(s_old * jnp.exp(tot_row))
        grows = slice(g * hpg * hd, (g + 1) * hpg * hd)
        st_scr[d, grows, :] = jnp.concatenate(keep_g, axis=0) + jnp.dot(
            jnp.concatenate(xdw_g, axis=0), b_g, preferred_element_type=F32)


def _ssd_kernel(dsk_ref, xt_f, bc_f, ct_f, aux_f, ac_f, xt_b, bc_b, ct_b, aux_b, ac_b, init_ref,
                yf_ref, yb_ref, fin_ref, st_scr):
    c = pl.program_id(1)

    @pl.when(c == 0)
    def _():
        st_scr[...] = init_ref[...]

    _ssd_direction(0, False, xt_f, bc_f, ct_f, aux_f, ac_f, st_scr, yf_ref, dsk_ref)
    _ssd_direction(1, True, xt_b, bc_b, ct_b, aux_b, ac_b, st_scr, yb_ref, dsk_ref)

    @pl.when(c == pl.num_programs(1) - 1)
    def _():
        fin_ref[...] = st_scr[...]


def _ssd(conv_out, d_skip, init_state, *, batch, seq_len):
    xt, bc, ct, aux, acum = conv_out
    d_inner, t = xt.shape
    n_bc = ct.shape[0]
    q = SSM_CHUNK
    nc = seq_len // q
    fwd = lambda b, c: b * nc + c
    bwd = lambda b, c: b * nc + nc - 1 - c

    def specs(pos):
        return [pl.BlockSpec((d_inner, q), lambda b, c: (0, pos(b, c))),
                pl.BlockSpec((q, 2 * n_bc), lambda b, c: (pos(b, c), 0)),
                pl.BlockSpec((n_bc, q), lambda b, c: (0, pos(b, c))),
                pl.BlockSpec((V7X_LANES, q), lambda b, c: (0, pos(b, c))),
                pl.BlockSpec((q, V7X_LANES), lambda b, c: (pos(b, c), 0))]

    st_spec = pl.BlockSpec((None, 2, d_inner, SSM_STATE), lambda b, c: (b, 0, 0, 0))
    st_bytes = _nbytes((2, d_inner, SSM_STATE), F32)
    vmem = 5 * st_bytes + 4 * (_nbytes((d_inner, q), BF16) * 2 + _nbytes((q, 2 * n_bc), BF16)
                               + _nbytes((n_bc, q), BF16) + 2 * _nbytes((q, V7X_LANES), F32))
    return pl.pallas_call(
        _ssd_kernel,
        out_shape=(jax.ShapeDtypeStruct((d_inner, t), BF16),
                   jax.ShapeDtypeStruct((d_inner, t), BF16),
                   jax.ShapeDtypeStruct((batch, 2, d_inner, SSM_STATE), F32)),
        grid=(batch, nc),
        in_specs=[pl.BlockSpec(memory_space=pltpu.SMEM)] + specs(fwd) + specs(bwd) + [st_spec],
        out_specs=(pl.BlockSpec((d_inner, q), lambda b, c: (0, fwd(b, c))),
                   pl.BlockSpec((d_inner, q), lambda b, c: (0, bwd(b, c))),
                   st_spec),
        scratch_shapes=[pltpu.VMEM((2, d_inner, SSM_STATE), F32)],
        compiler_params=_params(("arbitrary", "arbitrary"), vmem),
        name="ssd",
    )(d_skip.astype(F32), xt, bc, ct, aux, acum, xt, bc, ct, aux, acum, init_state)


def _pool_body(slabs, cur_ref, mask_refs, icnt_ref, pw_ref, ps_ref, o_ref):
    gw = cur_ref.shape[1] // MIX_GROUPS
    for g in range(MIX_GROUPS):
        cols = slice(g * gw, (g + 1) * gw)
        xin = slabs(g, cols)
        s = jnp.dot(mask_refs[g][...], xin, preferred_element_type=F32)
        inv = icnt_ref[:, g * V7X_LANES:(g + 1) * V7X_LANES]
        p = s * jnp.concatenate([inv] * (gw // V7X_LANES), axis=1)
        dlt = p - cur_ref[:, cols].astype(F32)
        y = jnp.dot(dlt.astype(BF16), pw_ref[g], preferred_element_type=F32) * ps_ref[:, cols]
        o_ref[:, cols] = y.astype(BF16)


def _pool2d_kernel(prev_ref, cur_ref, next_ref, m0, m1, m2, m3, icnt_ref, pw_ref, ps_ref, o_ref):
    tile = cur_ref.shape[0]

    def slabs(g, cols):
        w = POOL_WINDOWS[g]
        up, dn = w // 2, w - w // 2 - 1
        parts = [prev_ref[tile - GRID_W * up:, cols], cur_ref[:, cols]]
        if dn:
            parts.append(next_ref[:GRID_W * dn, cols])
        return jnp.concatenate(parts, axis=0)

    _pool_body(slabs, cur_ref, (m0, m1, m2, m3), icnt_ref, pw_ref, ps_ref, o_ref)


def _pool1d_kernel(cur_ref, m0, m1, m2, m3, icnt_ref, pw_ref, ps_ref, o_ref):
    _pool_body(lambda g, cols: cur_ref[:, cols], cur_ref, (m0, m1, m2, m3), icnt_ref, pw_ref, ps_ref, o_ref)


def _window_ok(out_pos, in_pos, w):
    lo = out_pos[:, None] - w // 2
    return (in_pos[None, :] >= lo) & (in_pos[None, :] < lo + w)


def _pool_tables_2d(tile):
    rows = tile // GRID_W
    t = jnp.arange(tile)
    masks, invs = [], []
    for w in POOL_WINDOWS:
        up, dn = w // 2, w - w // 2 - 1
        u = jnp.arange(tile + GRID_W * (up + dn))
        r_in, c_in = u // GRID_W - up, u % GRID_W
        base = _window_ok(t // GRID_W, r_in, w) & _window_ok(t % GRID_W, c_in, w)
        var = jnp.stack([base & (r_in >= 0)[None, :], base, base & (r_in < rows)[None, :]])
        masks.append(var.astype(BF16))
        invs.append(1.0 / jnp.sum(var.astype(F32), axis=2))
    inv = jnp.stack(invs, axis=1)
    inv = jnp.broadcast_to(inv[:, :, :, None], inv.shape + (V7X_LANES,))
    inv = jnp.transpose(inv, (0, 2, 1, 3)).reshape(3, tile, MIX_GROUPS * V7X_LANES)
    return masks, inv


def _pool_tables_1d(n):
    t = jnp.arange(n)
    masks, invs = [], []
    for w in POOL_WINDOWS:
        m = _window_ok(t, t, w)
        masks.append(m.astype(BF16))
        invs.append(1.0 / jnp.sum(m.astype(F32), axis=1))
    inv = jnp.stack(invs, axis=0)
    inv = jnp.broadcast_to(inv[:, :, None], inv.shape + (V7X_LANES,))
    return masks, jnp.transpose(inv, (1, 0, 2)).reshape(n, MIX_GROUPS * V7X_LANES)


def _pool2d(u, pool_w, pool_scale, *, seq_len):
    t = u.shape[0]
    d = pool_scale.shape[0]
    tile = TM_ROW
    nt, per_img = t // tile, seq_len // tile
    masks, inv = _pool_tables_2d(tile)

    def variant(i):
        p = i % per_img
        return jnp.where(p == 0, 0, jnp.where(p == per_img - 1, 2, 1))

    gw = d // MIX_GROUPS
    vmem = 2 * (4 * _nbytes((tile, d), BF16) + sum(_nbytes(m.shape[1:], BF16) for m in masks)
                + _nbytes(inv.shape[1:], F32) + _nbytes((MIX_GROUPS, gw, gw), BF16))
    return pl.pallas_call(
        _pool2d_kernel,
        out_shape=jax.ShapeDtypeStruct((t, d), BF16),
        grid=(nt,),
        in_specs=[pl.BlockSpec((tile, d), lambda i: (jnp.maximum(i - 1, 0), U_POOL)),
                  pl.BlockSpec((tile, d), lambda i: (i, U_POOL)),
                  pl.BlockSpec((tile, d), lambda i: (jnp.minimum(i + 1, nt - 1), U_POOL))]
                 + [pl.BlockSpec((None,) + m.shape[1:], lambda i: (variant(i), 0, 0)) for m in masks]
                 + [pl.BlockSpec((None,) + inv.shape[1:], lambda i: (variant(i), 0, 0)),
                    pl.BlockSpec((MIX_GROUPS, gw, gw), lambda i: (0, 0, 0)),
                    pl.BlockSpec((1, d), lambda i: (0, 0))],
        out_specs=pl.BlockSpec((tile, d), lambda i: (i, 0)),
        compiler_params=_params(("parallel",), vmem),
        name="pool2d",
    )(u, u, u, *masks, inv, pool_w.astype(BF16), pool_scale.astype(F32).reshape(1, d))


def _pool1d(u, pool_w, pool_scale, *, seq_len):
    t = u.shape[0]
    d = pool_scale.shape[0]
    masks, inv = _pool_tables_1d(seq_len)
    gw = d // MIX_GROUPS
    vmem = 2 * (2 * _nbytes((seq_len, d), BF16) + 4 * _nbytes((seq_len, seq_len), BF16)
                + _nbytes(inv.shape, F32) + _nbytes((MIX_GROUPS, gw, gw), BF16))
    return pl.pallas_call(
        _pool1d_kernel,
        out_shape=jax.ShapeDtypeStruct((t, d), BF16),
        grid=(t // seq_len,),
        in_specs=[pl.BlockSpec((seq_len, d), lambda i: (i, U_POOL))]
                 + [pl.BlockSpec(m.shape, lambda i: (0, 0)) for m in masks]
                 + [pl.BlockSpec(inv.shape, lambda i: (0, 0)),
                    pl.BlockSpec((MIX_GROUPS, gw, gw), lambda i: (0, 0, 0)),
                    pl.BlockSpec((1, d), lambda i: (0, 0))],
        out_specs=pl.BlockSpec((seq_len, d), lambda i: (i, 0)),
        compiler_params=_params(("parallel",), vmem),
        name="pool1d",
    )(u, *masks, inv, pool_w.astype(BF16), pool_scale.astype(F32).reshape(1, d))


def _dft_tables(n_pos, gw):
    n1, n2 = FOURIER_N1, n_pos // FOURIER_N1
    c = np.arange(gw)
    ang = 2.0 * np.pi * ((c[:, None] * c[None, :]) % gw) / gw
    cs = np.concatenate([np.cos(ang), -np.sin(ang)], axis=1)
    a = np.arange(n1)
    ang1 = 2.0 * np.pi * ((a[:, None] * a[None, :]) % n1) / n1
    wc, ws = np.cos(ang1), np.sin(ang1)
    w1 = np.block([[wc, ws], [-ws, wc]])
    k = a[:, None, None] + n1 * np.arange(n2)[None, :, None]
    ang2 = 2.0 * np.pi * ((k * np.arange(n2)[None, None, :]) % n_pos) / n_pos
    m2 = np.concatenate([np.cos(ang2), np.sin(ang2)], axis=2)
    to_bf = lambda v: jnp.asarray(v, dtype=F32).astype(BF16)
    return to_bf(cs), to_bf(w1), to_bf(m2)


def _f1_kernel(u_ref, cs_ref, w1_ref, o_ref, z_scr):
    nb, n1, d = u_ref.shape
    gw = d // MIX_GROUPS
    u = u_ref[...].reshape(nb * n1, d)
    for g in range(MIX_GROUPS):
        z = jnp.dot(u[:, g * gw:(g + 1) * gw], cs_ref[...], preferred_element_type=F32)
        z_scr[:, g * gw:(g + 1) * gw] = z[:, :gw].astype(BF16)
        z_scr[:, d + g * gw:d + (g + 1) * gw] = z[:, gw:].astype(BF16)
    for j in range(nb):
        zz = jnp.concatenate([z_scr[j * n1:(j + 1) * n1, :d], z_scr[j * n1:(j + 1) * n1, d:]], axis=0)
        o_ref[j] = jnp.dot(w1_ref[...], zz, preferred_element_type=F32).astype(BF16)


def _f2_kernel(a_ref, m2_ref, o_ref, *, scale):
    for j in range(a_ref.shape[0]):
        o_ref[j] = (jnp.dot(m2_ref[j], a_ref[j], preferred_element_type=F32) * scale).astype(BF16)


def _fourier_lat(u, *, batch, seq_len, d):
    n1, n2 = FOURIER_N1, seq_len // FOURIER_N1
    gw = d // MIX_GROUPS
    cs, w1, m2 = _dft_tables(seq_len, gw)
    ut = u[:, U_FOUR * d:(U_FOUR + 1) * d].reshape(batch, n1, n2, d).transpose(0, 2, 1, 3)
    nb = F1_BLOCK
    vmem1 = 2 * (_nbytes((nb, n1, d), BF16) + _nbytes((nb, 2 * n1, d), BF16)) + _nbytes((nb * n1, 2 * d), BF16) \
        + 2 * _nbytes((nb * n1, 2 * gw), F32)
    a = pl.pallas_call(
        _f1_kernel,
        out_shape=jax.ShapeDtypeStruct((batch, n2, 2 * n1, d), BF16),
        grid=(batch, n2 // nb),
        in_specs=[pl.BlockSpec((None, nb, n1, d), lambda b, j: (b, j, 0, 0)),
                  pl.BlockSpec(cs.shape, lambda b, j: (0, 0)),
                  pl.BlockSpec(w1.shape, lambda b, j: (0, 0))],
        out_specs=pl.BlockSpec((None, nb, 2 * n1, d), lambda b, j: (b, j, 0, 0)),
        scratch_shapes=[pltpu.VMEM((nb * n1, 2 * d), BF16)],
        compiler_params=_params(("parallel", "parallel"), vmem1),
        name="fourier_stage1",
    )(ut, cs, w1)
    at = a.reshape(batch, n2, 2, n1, d).transpose(0, 3, 2, 1, 4).reshape(batch, n1, 2 * n2, d)
    kb = F2_BLOCK
    vmem2 = 2 * (_nbytes((kb, 2 * n2, d), BF16) + _nbytes((kb, n2, 2 * n2), BF16) + _nbytes((kb, n2, d), BF16))
    x2 = pl.pallas_call(
        functools.partial(_f2_kernel, scale=1.0 / math.sqrt(seq_len * gw)),
        out_shape=jax.ShapeDtypeStruct((batch, n1, n2, d), BF16),
        grid=(batch, n1 // kb),
        in_specs=[pl.BlockSpec((None, kb, 2 * n2, d), lambda b, j: (b, j, 0, 0)),
                  pl.BlockSpec((kb, n2, 2 * n2), lambda b, j: (j, 0, 0))],
        out_specs=pl.BlockSpec((None, kb, n2, d), lambda b, j: (b, j, 0, 0)),
        compiler_params=_params(("parallel", "parallel"), vmem2),
        name="fourier_stage2",
    )(at, m2)
    return x2.transpose(0, 2, 1, 3).reshape(batch * seq_len, d)


def _fourier_ctx_kernel(u_ref, cs_ref, cn_ref, o_ref, *, scale):
    d = u_ref.shape[1]
    gw = d // MIX_GROUPS
    zr, zi = [], []
    for g in range(MIX_GROUPS):
        z = jnp.dot(u_ref[:, g * gw:(g + 1) * gw], cs_ref[...], preferred_element_type=F32)
        zr.append(z[:, :gw].astype(BF16))
        zi.append(z[:, gw:].astype(BF16))
    zz = jnp.concatenate([jnp.concatenate(zr, axis=1), jnp.concatenate(zi, axis=1)], axis=0)
    o_ref[...] = (jnp.dot(cn_ref[...], zz, preferred_element_type=F32) * scale).astype(BF16)


def _fourier_ctx(u, *, seq_len, d):
    t = u.shape[0]
    gw = d // MIX_GROUPS
    cs, _, _ = _dft_tables(FOURIER_N1 * 2, gw)
    n = np.arange(seq_len)
    ang = 2.0 * np.pi * ((n[:, None] * n[None, :]) % seq_len) / seq_len
    cn = jnp.asarray(np.concatenate([np.cos(ang), np.sin(ang)], axis=1), dtype=F32).astype(BF16)
    vmem = 2 * (2 * _nbytes((seq_len, d), BF16) + _nbytes(cs.shape, BF16) + _nbytes(cn.shape, BF16)) \
        + 4 * _nbytes((seq_len, d), F32)
    return pl.pallas_call(
        functools.partial(_fourier_ctx_kernel, scale=1.0 / math.sqrt(seq_len * gw)),
        out_shape=jax.ShapeDtypeStruct((t, d), BF16),
        grid=(t // seq_len,),
        in_specs=[pl.BlockSpec((seq_len, d), lambda i: (i, U_FOUR)),
                  pl.BlockSpec(cs.shape, lambda i: (0, 0)),
                  pl.BlockSpec(cn.shape, lambda i: (0, 0))],
        out_specs=pl.BlockSpec((seq_len, d), lambda i: (i, 0)),
        compiler_params=_params(("parallel",), vmem),
        name="fourier_ctx",
    )(u, cs, cn)


def _merge_kernel(ya_ref, yb_ref, yf_ref, ybw_ref, z_ref, g0_ref, g1_ref, g2_ref, lat_ref, mod_ref,
                  snw_ref, wb_ref, wo_ref, n2w_ref, rw_ref, rb_ref, tri_ref, ones_ref, cnt0_ref,
                  lat_o, h2_o, idx_o, rank_o, wts_o, cnt_o, run_scr):
    tm = lat_ref.shape[0]

    @pl.when(pl.program_id(0) == 0)
    def _():
        run_scr[...] = cnt0_ref[...]

    y_ssm = (yf_ref[...].astype(F32) + ybw_ref[...].astype(F32)).T
    yc = _rms(y_ssm * _silu(z_ref[...].astype(F32)), snw_ref[...])
    merged = jnp.zeros(lat_ref.shape, F32)
    for k, (y, g_ref) in enumerate(((ya_ref[...], g0_ref), (yb_ref[...], g1_ref), (yc.astype(BF16), g2_ref))):
        proj = jnp.dot(y, wb_ref[k], preferred_element_type=F32)
        merged = merged + jax.nn.sigmoid(g_ref[...].astype(F32)) * proj
    ol = jnp.dot(merged.astype(BF16), wo_ref[...], preferred_element_type=F32)
    lat = lat_ref[...] + mod_ref[2:3, :] * ol
    lat_o[...] = lat
    h2 = _rms(lat, n2w_ref[...]) * (1.0 + mod_ref[4:5, :]) + mod_ref[3:4, :]
    _store_tile_rows(h2_o, h2)

    h_hi = h2.astype(BF16)
    h_lo = (h2 - h_hi.astype(F32)).astype(BF16)
    nt_dims = (((1,), (1,)), ((), ()))
    lt = (lax.dot_general(rw_ref[0], h_hi, nt_dims, preferred_element_type=F32)
          + lax.dot_general(rw_ref[0], h_lo, nt_dims, preferred_element_type=F32)
          + lax.dot_general(rw_ref[1], h_hi, nt_dims, preferred_element_type=F32))
    lt = lt + jnp.concatenate([rb_ref[...]] * (tm // V7X_LANES), axis=1)
    ei = lax.broadcasted_iota(I32, lt.shape, 0).astype(F32)
    vals, idxs = [], []
    for _ in range(TOP_K):
        m = jnp.max(lt, axis=0, keepdims=True)
        sel = jnp.min(jnp.where(lt == m, ei, float(V7X_LANES)), axis=0, keepdims=True)
        vals.append(m)
        idxs.append(sel)
        lt = jnp.where(ei == sel, -jnp.inf, lt)
    exps = [jnp.exp(v - vals[0]) for v in vals]
    den = exps[0] + exps[1] + exps[2] + exps[3]
    onehots = [ei == s for s in idxs]
    chosen = jnp.zeros(lt.shape, F32)
    for oh in onehots:
        chosen = chosen + jnp.where(oh, 1.0, 0.0)
    chosen_b = chosen.astype(BF16)
    before = jnp.dot(chosen_b, tri_ref[...], preferred_element_type=F32) \
        + jnp.concatenate([run_scr[...]] * (tm // V7X_LANES), axis=1)
    for k in range(TOP_K):
        idx_o[k:k + 1, :] = idxs[k].astype(I32)
        rank_o[k:k + 1, :] = jnp.sum(jnp.where(onehots[k], before, 0.0), axis=0, keepdims=True).astype(I32)
        wts_o[k:k + 1, :] = exps[k] / den
    pad = V7X_SUBLANES - TOP_K
    idx_o[TOP_K:, :] = jnp.zeros((pad, tm), I32)
    rank_o[TOP_K:, :] = jnp.zeros((pad, tm), I32)
    wts_o[TOP_K:, :] = jnp.zeros((pad, tm), F32)
    run_scr[...] = run_scr[...] + jnp.dot(chosen_b, ones_ref[...], preferred_element_type=F32)
    cnt_o[...] = run_scr[...]


def _merge(ya, yb, yf_t, yb_t, u, lat, modv, ssm_norm_w, wb, wo, norm2_w, rw, rb, cnt0, *,
           tm, seg0, tiles_per_seg):
    t, d = lat.shape
    ne = V7X_LANES
    tri = (jnp.arange(tm)[:, None] < jnp.arange(tm)[None, :]).astype(BF16)
    ones = jnp.ones((tm, ne), BF16)
    row = lambda c: pl.BlockSpec((tm, d), lambda i: (i, c))
    col = pl.BlockSpec((d, tm), lambda i: (0, i))
    const = lambda shape: pl.BlockSpec(shape, lambda i: (0,) * len(shape), pipeline_mode=pl.Buffered(1))
    tok = pl.BlockSpec((V7X_SUBLANES, tm), lambda i: (0, i))
    vmem = (2 * (4 * _nbytes((tm, d), BF16) + 2 * _nbytes((d, tm), BF16) + 2 * _nbytes((tm, d), BF16)
                 + 3 * _nbytes((tm, d), F32))
            + 4 * _nbytes((d, d), BF16) + _nbytes((2, ne, d), BF16) + _nbytes((tm, tm), BF16)
            + 8 * _nbytes((tm, d), F32))
    return pl.pallas_call(
        _merge_kernel,
        out_shape=(jax.ShapeDtypeStruct((t, d), F32), jax.ShapeDtypeStruct((t * V7X_SUBLANES, V7X_LANES), F32),
                   jax.ShapeDtypeStruct((V7X_SUBLANES, t), I32), jax.ShapeDtypeStruct((V7X_SUBLANES, t), I32),
                   jax.ShapeDtypeStruct((V7X_SUBLANES, t), F32), jax.ShapeDtypeStruct((ne, ne), F32)),
        grid=(t // tm,),
        in_specs=[row(0), row(0), col, col, row(U_Z), row(U_GATE), row(U_GATE + 1), row(U_GATE + 2),
                  row(0),
                  pl.BlockSpec((None, V7X_SUBLANES, d), lambda i: (seg0 + i // tiles_per_seg, 0, 0)),
                  const((1, d)), const((N_BRANCH, d, d)), const((d, d)), const((1, d)),
                  const((2, ne, d)), const((ne, ne)), const((tm, tm)), const((tm, ne)), const((ne, ne))],
        out_specs=(row(0), pl.BlockSpec((tm * V7X_SUBLANES, V7X_LANES), lambda i: (i, 0)), tok, tok, tok,
                   pl.BlockSpec((ne, ne), lambda i: (0, 0))),
        scratch_shapes=[pltpu.VMEM((ne, ne), F32)],
        compiler_params=_params(("arbitrary",), vmem),
        name="merge",
    )(ya, yb, yf_t, yb_t, u, u, u, u, lat, modv, ssm_norm_w.astype(F32).reshape(1, d), wb, wo,
      norm2_w.astype(F32).reshape(1, d), rw, rb, tri, ones, cnt0)


def _tile_row(r):
    return pl.ds(pl.multiple_of(r * V7X_SUBLANES, V7X_SUBLANES), V7X_SUBLANES)


def _dispatch_kernel(dest_ref, h_ref, *rest):
    xs_ref, sem = rest[-2], rest[-1]
    rows = h_ref.shape[0]
    n = rows // V7X_SUBLANES

    def start(t, carry):
        for k in range(TOP_K):
            pltpu.make_async_copy(h_ref.at[_tile_row(t), :], xs_ref.at[_tile_row(dest_ref[0, TOP_K * t + k]), :],
                                  sem).start(priority=k % DMA_PRIORITIES)
        return carry

    lax.fori_loop(0, n, start, 0, unroll=DMA_UNROLL)
    for k in range(TOP_K):
        pltpu.make_async_copy(h_ref, xs_ref.at[pl.ds(0, rows), :], sem).wait()


def _dispatch(h2, dest_tiles, n_slots, xs_prev=None):
    t, d = h2.shape[0] // V7X_SUBLANES, V7X_LANES
    td = T_DMA
    in_specs = [pl.BlockSpec((None, 1, TOP_K * td), lambda i: (i, 0, 0), memory_space=pltpu.SMEM),
                pl.BlockSpec((td * V7X_SUBLANES, d), lambda i: (i, 0))]
    args = [dest_tiles, h2]
    aliases = {}
    if xs_prev is not None:
        in_specs.append(pl.BlockSpec(memory_space=pl.ANY))
        args.append(xs_prev)
        aliases = {2: 0}
    return pl.pallas_call(
        _dispatch_kernel,
        out_shape=jax.ShapeDtypeStruct((n_slots * V7X_SUBLANES, d), F32),
        grid=(t // td,),
        in_specs=in_specs,
        out_specs=pl.BlockSpec(memory_space=pl.ANY),
        scratch_shapes=[pltpu.SemaphoreType.DMA(())],
        input_output_aliases=aliases,
        compiler_params=_params(("arbitrary",), 2 * _nbytes((td * V7X_SUBLANES, d), F32)),
        name="dispatch",
    )(*args)


EXPERT_NB = 256


def _expert_kernel(be_ref, nv_ref, nxt_ref, xs_ref, w1_hbm, b1_ref, w2_hbm, b2_ref, ys_ref,
                   w1_stage, w2_stage, w1_scr, w2_scr, act_scr, sems, *, layer):
    i = pl.program_id(0)
    d, dff = w1_stage.shape[0], w2_stage.shape[0]
    e = be_ref[i]
    changed = (i == 0) | (e != be_ref[jnp.maximum(i - 1, 0)])

    def weight_copies(ex):
        return (pltpu.make_async_copy(w1_hbm.at[layer, ex], w1_stage, sems.at[0]),
                pltpu.make_async_copy(w2_hbm.at[layer, ex], w2_stage, sems.at[1]))

    @pl.when(i == 0)
    def _():
        for cp in weight_copies(e):
            cp.start()

    @pl.when(changed)
    def _():
        for cp in weight_copies(e):
            cp.wait()
        rows = 128

        def cast1(r, carry):
            sl = pl.ds(pl.multiple_of(r * rows, rows), rows)
            w1_scr[sl, :] = w1_stage[sl, :].astype(BF16)
            return carry

        def cast2(r, carry):
            sl = pl.ds(pl.multiple_of(r * rows, rows), rows)
            w2_scr[sl, :] = w2_stage[sl, :].astype(BF16)
            return carry

        lax.fori_loop(0, d // rows, cast1, 0)
        lax.fori_loop(0, dff // rows, cast2, 0)
        nxt = nxt_ref[i]

        @pl.when(nxt >= 0)
        def _():
            for cp in weight_copies(nxt):
                cp.start()

    nv = nv_ref[i]
    half = MOE_BLOCK // 2

    def mlp(rows):
        xs_v = xs_ref.at[pl.ds(0, rows * V7X_SUBLANES), :]
        ys_v = ys_ref.at[pl.ds(0, rows * V7X_SUBLANES), :]
        x = jnp.concatenate(_load_tile_rows(xs_v, rows), axis=1)
        ridx = lax.broadcasted_iota(I32, x.shape, 0)
        x = jnp.where(ridx < nv, x, 0.0).astype(BF16)
        for c in range(dff // EXPERT_NB):
            cg = slice(c * EXPERT_NB, (c + 1) * EXPERT_NB)
            cl = slice(dff + c * EXPERT_NB, dff + (c + 1) * EXPERT_NB)
            gate = jnp.dot(x, w1_scr[:, cg], preferred_element_type=F32) + b1_ref[:, cg]
            lin = jnp.dot(x, w1_scr[:, cl], preferred_element_type=F32) + b1_ref[:, cl]
            gate = jnp.minimum(gate, SWIGLU_LIMIT)
            lin = jnp.clip(lin, -SWIGLU_LIMIT, SWIGLU_LIMIT)
            act = gate * jax.nn.sigmoid(SWIGLU_ALPHA * gate) * (lin + 1.0)
            act_scr[0:rows, cg] = act.astype(BF16)
        _store_tile_rows(ys_v, jnp.dot(act_scr[0:rows, :], w2_scr[...], preferred_element_type=F32) + b2_ref[...])

    def zero_from(row0):
        tail = ys_ref.at[pl.ds(row0 * V7X_SUBLANES, (MOE_BLOCK - row0) * V7X_SUBLANES), :]
        tail[...] = jnp.zeros(tail.shape, F32)

    @pl.when(nv == 0)
    def _():
        zero_from(0)

    @pl.when((nv > 0) & (nv <= half))
    def _():
        mlp(half)
        zero_from(half)

    @pl.when(nv > half)
    def _():
        mlp(MOE_BLOCK)


def _experts(xs, block_e, nvalid, next_e, w1, b1, w2, b2, *, layer):
    n_slots = xs.shape[0] // V7X_SUBLANES
    depth, ne, d, two_dff = w1.shape
    dff = two_dff // 2
    n_blocks = n_slots // MOE_BLOCK
    blk = pl.BlockSpec((MOE_BLOCK * V7X_SUBLANES, V7X_LANES), lambda i, be, nv, nx: (i, 0))
    grid_spec = pltpu.PrefetchScalarGridSpec(
        num_scalar_prefetch=3,
        grid=(n_blocks,),
        in_specs=[blk,
                  pl.BlockSpec(memory_space=pl.ANY),
                  pl.BlockSpec((None, None, 1, two_dff), lambda i, be, nv, nx: (layer, be[i], 0, 0)),
                  pl.BlockSpec(memory_space=pl.ANY),
                  pl.BlockSpec((None, None, 1, d), lambda i, be, nv, nx: (layer, be[i], 0, 0))],
        out_specs=blk,
        scratch_shapes=[pltpu.VMEM((d, two_dff), F32), pltpu.VMEM((dff, d), F32),
                        pltpu.VMEM((d, two_dff), BF16), pltpu.VMEM((dff, d), BF16),
                        pltpu.VMEM((MOE_BLOCK, dff), BF16), pltpu.SemaphoreType.DMA((2,))],
    )
    vmem = (_nbytes((d, two_dff), F32) + _nbytes((dff, d), F32) + 4 * _nbytes((MOE_BLOCK, d), F32)
            + _nbytes((d, two_dff), BF16) + _nbytes((dff, d), BF16) + 6 * _nbytes((MOE_BLOCK, dff), F32))
    return pl.pallas_call(
        functools.partial(_expert_kernel, layer=layer),
        out_shape=jax.ShapeDtypeStruct(xs.shape, F32),
        grid_spec=grid_spec,
        compiler_params=_params(("arbitrary",), vmem),
        name="experts",
    )(block_e, nvalid, next_e, xs, w1, b1.reshape(depth, ne, 1, two_dff), w2, b2.reshape(depth, ne, 1, d))


def _combine_kernel(dest_ref, dnext_ref, lat_ref, w_ref, mod_ref, fnw_ref, ys_ref, o_ref, buf, sems, *,
                    final_norm):
    i = pl.program_id(0)
    n = lat_ref.shape[0]
    slot = i % 2

    def issue(d_ref, s):
        def start(t, carry):
            for k in range(TOP_K):
                pltpu.make_async_copy(ys_ref.at[_tile_row(d_ref[0, TOP_K * t + k]), :],
                                      buf.at[s, k, _tile_row(t), :], sems.at[s]).start(priority=k % DMA_PRIORITIES)
            return carry

        lax.fori_loop(0, n, start, 0, unroll=DMA_UNROLL)

    @pl.when(i == 0)
    def _():
        issue(dest_ref, slot)

    @pl.when(i + 1 < pl.num_programs(0))
    def _():
        issue(dnext_ref, 1 - slot)

    for k in range(TOP_K):
        pltpu.make_async_copy(ys_ref.at[pl.ds(0, n * V7X_SUBLANES), :], buf.at[slot, k], sems.at[slot]).wait()
    parts = [jnp.zeros((n, V7X_LANES), F32)] * V7X_SUBLANES
    for k in range(TOP_K):
        wk = w_ref[:, k:k + 1]
        rows = _load_tile_rows(buf.at[slot, k], n)
        parts = [p + wk * r for p, r in zip(parts, rows)]
    out = lat_ref[...] + mod_ref[5:6, :] * jnp.concatenate(parts, axis=1)
    if final_norm:
        out = _rms(out, fnw_ref[...])
    o_ref[...] = out


def _combine(lat, dest_tiles, wts, modv, fnw, ys, *, seg0, tiles_per_seg, final_norm):
    t, d = lat.shape
    td = T_DMA
    nt = t // td
    vmem = 2 * (2 * _nbytes((td, d), F32) + _nbytes((td, V7X_LANES), F32)) + 2 * _nbytes((TOP_K, td, d), F32) \
        + 4 * _nbytes((td, d), F32)
    return pl.pallas_call(
        functools.partial(_combine_kernel, final_norm=final_norm),
        out_shape=jax.ShapeDtypeStruct((t, d), F32),
        grid=(nt,),
        in_specs=[pl.BlockSpec((None, 1, TOP_K * td), lambda i: (i, 0, 0), memory_space=pltpu.SMEM),
                  pl.BlockSpec((None, 1, TOP_K * td), lambda i: (jnp.minimum(i + 1, nt - 1), 0, 0),
                               memory_space=pltpu.SMEM),
                  pl.BlockSpec((td, d), lambda i: (i, 0)),
                  pl.BlockSpec((td, V7X_SUBLANES), lambda i: (i, 0)),
                  pl.BlockSpec((None, V7X_SUBLANES, d), lambda i: (seg0 + i // tiles_per_seg, 0, 0)),
                  pl.BlockSpec((1, d), lambda i: (0, 0)),
                  pl.BlockSpec(memory_space=pl.ANY)],
        out_specs=pl.BlockSpec((td, d), lambda i: (i, 0)),
        scratch_shapes=[pltpu.VMEM((2, TOP_K, td * V7X_SUBLANES, V7X_LANES), F32),
                        pltpu.SemaphoreType.DMA((2,))],
        compiler_params=_params(("arbitrary",), vmem),
        name="combine",
    )(dest_tiles, dest_tiles, lat, wts, modv, fnw.astype(F32).reshape(1, d), ys)


def _routing_tables(counts, n_blocks):
    padded = (counts + MOE_BLOCK - 1) // MOE_BLOCK * MOE_BLOCK
    pends = jnp.cumsum(padded)
    pstarts = pends - padded
    bstart = jnp.arange(n_blocks, dtype=I32) * MOE_BLOCK
    block_e = jnp.minimum(jnp.sum((pends[None, :] <= bstart[:, None]).astype(I32), axis=1), N_EXPERTS - 1)
    nvalid = jnp.clip(counts[block_e] - (bstart - pstarts[block_e]), 0, MOE_BLOCK)
    nvalid = jnp.where(bstart < pends[-1], nvalid, 0)
    blk = jnp.arange(n_blocks, dtype=I32)
    is_change = jnp.concatenate([jnp.zeros((1,), bool), block_e[1:] != block_e[:-1]])
    change_at = jnp.where(is_change, blk, n_blocks)
    next_change = lax.cummin(change_at, axis=0, reverse=True)
    after = jnp.concatenate([next_change[1:], jnp.full((1,), n_blocks, I32)])
    next_e = jnp.where(after < n_blocks, block_e[jnp.minimum(after, n_blocks - 1)], -1)
    return pstarts, block_e.astype(I32), nvalid.astype(I32), next_e.astype(I32)


def _dest_tiles(idx, rank, pstarts):
    e = idx[:TOP_K]
    start = jnp.zeros(e.shape, I32)
    for k in range(N_EXPERTS):
        start = jnp.where(e == k, pstarts[k], start)
    dest = (start + rank[:TOP_K]).astype(I32).T
    return dest.reshape(-1, 1, TOP_K * T_DMA)


def kernel(x, c, ctx, c_ctx, w_ada, b_ada, norm1_w, norm2_w, w_in, pool_w, pool_scale, conv_w, conv_b,
           dt_bias, a_log, d_skip, ssm_norm_w, w_branch, w_out, router_w, router_b, moe_w1, moe_b1,
           moe_w2, moe_b2, final_norm_w):
    batch, n, d = x.shape
    n_ctx = ctx.shape[1]
    depth = w_ada.shape[0]
    d_inner = SSM_HEADS * SSM_HEAD_DIM
    conv_dim = conv_w.shape[2]
    off_four, off_z, off_xbc = d, 2 * d, 3 * d
    off_dt = off_xbc + conv_dim
    off_gate = off_dt + 2 * SSM_HEADS
    assert batch + 1 <= V7X_SUBLANES and d_inner == d and conv_dim == 2 * d
    assert d == V7X_SUBLANES * V7X_LANES

    lat = x.reshape(batch * n, d).astype(F32)
    cx = ctx.reshape(batch * n_ctx, d).astype(F32)
    cvec = jnp.concatenate([c.astype(F32), c_ctx.astype(F32)[None, :],
                            jnp.zeros((V7X_SUBLANES - batch - 1, d), F32)], axis=0)
    mods = _ada(cvec, w_ada.astype(F32), b_ada.astype(F32))
    seg_ctx = batch
    zero_state = jnp.zeros((batch, 2, d_inner, SSM_STATE), F32)
    zero_cnt = jnp.zeros((V7X_LANES, V7X_LANES), F32)

    for i in range(depth):
        last = i == depth - 1
        modv = mods[i].reshape(V7X_SUBLANES, N_ADA, d)[:batch + 1]
        modv = jnp.pad(modv, ((0, 0), (0, V7X_SUBLANES - N_ADA), (0, 0)))
        wi = w_in[i]
        w_main = jnp.concatenate([wi[:, off_xbc:off_dt], wi[:, :off_z], wi[:, off_z:off_xbc], wi[:, off_gate:]],
                                 axis=1).astype(BF16)
        w_dt = jnp.pad(wi[:, off_dt:off_gate], ((0, 0), (0, V7X_LANES - 2 * SSM_HEADS))).astype(BF16)

        u_l, dtr_l = _inproj(lat, modv, norm1_w[i], w_main, w_dt, tm=TM_LAT, seg0=0, tiles_per_seg=n // TM_LAT)
        u_c, dtr_c = _inproj(cx, modv, norm1_w[i], w_main, w_dt, tm=batch * n_ctx, seg0=seg_ctx, tiles_per_seg=1)

        conv_args = (conv_w[i], conv_b[i], dt_bias[i], a_log[i])
        cv_c = _conv(u_c, dtr_c, *conv_args, tc=n_ctx, seq_len=n_ctx, d_inner=d_inner)
        cv_l = _conv(u_l, dtr_l, *conv_args, tc=TM_ROW, seq_len=n, d_inner=d_inner)
        yf_c, yb_c, st = _ssd(cv_c, d_skip[i], zero_state, batch=batch, seq_len=n_ctx)
        yf_l, yb_l, _ = _ssd(cv_l, d_skip[i], st, batch=batch, seq_len=n)

        ya_l = _pool2d(u_l, pool_w[i], pool_scale[i], seq_len=n)
        fb_l = _fourier_lat(u_l, batch=batch, seq_len=n, d=d)

        wb = w_branch[i].astype(BF16)
        wo = w_out[i].astype(BF16)
        rw_t = jnp.pad(router_w[i].astype(F32).T, ((0, V7X_LANES - N_EXPERTS), (0, 0)))
        rw_hi = rw_t.astype(BF16)
        rw = jnp.stack([rw_hi, (rw_t - rw_hi.astype(F32)).astype(BF16)])
        rb = jnp.pad(router_b[i].astype(F32), (0, V7X_LANES - N_EXPERTS), constant_values=-1e30)
        rb = jnp.broadcast_to(rb[:, None], (V7X_LANES, V7X_LANES))
        merge_w = (ssm_norm_w[i], wb, wo, norm2_w[i], rw, rb)
        lat_m, h2_l, idx_l, rank_l, wts_l, cnt = _merge(
            ya_l, fb_l, yf_l, yb_l, u_l, lat, modv, *merge_w, zero_cnt,
            tm=TM_ROW, seg0=0, tiles_per_seg=n // TM_ROW)
        n_tok = batch * n
        if not last:
            ya_c = _pool1d(u_c, pool_w[i], pool_scale[i], seq_len=n_ctx)
            fb_c = _fourier_ctx(u_c, seq_len=n_ctx, d=d)
            cx_m, h2_c, idx_c, rank_c, wts_c, cnt = _merge(
                ya_c, fb_c, yf_c, yb_c, u_c, cx, modv, *merge_w, cnt,
                tm=batch * n_ctx, seg0=seg_ctx, tiles_per_seg=1)
            n_tok += batch * n_ctx

        n_blocks = -(-(n_tok * TOP_K) // MOE_BLOCK) + N_EXPERTS
        n_slots = n_blocks * MOE_BLOCK
        counts = cnt[:N_EXPERTS, 0].astype(I32)
        pstarts, block_e, nvalid, next_e = _routing_tables(counts, n_blocks)
        dest_l = _dest_tiles(idx_l, rank_l, pstarts)
        xs = _dispatch(h2_l, dest_l, n_slots)
        if not last:
            dest_c = _dest_tiles(idx_c, rank_c, pstarts)
            xs = _dispatch(h2_c, dest_c, n_slots, xs_prev=xs)
        ys = _experts(xs, block_e, nvalid, next_e, moe_w1, moe_b1, moe_w2, moe_b2, layer=i)
        lat = _combine(lat_m, dest_l, wts_l.T, modv, final_norm_w, ys, seg0=0, tiles_per_seg=n // T_DMA,
                       final_norm=last)
        if not last:
            cx = _combine(cx_m, dest_c, wts_c.T, modv, final_norm_w, ys, seg0=seg_ctx,
                          tiles_per_seg=batch * n_ctx // T_DMA, final_norm=False)
    return lat.reshape(batch, n, d).astype(x.dtype)
```

```python
import functools
import math

import numpy as np
import jax
import jax.numpy as jnp
from jax import lax
from jax.experimental import pallas as pl
from jax.experimental.pallas import tpu as pltpu

F32 = jnp.float32
BF16 = jnp.bfloat16
I32 = jnp.int32
HIGHEST = lax.Precision.HIGHEST

GRID_W = 64
N_ADA = 6
N_BRANCH = 3
MIX_GROUPS = 4
POOL_WINDOWS = (2, 4, 8, 16)
SSM_HEADS = 16
SSM_HEAD_DIM = 64
SSM_GROUPS = 4
SSM_STATE = 128
SSM_CONV = 5
SSM_CHUNK = 128
N_EXPERTS = 32
TOP_K = 4
SWIGLU_ALPHA = 1.702
SWIGLU_LIMIT = 7.0
RMS_EPS = 1e-6
MOE_BLOCK = 1024

V7X_VMEM_BYTES = 64 * 2**20
V7X_LANES = 128
V7X_SUBLANES = 8
V7X_BF16_SUBLANES = 16
VMEM_COMPILER_ALLOWANCE = 6 * 2**20

TM_LAT = 1024
INPROJ_COLS = 2048
TM_ROW = 512
T_DMA = 512
DMA_UNROLL = 4
DMA_PRIORITIES = 2
FOURIER_N1 = 64
F1_BLOCK = 16
F2_BLOCK = 8


def _params(semantics, vmem_bytes):
    limit = min(int(vmem_bytes) + VMEM_COMPILER_ALLOWANCE, V7X_VMEM_BYTES - 2 * 2**20)
    return pltpu.CompilerParams(dimension_semantics=semantics, vmem_limit_bytes=limit)


def _nbytes(shape, dtype):
    return int(np.prod(shape)) * jnp.dtype(dtype).itemsize


def _silu(v):
    return v * jax.nn.sigmoid(v)


def _rms(v, w):
    return v * lax.rsqrt(jnp.mean(v * v, axis=-1, keepdims=True) + RMS_EPS) * w


def _store_tile_rows(ref, v):
    n = v.shape[0]
    for j in range(V7X_SUBLANES):
        ref[pl.ds(j, n, stride=V7X_SUBLANES), :] = v[:, j * V7X_LANES:(j + 1) * V7X_LANES]


def _load_tile_rows(ref, n):
    return [ref[pl.ds(j, n, stride=V7X_SUBLANES), :] for j in range(V7X_SUBLANES)]


def _ada_kernel(c_ref, w_ref, b_ref, o_ref):
    o_ref[...] = jnp.dot(_silu(c_ref[...]), w_ref[...], preferred_element_type=F32,
                         precision=HIGHEST) + b_ref[...]


def _ada(cvec, w_ada, b_ada):
    depth, d, nd = w_ada.shape
    return pl.pallas_call(
        _ada_kernel,
        out_shape=jax.ShapeDtypeStruct((depth, V7X_SUBLANES, nd), F32),
        grid=(depth, nd // d),
        in_specs=[pl.BlockSpec((V7X_SUBLANES, d), lambda l, j: (0, 0)),
                  pl.BlockSpec((None, d, d), lambda l, j: (l, 0, j)),
                  pl.BlockSpec((None, 1, d), lambda l, j: (l, 0, j))],
        out_specs=pl.BlockSpec((None, V7X_SUBLANES, d), lambda l, j: (l, 0, j)),
        compiler_params=_params(("parallel", "parallel"), 2 * _nbytes((d, d), F32)),
        name="ada",
    )(cvec, w_ada, b_ada.reshape(depth, 1, nd))


U_XBC, U_POOL, U_FOUR, U_Z, U_GATE = 0, 2, 3, 4, 5
U_BLOCKS = 8


def _inproj_kernel(x_ref, mod_ref, nw_ref, w_ref, wdt_ref, u_ref, dt_ref, h_scr):
    @pl.when(pl.program_id(1) == 0)
    def _():
        h = _rms(x_ref[...], nw_ref[...]) * (1.0 + mod_ref[1:2, :]) + mod_ref[0:1, :]
        hb = h.astype(BF16)
        h_scr[...] = hb
        dt_ref[...] = jnp.dot(hb, wdt_ref[...], preferred_element_type=F32)

    u_ref[...] = jnp.dot(h_scr[...], w_ref[...], preferred_element_type=F32).astype(BF16)


def _inproj(x, modv, nw, w_main, w_dt, *, tm, seg0, tiles_per_seg):
    t, d = x.shape
    tn = INPROJ_COLS
    vmem = 2 * (_nbytes((tm, d), F32) + _nbytes((tm, tn), BF16) + _nbytes((d, tn), BF16)
                + _nbytes((tm, V7X_LANES), F32) + _nbytes((d, V7X_LANES), BF16)) + _nbytes((tm, d), BF16) \
        + _nbytes((tm, tn), F32)
    return pl.pallas_call(
        _inproj_kernel,
        out_shape=(jax.ShapeDtypeStruct((t, U_BLOCKS * d), BF16),
                   jax.ShapeDtypeStruct((t, V7X_LANES), F32)),
        grid=(t // tm, U_BLOCKS * d // tn),
        in_specs=[pl.BlockSpec((tm, d), lambda i, j: (i, 0)),
                  pl.BlockSpec((None, V7X_SUBLANES, d), lambda i, j: (seg0 + i // tiles_per_seg, 0, 0)),
                  pl.BlockSpec((1, d), lambda i, j: (0, 0)),
                  pl.BlockSpec((d, tn), lambda i, j: (0, j)),
                  pl.BlockSpec((d, V7X_LANES), lambda i, j: (0, 0))],
        out_specs=(pl.BlockSpec((tm, tn), lambda i, j: (i, j)),
                   pl.BlockSpec((tm, V7X_LANES), lambda i, j: (i, 0))),
        scratch_shapes=[pltpu.VMEM((tm, d), BF16)],
        compiler_params=_params(("parallel", "arbitrary"), vmem),
        name="inproj",
    )(x, modv, nw.reshape(1, d), w_main, w_dt)


CONV_HALO = V7X_BF16_SUBLANES
CONV_RB = 128
CONV_CB = 128


def _conv_kernel(cur_ref, prev_ref, next_ref, dtraw_ref, cw_ref, cb_ref, dtb_ref, alog_ref,
                 xt_ref, bc_ref, ct_ref, aux_ref, acum_ref, xp_scr, *, tc, tiles_per_seq, d_inner):
    p = pl.program_id(0) % tiles_per_seq
    xp_scr[0:CONV_HALO, :] = jnp.where(p > 0, prev_ref[...].astype(F32), 0.0)
    xp_scr[CONV_HALO:CONV_HALO + tc, :] = cur_ref[...].astype(F32)
    xp_scr[CONV_HALO + tc:, :] = jnp.where(p < tiles_per_seq - 1, next_ref[...].astype(F32), 0.0)
    n_bc = SSM_GROUPS * SSM_STATE
    for rb in range(tc // CONV_RB):
        rows = slice(rb * CONV_RB, (rb + 1) * CONV_RB)
        for cb in range(cur_ref.shape[1] // CONV_CB):
            c0 = cb * CONV_CB
            cols = slice(c0, c0 + CONV_CB)
            acc = jnp.broadcast_to(cb_ref[:, cols], (CONV_RB, CONV_CB))
            a0 = CONV_HALO - V7X_SUBLANES + rb * CONV_RB
            win = xp_scr[a0:a0 + CONV_RB + 2 * V7X_SUBLANES, cols]
            for k in range(SSM_CONV):
                s = SSM_CONV // 2 - k
                tap = win if s == 0 else pltpu.roll(win, s % win.shape[0], axis=0)
                acc = acc + cw_ref[k:k + 1, cols] * tap[V7X_SUBLANES:V7X_SUBLANES + CONV_RB, :]
            v = _silu(acc)
            if c0 < d_inner:
                xt_ref[cols, rows] = v.T.astype(BF16)
            elif c0 < d_inner + n_bc:
                bc_ref[rows, c0 - d_inner:c0 - d_inner + CONV_CB] = v.astype(BF16)
            else:
                bc_ref[rows, c0 - d_inner:c0 - d_inner + CONV_CB] = v.astype(BF16)
                cc = c0 - d_inner - n_bc
                ct_ref[cc:cc + CONV_CB, rows] = v.T.astype(BF16)

    dtr = dtraw_ref[...] + dtb_ref[...]
    dt = jnp.maximum(dtr, 0.0) + jnp.log1p(jnp.exp(-jnp.abs(dtr)))
    dta = dt * (-jnp.exp(alog_ref[...]))
    dt_t = dt.T
    dta_t = dta.T
    ki = lax.broadcasted_iota(I32, (tc, tc), 0)
    li = lax.broadcasted_iota(I32, (tc, tc), 1)
    same = (ki // SSM_CHUNK) == (li // SSM_CHUNK)
    one, zero = jnp.float32(1.0), jnp.float32(0.0)
    cumf = jnp.dot(dta_t, jnp.where(same & (ki <= li), one, zero), preferred_element_type=F32,
                   precision=HIGHEST)
    cumb = jnp.dot(dta_t, jnp.where(same & (ki >= li), one, zero), preferred_element_type=F32,
                   precision=HIGHEST)
    tot = jnp.dot(dta_t, jnp.where(same, one, zero), preferred_element_type=F32, precision=HIGHEST)
    nh = SSM_HEADS
    aux_ref[0:2 * nh, :] = dt_t[0:2 * nh, :]
    aux_ref[2 * nh:3 * nh, :] = cumf[0:nh, :]
    aux_ref[3 * nh:4 * nh, :] = cumb[nh:2 * nh, :]
    aux_ref[4 * nh:6 * nh, :] = tot[0:2 * nh, :]
    aux_ref[6 * nh:, :] = jnp.zeros((V7X_LANES - 6 * nh, tc), F32)
    hrow = lax.broadcasted_iota(I32, (V7X_LANES, tc), 0)
    acum_ref[...] = jnp.where(hrow < nh, cumf, cumb).T


def _conv(u, dt_raw, conv_w, conv_b, dt_bias, a_log, *, tc, seq_len, d_inner):
    t = u.shape[0]
    conv_dim = conv_w.shape[1]
    n_bc = SSM_GROUPS * SSM_STATE
    nt = t // tc
    hb = tc // CONV_HALO
    last_halo = t // CONV_HALO - 1
    cw = jnp.pad(conv_w.astype(F32), ((0, V7X_SUBLANES - SSM_CONV), (0, 0)))
    pad32 = lambda v: jnp.pad(v.astype(F32).reshape(1, -1), ((0, 0), (0, V7X_LANES - 2 * SSM_HEADS)))
    vmem = (2 * (_nbytes((tc, conv_dim), BF16) * 2 + _nbytes((tc, V7X_LANES), F32) * 3
                 + _nbytes((d_inner + 2 * n_bc, tc), BF16))
            + _nbytes((tc + 2 * CONV_HALO, conv_dim), F32) + 4 * _nbytes((tc, tc), F32))
    kern = functools.partial(_conv_kernel, tc=tc, tiles_per_seq=seq_len // tc, d_inner=d_inner)
    return pl.pallas_call(
        kern,
        out_shape=(jax.ShapeDtypeStruct((d_inner, t), BF16),
                   jax.ShapeDtypeStruct((t, 2 * n_bc), BF16),
                   jax.ShapeDtypeStruct((n_bc, t), BF16),
                   jax.ShapeDtypeStruct((V7X_LANES, t), F32),
                   jax.ShapeDtypeStruct((t, V7X_LANES), F32)),
        grid=(nt,),
        in_specs=[pl.BlockSpec((tc, conv_dim), lambda i: (i, U_XBC)),
                  pl.BlockSpec((CONV_HALO, conv_dim), lambda i: (jnp.maximum(i * hb - 1, 0), U_XBC)),
                  pl.BlockSpec((CONV_HALO, conv_dim), lambda i: (jnp.minimum((i + 1) * hb, last_halo), U_XBC)),
                  pl.BlockSpec((tc, V7X_LANES), lambda i: (i, 0)),
                  pl.BlockSpec((V7X_SUBLANES, conv_dim), lambda i: (0, 0)),
                  pl.BlockSpec((1, conv_dim), lambda i: (0, 0)),
                  pl.BlockSpec((1, V7X_LANES), lambda i: (0, 0)),
                  pl.BlockSpec((1, V7X_LANES), lambda i: (0, 0))],
        out_specs=(pl.BlockSpec((d_inner, tc), lambda i: (0, i)),
                   pl.BlockSpec((tc, 2 * n_bc), lambda i: (i, 0)),
                   pl.BlockSpec((n_bc, tc), lambda i: (0, i)),
                   pl.BlockSpec((V7X_LANES, tc), lambda i: (0, i)),
                   pl.BlockSpec((tc, V7X_LANES), lambda i: (i, 0))),
        scratch_shapes=[pltpu.VMEM((tc + 2 * CONV_HALO, conv_dim), F32)],
        compiler_params=_params(("parallel",), vmem),
        name="conv",
    )(u, u, u, dt_raw, cw, conv_b.astype(F32).reshape(1, -1), pad32(dt_bias), pad32(a_log))


def _ssd_direction(d, rev, xt_ref, bc_ref, ct_ref, aux_ref, acum_ref, st_scr, y_ref, dsk_ref):
    q = SSM_CHUNK
    hd = SSM_HEAD_DIM
    nh = SSM_HEADS
    si = lax.broadcasted_iota(I32, (q, q), 0)
    li = lax.broadcasted_iota(I32, (q, q), 1)
    mask = (li <= si) if rev else (li >= si)
    hpg = nh // SSM_GROUPS
    for g in range(SSM_GROUPS):
        b_g = bc_ref[:, g * SSM_STATE:(g + 1) * SSM_STATE]
        c_g = bc_ref[:, (SSM_GROUPS + g) * SSM_STATE:(SSM_GROUPS + g + 1) * SSM_STATE]
        ct_g = ct_ref[g * SSM_STATE:(g + 1) * SSM_STATE, :].astype(F32)
        cbt = lax.dot_general(b_g, c_g, (((1,), (1,)), ((), ())), preferred_element_type=F32)
        xdw_g, keep_g = [], []
        for hh in range(hpg):
            h = g * hpg + hh
            r = d * nh + h
            dt_row = aux_ref[r:r + 1, :]
            cum_row = aux_ref[2 * nh + r:2 * nh + r + 1, :]
            tot_row = aux_ref[4 * nh + r:4 * nh + r + 1, :]
            cum_col = acum_ref[:, r:r + 1]
            decay = jnp.exp(jnp.where(mask, cum_row - cum_col, -jnp.inf))
            w_top = (cbt * decay).astype(BF16)
            w_bot = (ct_g * jnp.exp(cum_row)).astype(BF16)
            w = jnp.concatenate([w_top, w_bot], axis=0)
            rows = slice(h * hd, (h + 1) * hd)
            x_t = xt_ref[rows, :].astype(F32)
            xd = x_t * dt_row
            s_old = st_scr[d, rows, :]
            lhs = jnp.concatenate([xd.astype(BF16), s_old.astype(BF16)], axis=1)
            y_t = jnp.dot(lhs, w, preferred_element_type=F32)
            if not rev:
                y_t = y_t + dsk_ref[h] * x_t
            y_ref[rows, :] = y_t.astype(y_ref.dtype)
            xdw_g.append((xd * jnp.exp(tot_row - cum_row)).astype(BF16))
            keep_g.append(s_old * jnp.exp(tot_row))
        grows = slice(g * hpg * hd, (g + 1) * hpg * hd)
        st_scr[d, grows, :] = jnp.concatenate(keep_g, axis=0) + jnp.dot(
            jnp.concatenate(xdw_g, axis=0), b_g, preferred_element_type=F32)


def _ssd_kernel(dsk_ref, xt_f, bc_f, ct_f, aux_f, ac_f, xt_b, bc_b, ct_b, aux_b, ac_b, init_ref,
                yf_ref, yb_ref, fin_ref, st_scr):
    c = pl.program_id(1)

    @pl.when(c == 0)
    def _():
        st_scr[...] = init_ref[...]

    _ssd_direction(0, False, xt_f, bc_f, ct_f, aux_f, ac_f, st_scr, yf_ref, dsk_ref)
    _ssd_direction(1, True, xt_b, bc_b, ct_b, aux_b, ac_b, st_scr, yb_ref, dsk_ref)

    @pl.when(c == pl.num_programs(1) - 1)
    def _():
        fin_ref[...] = st_scr[...]


def _ssd(conv_out, d_skip, init_state, *, batch, seq_len):
    xt, bc, ct, aux, acum = conv_out
    d_inner, t = xt.shape
    n_bc = ct.shape[0]
    q = SSM_CHUNK
    nc = seq_len // q
    fwd = lambda b, c: b * nc + c
    bwd = lambda b, c: b * nc + nc - 1 - c

    def specs(pos):
        return [pl.BlockSpec((d_inner, q), lambda b, c: (0, pos(b, c))),
                pl.BlockSpec((q, 2 * n_bc), lambda b, c: (pos(b, c), 0)),
                pl.BlockSpec((n_bc, q), lambda b, c: (0, pos(b, c))),
                pl.BlockSpec((V7X_LANES, q), lambda b, c: (0, pos(b, c))),
                pl.BlockSpec((q, V7X_LANES), lambda b, c: (pos(b, c), 0))]

    st_spec = pl.BlockSpec((None, 2, d_inner, SSM_STATE), lambda b, c: (b, 0, 0, 0))
    st_bytes = _nbytes((2, d_inner, SSM_STATE), F32)
    vmem = 5 * st_bytes + 4 * (_nbytes((d_inner, q), BF16) * 2 + _nbytes((q, 2 * n_bc), BF16)
                               + _nbytes((n_bc, q), BF16) + 2 * _nbytes((q, V7X_LANES), F32))
    return pl.pallas_call(
        _ssd_kernel,
        out_shape=(jax.ShapeDtypeStruct((d_inner, t), BF16),
                   jax.ShapeDtypeStruct((d_inner, t), BF16),
                   jax.ShapeDtypeStruct((batch, 2, d_inner, SSM_STATE), F32)),
        grid=(batch, nc),
        in_specs=[pl.BlockSpec(memory_space=pltpu.SMEM)] + specs(fwd) + specs(bwd) + [st_spec],
        out_specs=(pl.BlockSpec((d_inner, q), lambda b, c: (0, fwd(b, c))),
                   pl.BlockSpec((d_inner, q), lambda b, c: (0, bwd(b, c))),
                   st_spec),
        scratch_shapes=[pltpu.VMEM((2, d_inner, SSM_STATE), F32)],
        compiler_params=_params(("arbitrary", "arbitrary"), vmem),
        name="ssd",
    )(d_skip.astype(F32), xt, bc, ct, aux, acum, xt, bc, ct, aux, acum, init_state)


def _pool_body(slabs, cur_ref, mask_refs, icnt_ref, pw_ref, ps_ref, o_ref):
    gw = cur_ref.shape[1] // MIX_GROUPS
    for g in range(MIX_GROUPS):
        cols = slice(g * gw, (g + 1) * gw)
        xin = slabs(g, cols)
        s = jnp.dot(mask_refs[g][...], xin, preferred_element_type=F32)
        inv = icnt_ref[:, g * V7X_LANES:(g + 1) * V7X_LANES]
        p = s * jnp.concatenate([inv] * (gw // V7X_LANES), axis=1)
        dlt = p - cur_ref[:, cols].astype(F32)
        y = jnp.dot(dlt.astype(BF16), pw_ref[g], preferred_element_type=F32) * ps_ref[:, cols]
        o_ref[:, cols] = y.astype(BF16)


def _pool2d_kernel(prev_ref, cur_ref, next_ref, m0, m1, m2, m3, icnt_ref, pw_ref, ps_ref, o_ref):
    tile = cur_ref.shape[0]

    def slabs(g, cols):
        w = POOL_WINDOWS[g]
        up, dn = w // 2, w - w // 2 - 1
        parts = [prev_ref[tile - GRID_W * up:, cols], cur_ref[:, cols]]
        if dn:
            parts.append(next_ref[:GRID_W * dn, cols])
        return jnp.concatenate(parts, axis=0)

    _pool_body(slabs, cur_ref, (m0, m1, m2, m3), icnt_ref, pw_ref, ps_ref, o_ref)


def _pool1d_kernel(cur_ref, m0, m1, m2, m3, icnt_ref, pw_ref, ps_ref, o_ref):
    _pool_body(lambda g, cols: cur_ref[:, cols], cur_ref, (m0, m1, m2, m3), icnt_ref, pw_ref, ps_ref, o_ref)


def _window_ok(out_pos, in_pos, w):
    lo = out_pos[:, None] - w // 2
    return (in_pos[None, :] >= lo) & (in_pos[None, :] < lo + w)


def _pool_tables_2d(tile):
    rows = tile // GRID_W
    t = np.arange(tile)
    masks, invs = [], []
    for w in POOL_WINDOWS:
        up, dn = w // 2, w - w // 2 - 1
        u = np.arange(tile + GRID_W * (up + dn))
        r_in, c_in = u // GRID_W - up, u % GRID_W
        base = _window_ok(t // GRID_W, r_in, w) & _window_ok(t % GRID_W, c_in, w)
        var = np.stack([base & (r_in >= 0)[None, :], base, base & (r_in < rows)[None, :]])
        masks.append(jnp.asarray(var, dtype=BF16))
        invs.append(1.0 / np.sum(var, axis=2, dtype=np.float64))
    inv = np.stack(invs, axis=1)
    inv = np.broadcast_to(inv[:, :, :, None], inv.shape + (V7X_LANES,))
    inv = np.transpose(inv, (0, 2, 1, 3)).reshape(3, tile, MIX_GROUPS * V7X_LANES)
    return masks, jnp.asarray(inv, dtype=F32)


def _pool_tables_1d(n):
    t = np.arange(n)
    masks, invs = [], []
    for w in POOL_WINDOWS:
        m = _window_ok(t, t, w)
        masks.append(jnp.asarray(m, dtype=BF16))
        invs.append(1.0 / np.sum(m, axis=1, dtype=np.float64))
    inv = np.stack(invs, axis=0)
    inv = np.broadcast_to(inv[:, :, None], inv.shape + (V7X_LANES,))
    return masks, jnp.asarray(np.transpose(inv, (1, 0, 2)).reshape(n, MIX_GROUPS * V7X_LANES), dtype=F32)


def _pool2d(u, pool_w, pool_scale, *, seq_len):
    t = u.shape[0]
    d = pool_scale.shape[0]
    tile = TM_ROW
    nt, per_img = t // tile, seq_len // tile
    masks, inv = _pool_tables_2d(tile)

    def variant(i):
        p = i % per_img
        return jnp.where(p == 0, 0, jnp.where(p == per_img - 1, 2, 1))

    gw = d // MIX_GROUPS
    vmem = 2 * (4 * _nbytes((tile, d), BF16) + sum(_nbytes(m.shape[1:], BF16) for m in masks)
                + _nbytes(inv.shape[1:], F32) + _nbytes((MIX_GROUPS, gw, gw), BF16))
    return pl.pallas_call(
        _pool2d_kernel,
        out_shape=jax.ShapeDtypeStruct((t, d), BF16),
        grid=(nt,),
        in_specs=[pl.BlockSpec((tile, d), lambda i: (jnp.maximum(i - 1, 0), U_POOL)),
                  pl.BlockSpec((tile, d), lambda i: (i, U_POOL)),
                  pl.BlockSpec((tile, d), lambda i: (jnp.minimum(i + 1, nt - 1), U_POOL))]
                 + [pl.BlockSpec((None,) + m.shape[1:], lambda i: (variant(i), 0, 0)) for m in masks]
                 + [pl.BlockSpec((None,) + inv.shape[1:], lambda i: (variant(i), 0, 0)),
                    pl.BlockSpec((MIX_GROUPS, gw, gw), lambda i: (0, 0, 0)),
                    pl.BlockSpec((1, d), lambda i: (0, 0))],
        out_specs=pl.BlockSpec((tile, d), lambda i: (i, 0)),
        compiler_params=_params(("parallel",), vmem),
        name="pool2d",
    )(u, u, u, *masks, inv, pool_w.astype(BF16), pool_scale.astype(F32).reshape(1, d))


def _pool1d(u, pool_w, pool_scale, *, seq_len):
    t = u.shape[0]
    d = pool_scale.shape[0]
    masks, inv = _pool_tables_1d(seq_len)
    gw = d // MIX_GROUPS
    vmem = 2 * (2 * _nbytes((seq_len, d), BF16) + 4 * _nbytes((seq_len, seq_len), BF16)
                + _nbytes(inv.shape, F32) + _nbytes((MIX_GROUPS, gw, gw), BF16))
    return pl.pallas_call(
        _pool1d_kernel,
        out_shape=jax.ShapeDtypeStruct((t, d), BF16),
        grid=(t // seq_len,),
        in_specs=[pl.BlockSpec((seq_len, d), lambda i: (i, U_POOL))]
                 + [pl.BlockSpec(m.shape, lambda i: (0, 0)) for m in masks]
                 + [pl.BlockSpec(inv.shape, lambda i: (0, 0)),
                    pl.BlockSpec((MIX_GROUPS, gw, gw), lambda i: (0, 0, 0)),
                    pl.BlockSpec((1, d), lambda i: (0, 0))],
        out_specs=pl.BlockSpec((seq_len, d), lambda i: (i, 0)),
        compiler_params=_params(("parallel",), vmem),
        name="pool1d",
    )(u, *masks, inv, pool_w.astype(BF16), pool_scale.astype(F32).reshape(1, d))


def _dft_tables(n_pos, gw):
    n1, n2 = FOURIER_N1, n_pos // FOURIER_N1
    c = np.arange(gw)
    ang = 2.0 * np.pi * ((c[:, None] * c[None, :]) % gw) / gw
    cs = np.concatenate([np.cos(ang), -np.sin(ang)], axis=1)
    a = np.arange(n1)
    ang1 = 2.0 * np.pi * ((a[:, None] * a[None, :]) % n1) / n1
    wc, ws = np.cos(ang1), np.sin(ang1)
    w1 = np.block([[wc, ws], [-ws, wc]])
    k = a[:, None, None] + n1 * np.arange(n2)[None, :, None]
    ang2 = 2.0 * np.pi * ((k * np.arange(n2)[None, None, :]) % n_pos) / n_pos
    m2 = np.concatenate([np.cos(ang2), np.sin(ang2)], axis=2)
    to_bf = lambda v: jnp.asarray(v, dtype=F32).astype(BF16)
    return to_bf(cs), to_bf(w1), to_bf(m2)


def _f1_kernel(u_ref, cs_ref, w1_ref, o_ref, z_scr):
    nb, n1, d = u_ref.shape
    gw = d // MIX_GROUPS
    u = u_ref[...].reshape(nb * n1, d)
    for g in range(MIX_GROUPS):
        z = jnp.dot(u[:, g * gw:(g + 1) * gw], cs_ref[...], preferred_element_type=F32)
        z_scr[:, g * gw:(g + 1) * gw] = z[:, :gw].astype(BF16)
        z_scr[:, d + g * gw:d + (g + 1) * gw] = z[:, gw:].astype(BF16)
    for j in range(nb):
        zz = jnp.concatenate([z_scr[j * n1:(j + 1) * n1, :d], z_scr[j * n1:(j + 1) * n1, d:]], axis=0)
        o_ref[j] = jnp.dot(w1_ref[...], zz, preferred_element_type=F32).astype(BF16)


def _f2_kernel(a_ref, m2_ref, o_ref, *, scale):
    for j in range(a_ref.shape[0]):
        o_ref[j] = (jnp.dot(m2_ref[j], a_ref[j], preferred_element_type=F32) * scale).astype(BF16)


def _fourier_lat(u, *, batch, seq_len, d):
    n1, n2 = FOURIER_N1, seq_len // FOURIER_N1
    gw = d // MIX_GROUPS
    cs, w1, m2 = _dft_tables(seq_len, gw)
    ut = u[:, U_FOUR * d:(U_FOUR + 1) * d].reshape(batch, n1, n2, d).transpose(0, 2, 1, 3)
    nb = F1_BLOCK
    vmem1 = 2 * (_nbytes((nb, n1, d), BF16) + _nbytes((nb, 2 * n1, d), BF16)) + _nbytes((nb * n1, 2 * d), BF16) \
        + 2 * _nbytes((nb * n1, 2 * gw), F32)
    a = pl.pallas_call(
        _f1_kernel,
        out_shape=jax.ShapeDtypeStruct((batch, n2, 2 * n1, d), BF16),
        grid=(batch, n2 // nb),
        in_specs=[pl.BlockSpec((None, nb, n1, d), lambda b, j: (b, j, 0, 0)),
                  pl.BlockSpec(cs.shape, lambda b, j: (0, 0)),
                  pl.BlockSpec(w1.shape, lambda b, j: (0, 0))],
        out_specs=pl.BlockSpec((None, nb, 2 * n1, d), lambda b, j: (b, j, 0, 0)),
        scratch_shapes=[pltpu.VMEM((nb * n1, 2 * d), BF16)],
        compiler_params=_params(("parallel", "parallel"), vmem1),
        name="fourier_stage1",
    )(ut, cs, w1)
    at = a.reshape(batch, n2, 2, n1, d).transpose(0, 3, 2, 1, 4).reshape(batch, n1, 2 * n2, d)
    kb = F2_BLOCK
    vmem2 = 2 * (_nbytes((kb, 2 * n2, d), BF16) + _nbytes((kb, n2, 2 * n2), BF16) + _nbytes((kb, n2, d), BF16))
    x2 = pl.pallas_call(
        functools.partial(_f2_kernel, scale=1.0 / math.sqrt(seq_len * gw)),
        out_shape=jax.ShapeDtypeStruct((batch, n1, n2, d), BF16),
        grid=(batch, n1 // kb),
        in_specs=[pl.BlockSpec((None, kb, 2 * n2, d), lambda b, j: (b, j, 0, 0)),
                  pl.BlockSpec((kb, n2, 2 * n2), lambda b, j: (j, 0, 0))],
        out_specs=pl.BlockSpec((None, kb, n2, d), lambda b, j: (b, j, 0, 0)),
        compiler_params=_params(("parallel", "parallel"), vmem2),
        name="fourier_stage2",
    )(at, m2)
    return x2.transpose(0, 2, 1, 3).reshape(batch * seq_len, d)


def _fourier_ctx_kernel(u_ref, cs_ref, cn_ref, o_ref, *, scale):
    d = u_ref.shape[1]
    gw = d // MIX_GROUPS
    zr, zi = [], []
    for g in range(MIX_GROUPS):
        z = jnp.dot(u_ref[:, g * gw:(g + 1) * gw], cs_ref[...], preferred_element_type=F32)
        zr.append(z[:, :gw].astype(BF16))
        zi.append(z[:, gw:].astype(BF16))
    zz = jnp.concatenate([jnp.concatenate(zr, axis=1), jnp.concatenate(zi, axis=1)], axis=0)
    o_ref[...] = (jnp.dot(cn_ref[...], zz, preferred_element_type=F32) * scale).astype(BF16)


def _fourier_ctx(u, *, seq_len, d):
    t = u.shape[0]
    gw = d // MIX_GROUPS
    cs, _, _ = _dft_tables(FOURIER_N1 * 2, gw)
    n = np.arange(seq_len)
    ang = 2.0 * np.pi * ((n[:, None] * n[None, :]) % seq_len) / seq_len
    cn = jnp.asarray(np.concatenate([np.cos(ang), np.sin(ang)], axis=1), dtype=F32).astype(BF16)
    vmem = 2 * (2 * _nbytes((seq_len, d), BF16) + _nbytes(cs.shape, BF16) + _nbytes(cn.shape, BF16)) \
        + 4 * _nbytes((seq_len, d), F32)
    return pl.pallas_call(
        functools.partial(_fourier_ctx_kernel, scale=1.0 / math.sqrt(seq_len * gw)),
        out_shape=jax.ShapeDtypeStruct((t, d), BF16),
        grid=(t // seq_len,),
        in_specs=[pl.BlockSpec((seq_len, d), lambda i: (i, U_FOUR)),
                  pl.BlockSpec(cs.shape, lambda i: (0, 0)),
                  pl.BlockSpec(cn.shape, lambda i: (0, 0))],
        out_specs=pl.BlockSpec((seq_len, d), lambda i: (i, 0)),
        compiler_params=_params(("parallel",), vmem),
        name="fourier_ctx",
    )(u, cs, cn)


def _merge_kernel(ya_ref, yb_ref, yf_ref, ybw_ref, z_ref, g0_ref, g1_ref, g2_ref, lat_ref, mod_ref,
                  snw_ref, wb_ref, wo_ref, n2w_ref, rw_ref, rb_ref, tri_ref, ones_ref, cnt0_ref,
                  lat_o, h2_o, idx_o, rank_o, wts_o, cnt_o, run_scr):
    tm = lat_ref.shape[0]

    @pl.when(pl.program_id(0) == 0)
    def _():
        run_scr[...] = cnt0_ref[...]

    y_ssm = (yf_ref[...].astype(F32) + ybw_ref[...].astype(F32)).T
    yc = _rms(y_ssm * _silu(z_ref[...].astype(F32)), snw_ref[...])
    merged = jnp.zeros(lat_ref.shape, F32)
    for k, (y, g_ref) in enumerate(((ya_ref[...], g0_ref), (yb_ref[...], g1_ref), (yc.astype(BF16), g2_ref))):
        proj = jnp.dot(y, wb_ref[k], preferred_element_type=F32)
        merged = merged + jax.nn.sigmoid(g_ref[...].astype(F32)) * proj
    ol = jnp.dot(merged.astype(BF16), wo_ref[...], preferred_element_type=F32)
    lat = lat_ref[...] + mod_ref[2:3, :] * ol
    lat_o[...] = lat
    h2 = _rms(lat, n2w_ref[...]) * (1.0 + mod_ref[4:5, :]) + mod_ref[3:4, :]
    _store_tile_rows(h2_o, h2)

    h_hi = h2.astype(BF16)
    h_lo = (h2 - h_hi.astype(F32)).astype(BF16)
    nt_dims = (((1,), (1,)), ((), ()))
    lt = (lax.dot_general(rw_ref[0], h_hi, nt_dims, preferred_element_type=F32)
          + lax.dot_general(rw_ref[0], h_lo, nt_dims, preferred_element_type=F32)
          + lax.dot_general(rw_ref[1], h_hi, nt_dims, preferred_element_type=F32))
    lt = lt + jnp.concatenate([rb_ref[...]] * (tm // V7X_LANES), axis=1)
    ei = lax.broadcasted_iota(I32, lt.shape, 0).astype(F32)
    vals, idxs = [], []
    for _ in range(TOP_K):
        m = jnp.max(lt, axis=0, keepdims=True)
        sel = jnp.min(jnp.where(lt == m, ei, float(V7X_LANES)), axis=0, keepdims=True)
        vals.append(m)
        idxs.append(sel)
        lt = jnp.where(ei == sel, -jnp.inf, lt)
    exps = [jnp.exp(v - vals[0]) for v in vals]
    den = exps[0] + exps[1] + exps[2] + exps[3]
    onehots = [ei == s for s in idxs]
    chosen = jnp.zeros(lt.shape, F32)
    for oh in onehots:
        chosen = chosen + jnp.where(oh, 1.0, 0.0)
    chosen_b = chosen.astype(BF16)
    before = jnp.dot(chosen_b, tri_ref[...], preferred_element_type=F32) \
        + jnp.concatenate([run_scr[...]] * (tm // V7X_LANES), axis=1)
    for k in range(TOP_K):
        idx_o[k:k + 1, :] = idxs[k].astype(I32)
        rank_o[k:k + 1, :] = jnp.sum(jnp.where(onehots[k], before, 0.0), axis=0, keepdims=True).astype(I32)
        wts_o[k:k + 1, :] = exps[k] / den
    pad = V7X_SUBLANES - TOP_K
    idx_o[TOP_K:, :] = jnp.zeros((pad, tm), I32)
    rank_o[TOP_K:, :] = jnp.zeros((pad, tm), I32)
    wts_o[TOP_K:, :] = jnp.zeros((pad, tm), F32)
    run_scr[...] = run_scr[...] + jnp.dot(chosen_b, ones_ref[...], preferred_element_type=F32)
    cnt_o[...] = run_scr[...]


def _merge(ya, yb, yf_t, yb_t, u, lat, modv, ssm_norm_w, wb, wo, norm2_w, rw, rb, cnt0, *,
           tm, seg0, tiles_per_seg):
    t, d = lat.shape
    ne = V7X_LANES
    tri = jnp.asarray(np.arange(tm)[:, None] < np.arange(tm)[None, :], dtype=BF16)
    ones = jnp.ones((tm, ne), BF16)
    row = lambda c: pl.BlockSpec((tm, d), lambda i: (i, c))
    col = pl.BlockSpec((d, tm), lambda i: (0, i))
    const = lambda shape: pl.BlockSpec(shape, lambda i: (0,) * len(shape), pipeline_mode=pl.Buffered(1))
    tok = pl.BlockSpec((V7X_SUBLANES, tm), lambda i: (0, i))
    vmem = (2 * (4 * _nbytes((tm, d), BF16) + 2 * _nbytes((d, tm), BF16) + 2 * _nbytes((tm, d), BF16)
                 + 3 * _nbytes((tm, d), F32))
            + 4 * _nbytes((d, d), BF16) + _nbytes((2, ne, d), BF16) + _nbytes((tm, tm), BF16)
            + 8 * _nbytes((tm, d), F32))
    return pl.pallas_call(
        _merge_kernel,
        out_shape=(jax.ShapeDtypeStruct((t, d), F32), jax.ShapeDtypeStruct((t * V7X_SUBLANES, V7X_LANES), F32),
                   jax.ShapeDtypeStruct((V7X_SUBLANES, t), I32), jax.ShapeDtypeStruct((V7X_SUBLANES, t), I32),
                   jax.ShapeDtypeStruct((V7X_SUBLANES, t), F32), jax.ShapeDtypeStruct((ne, ne), F32)),
        grid=(t // tm,),
        in_specs=[row(0), row(0), col, col, row(U_Z), row(U_GATE), row(U_GATE + 1), row(U_GATE + 2),
                  row(0),
                  pl.BlockSpec((None, V7X_SUBLANES, d), lambda i: (seg0 + i // tiles_per_seg, 0, 0)),
                  const((1, d)), const((N_BRANCH, d, d)), const((d, d)), const((1, d)),
                  const((2, ne, d)), const((ne, ne)), const((tm, tm)), const((tm, ne)), const((ne, ne))],
        out_specs=(row(0), pl.BlockSpec((tm * V7X_SUBLANES, V7X_LANES), lambda i: (i, 0)), tok, tok, tok,
                   pl.BlockSpec((ne, ne), lambda i: (0, 0))),
        scratch_shapes=[pltpu.VMEM((ne, ne), F32)],
        compiler_params=_params(("arbitrary",), vmem),
        name="merge",
    )(ya, yb, yf_t, yb_t, u, u, u, u, lat, modv, ssm_norm_w.astype(F32).reshape(1, d), wb, wo,
      norm2_w.astype(F32).reshape(1, d), rw, rb, tri, ones, cnt0)


def _tile_row(r):
    return pl.ds(pl.multiple_of(r * V7X_SUBLANES, V7X_SUBLANES), V7X_SUBLANES)


def _dispatch_kernel(dest_ref, h_ref, *rest):
    xs_ref, sem = rest[-2], rest[-1]
    rows = h_ref.shape[0]
    n = rows // V7X_SUBLANES

    def start(t, carry):
        for k in range(TOP_K):
            pltpu.make_async_copy(h_ref.at[_tile_row(t), :], xs_ref.at[_tile_row(dest_ref[0, TOP_K * t + k]), :],
                                  sem).start(priority=k % DMA_PRIORITIES)
        return carry

    lax.fori_loop(0, n, start, 0, unroll=DMA_UNROLL)
    for k in range(TOP_K):
        pltpu.make_async_copy(h_ref, xs_ref.at[pl.ds(0, rows), :], sem).wait()


def _dispatch(h2, dest_tiles, n_slots, xs_prev=None):
    t, d = h2.shape[0] // V7X_SUBLANES, V7X_LANES
    td = T_DMA
    in_specs = [pl.BlockSpec((None, 1, TOP_K * td), lambda i: (i, 0, 0), memory_space=pltpu.SMEM),
                pl.BlockSpec((td * V7X_SUBLANES, d), lambda i: (i, 0))]
    args = [dest_tiles, h2]
    aliases = {}
    if xs_prev is not None:
        in_specs.append(pl.BlockSpec(memory_space=pl.ANY))
        args.append(xs_prev)
        aliases = {2: 0}
    return pl.pallas_call(
        _dispatch_kernel,
        out_shape=jax.ShapeDtypeStruct((n_slots * V7X_SUBLANES, d), F32),
        grid=(t // td,),
        in_specs=in_specs,
        out_specs=pl.BlockSpec(memory_space=pl.ANY),
        scratch_shapes=[pltpu.SemaphoreType.DMA(())],
        input_output_aliases=aliases,
        compiler_params=_params(("arbitrary",), 2 * _nbytes((td * V7X_SUBLANES, d), F32)),
        name="dispatch",
    )(*args)


EXPERT_NB = 256
EXPERT_ROW_STEP = 256


def _expert_kernel(be_ref, nv_ref, nxt_ref, xs_ref, w1_hbm, b1_ref, w2_hbm, b2_ref, ys_ref,
                   w1_stage, w2_stage, w1_scr, w2_scr, act_scr, sems, *, layer):
    i = pl.program_id(0)
    d, dff = w1_stage.shape[0], w2_stage.shape[0]
    e = be_ref[i]
    changed = (i == 0) | (e != be_ref[jnp.maximum(i - 1, 0)])

    def weight_copies(ex):
        return (pltpu.make_async_copy(w1_hbm.at[layer, ex], w1_stage, sems.at[0]),
                pltpu.make_async_copy(w2_hbm.at[layer, ex], w2_stage, sems.at[1]))

    @pl.when(i == 0)
    def _():
        for cp in weight_copies(e):
            cp.start()

    @pl.when(changed)
    def _():
        for cp in weight_copies(e):
            cp.wait()
        rows = 128

        def cast1(r, carry):
            sl = pl.ds(pl.multiple_of(r * rows, rows), rows)
            w1_scr[sl, :] = w1_stage[sl, :].astype(BF16)
            return carry

        def cast2(r, carry):
            sl = pl.ds(pl.multiple_of(r * rows, rows), rows)
            w2_scr[sl, :] = w2_stage[sl, :].astype(BF16)
            return carry

        lax.fori_loop(0, d // rows, cast1, 0)
        lax.fori_loop(0, dff // rows, cast2, 0)
        nxt = nxt_ref[i]

        @pl.when(nxt >= 0)
        def _():
            for cp in weight_copies(nxt):
                cp.start()

    nv = nv_ref[i]

    def mlp(rows):
        xs_v = xs_ref.at[pl.ds(0, rows * V7X_SUBLANES), :]
        ys_v = ys_ref.at[pl.ds(0, rows * V7X_SUBLANES), :]
        x = jnp.concatenate(_load_tile_rows(xs_v, rows), axis=1)
        ridx = lax.broadcasted_iota(I32, x.shape, 0)
        x = jnp.where(ridx < nv, x, 0.0).astype(BF16)
        for c in range(dff // EXPERT_NB):
            cg = slice(c * EXPERT_NB, (c + 1) * EXPERT_NB)
            cl = slice(dff + c * EXPERT_NB, dff + (c + 1) * EXPERT_NB)
            gate = jnp.dot(x, w1_scr[:, cg], preferred_element_type=F32) + b1_ref[:, cg]
            lin = jnp.dot(x, w1_scr[:, cl], preferred_element_type=F32) + b1_ref[:, cl]
            gate = jnp.minimum(gate, SWIGLU_LIMIT)
            lin = jnp.clip(lin, -SWIGLU_LIMIT, SWIGLU_LIMIT)
            act = gate * jax.nn.sigmoid(SWIGLU_ALPHA * gate) * (lin + 1.0)
            act_scr[0:rows, cg] = act.astype(BF16)
        _store_tile_rows(ys_v, jnp.dot(act_scr[0:rows, :], w2_scr[...], preferred_element_type=F32) + b2_ref[...])

    def zero_from(row0):
        tail = ys_ref.at[pl.ds(row0 * V7X_SUBLANES, (MOE_BLOCK - row0) * V7X_SUBLANES), :]
        tail[...] = jnp.zeros(tail.shape, F32)

    @pl.when(nv == 0)
    def _():
        zero_from(0)

    for rows in range(EXPERT_ROW_STEP, MOE_BLOCK + 1, EXPERT_ROW_STEP):
        @pl.when((nv > rows - EXPERT_ROW_STEP) & (nv <= rows))
        def _(rows=rows):
            mlp(rows)
            if rows < MOE_BLOCK:
                zero_from(rows)


def _experts(xs, block_e, nvalid, next_e, w1, b1, w2, b2, *, layer):
    n_slots = xs.shape[0] // V7X_SUBLANES
    depth, ne, d, two_dff = w1.shape
    dff = two_dff // 2
    n_blocks = n_slots // MOE_BLOCK
    blk = pl.BlockSpec((MOE_BLOCK * V7X_SUBLANES, V7X_LANES), lambda i, be, nv, nx: (i, 0))
    grid_spec = pltpu.PrefetchScalarGridSpec(
        num_scalar_prefetch=3,
        grid=(n_blocks,),
        in_specs=[blk,
                  pl.BlockSpec(memory_space=pl.ANY),
                  pl.BlockSpec((None, None, 1, two_dff), lambda i, be, nv, nx: (layer, be[i], 0, 0)),
                  pl.BlockSpec(memory_space=pl.ANY),
                  pl.BlockSpec((None, None, 1, d), lambda i, be, nv, nx: (layer, be[i], 0, 0))],
        out_specs=blk,
        scratch_shapes=[pltpu.VMEM((d, two_dff), F32), pltpu.VMEM((dff, d), F32),
                        pltpu.VMEM((d, two_dff), BF16), pltpu.VMEM((dff, d), BF16),
                        pltpu.VMEM((MOE_BLOCK, dff), BF16), pltpu.SemaphoreType.DMA((2,))],
    )
    vmem = (_nbytes((d, two_dff), F32) + _nbytes((dff, d), F32) + 4 * _nbytes((MOE_BLOCK, d), F32)
            + _nbytes((d, two_dff), BF16) + _nbytes((dff, d), BF16) + 6 * _nbytes((MOE_BLOCK, dff), F32))
    return pl.pallas_call(
        functools.partial(_expert_kernel, layer=layer),
        out_shape=jax.ShapeDtypeStruct(xs.shape, F32),
        grid_spec=grid_spec,
        compiler_params=_params(("arbitrary",), vmem),
        name="experts",
    )(block_e, nvalid, next_e, xs, w1, b1.reshape(depth, ne, 1, two_dff), w2, b2.reshape(depth, ne, 1, d))


def _combine_kernel(dest_ref, dnext_ref, lat_ref, w_ref, mod_ref, fnw_ref, ys_ref, o_ref, buf, sems, *,
                    final_norm):
    i = pl.program_id(0)
    n = lat_ref.shape[0]
    slot = i % 2

    def issue(d_ref, s):
        def start(t, carry):
            for k in range(TOP_K):
                pltpu.make_async_copy(ys_ref.at[_tile_row(d_ref[0, TOP_K * t + k]), :],
                                      buf.at[s, k, _tile_row(t), :], sems.at[s]).start(priority=k % DMA_PRIORITIES)
            return carry

        lax.fori_loop(0, n, start, 0, unroll=DMA_UNROLL)

    @pl.when(i == 0)
    def _():
        issue(dest_ref, slot)

    @pl.when(i + 1 < pl.num_programs(0))
    def _():
        issue(dnext_ref, 1 - slot)

    for k in range(TOP_K):
        pltpu.make_async_copy(ys_ref.at[pl.ds(0, n * V7X_SUBLANES), :], buf.at[slot, k], sems.at[slot]).wait()
    parts = [jnp.zeros((n, V7X_LANES), F32)] * V7X_SUBLANES
    for k in range(TOP_K):
        wk = w_ref[:, k:k + 1]
        rows = _load_tile_rows(buf.at[slot, k], n)
        parts = [p + wk * r for p, r in zip(parts, rows)]
    out = lat_ref[...] + mod_ref[5:6, :] * jnp.concatenate(parts, axis=1)
    if final_norm:
        out = _rms(out, fnw_ref[...])
    o_ref[...] = out


def _combine(lat, dest_tiles, wts, modv, fnw, ys, *, seg0, tiles_per_seg, final_norm):
    t, d = lat.shape
    td = T_DMA
    nt = t // td
    vmem = 2 * (2 * _nbytes((td, d), F32) + _nbytes((td, V7X_LANES), F32)) + 2 * _nbytes((TOP_K, td, d), F32) \
        + 4 * _nbytes((td, d), F32)
    return pl.pallas_call(
        functools.partial(_combine_kernel, final_norm=final_norm),
        out_shape=jax.ShapeDtypeStruct((t, d), F32),
        grid=(nt,),
        in_specs=[pl.BlockSpec((None, 1, TOP_K * td), lambda i: (i, 0, 0), memory_space=pltpu.SMEM),
                  pl.BlockSpec((None, 1, TOP_K * td), lambda i: (jnp.minimum(i + 1, nt - 1), 0, 0),
                               memory_space=pltpu.SMEM),
                  pl.BlockSpec((td, d), lambda i: (i, 0)),
                  pl.BlockSpec((td, V7X_SUBLANES), lambda i: (i, 0)),
                  pl.BlockSpec((None, V7X_SUBLANES, d), lambda i: (seg0 + i // tiles_per_seg, 0, 0)),
                  pl.BlockSpec((1, d), lambda i: (0, 0)),
                  pl.BlockSpec(memory_space=pl.ANY)],
        out_specs=pl.BlockSpec((td, d), lambda i: (i, 0)),
        scratch_shapes=[pltpu.VMEM((2, TOP_K, td * V7X_SUBLANES, V7X_LANES), F32),
                        pltpu.SemaphoreType.DMA((2,))],
        compiler_params=_params(("arbitrary",), vmem),
        name="combine",
    )(dest_tiles, dest_tiles, lat, wts, modv, fnw.astype(F32).reshape(1, d), ys)


def _routing_tables(counts, n_blocks):
    padded = (counts + MOE_BLOCK - 1) // MOE_BLOCK * MOE_BLOCK
    pends = jnp.cumsum(padded)
    pstarts = pends - padded
    bstart = jnp.arange(n_blocks, dtype=I32) * MOE_BLOCK
    block_e = jnp.minimum(jnp.sum((pends[None, :] <= bstart[:, None]).astype(I32), axis=1), N_EXPERTS - 1)
    nvalid = jnp.clip(counts[block_e] - (bstart - pstarts[block_e]), 0, MOE_BLOCK)
    nvalid = jnp.where(bstart < pends[-1], nvalid, 0)
    blk = jnp.arange(n_blocks, dtype=I32)
    is_change = jnp.concatenate([jnp.zeros((1,), bool), block_e[1:] != block_e[:-1]])
    change_at = jnp.where(is_change, blk, n_blocks)
    next_change = lax.cummin(change_at, axis=0, reverse=True)
    after = jnp.concatenate([next_change[1:], jnp.full((1,), n_blocks, I32)])
    next_e = jnp.where(after < n_blocks, block_e[jnp.minimum(after, n_blocks - 1)], -1)
    return pstarts, block_e.astype(I32), nvalid.astype(I32), next_e.astype(I32)


def _dest_tiles(idx, rank, pstarts):
    e = idx[:TOP_K]
    start = jnp.sum(jnp.where(e[:, :, None] == jnp.arange(N_EXPERTS)[None, None, :], pstarts[None, None, :], 0),
                    axis=2)
    dest = (start + rank[:TOP_K]).astype(I32).T
    return dest.reshape(-1, 1, TOP_K * T_DMA)


def kernel(x, c, ctx, c_ctx, w_ada, b_ada, norm1_w, norm2_w, w_in, pool_w, pool_scale, conv_w, conv_b,
           dt_bias, a_log, d_skip, ssm_norm_w, w_branch, w_out, router_w, router_b, moe_w1, moe_b1,
           moe_w2, moe_b2, final_norm_w):
    batch, n, d = x.shape
    n_ctx = ctx.shape[1]
    depth = w_ada.shape[0]
    d_inner = SSM_HEADS * SSM_HEAD_DIM
    conv_dim = conv_w.shape[2]
    off_four, off_z, off_xbc = d, 2 * d, 3 * d
    off_dt = off_xbc + conv_dim
    off_gate = off_dt + 2 * SSM_HEADS
    assert batch + 1 <= V7X_SUBLANES and d_inner == d and conv_dim == 2 * d
    assert d == V7X_SUBLANES * V7X_LANES

    lat = x.reshape(batch * n, d).astype(F32)
    cx = ctx.reshape(batch * n_ctx, d).astype(F32)
    cvec = jnp.concatenate([c.astype(F32), c_ctx.astype(F32)[None, :],
                            jnp.zeros((V7X_SUBLANES - batch - 1, d), F32)], axis=0)
    mods = _ada(cvec, w_ada.astype(F32), b_ada.astype(F32))
    seg_ctx = batch
    zero_state = jnp.zeros((batch, 2, d_inner, SSM_STATE), F32)
    zero_cnt = jnp.zeros((V7X_LANES, V7X_LANES), F32)

    for i in range(depth):
        last = i == depth - 1
        modv = mods[i].reshape(V7X_SUBLANES, N_ADA, d)[:batch + 1]
        modv = jnp.pad(modv, ((0, 0), (0, V7X_SUBLANES - N_ADA), (0, 0)))
        wi = w_in[i]
        w_main = jnp.concatenate([wi[:, off_xbc:off_dt], wi[:, :off_z], wi[:, off_z:off_xbc], wi[:, off_gate:]],
                                 axis=1).astype(BF16)
        w_dt = jnp.pad(wi[:, off_dt:off_gate], ((0, 0), (0, V7X_LANES - 2 * SSM_HEADS))).astype(BF16)

        u_l, dtr_l = _inproj(lat, modv, norm1_w[i], w_main, w_dt, tm=TM_LAT, seg0=0, tiles_per_seg=n // TM_LAT)
        u_c, dtr_c = _inproj(cx, modv, norm1_w[i], w_main, w_dt, tm=batch * n_ctx, seg0=seg_ctx, tiles_per_seg=1)

        conv_args = (conv_w[i], conv_b[i], dt_bias[i], a_log[i])
        cv_c = _conv(u_c, dtr_c, *conv_args, tc=n_ctx, seq_len=n_ctx, d_inner=d_inner)
        cv_l = _conv(u_l, dtr_l, *conv_args, tc=TM_ROW, seq_len=n, d_inner=d_inner)
        yf_c, yb_c, st = _ssd(cv_c, d_skip[i], zero_state, batch=batch, seq_len=n_ctx)
        yf_l, yb_l, _ = _ssd(cv_l, d_skip[i], st, batch=batch, seq_len=n)

        ya_l = _pool2d(u_l, pool_w[i], pool_scale[i], seq_len=n)
        fb_l = _fourier_lat(u_l, batch=batch, seq_len=n, d=d)

        wb = w_branch[i].astype(BF16)
        wo = w_out[i].astype(BF16)
        rw_t = jnp.pad(router_w[i].astype(F32).T, ((0, V7X_LANES - N_EXPERTS), (0, 0)))
        rw_hi = rw_t.astype(BF16)
        rw = jnp.stack([rw_hi, (rw_t - rw_hi.astype(F32)).astype(BF16)])
        rb = jnp.pad(router_b[i].astype(F32), (0, V7X_LANES - N_EXPERTS), constant_values=-1e30)
        rb = jnp.broadcast_to(rb[:, None], (V7X_LANES, V7X_LANES))
        merge_w = (ssm_norm_w[i], wb, wo, norm2_w[i], rw, rb)
        lat_m, h2_l, idx_l, rank_l, wts_l, cnt = _merge(
            ya_l, fb_l, yf_l, yb_l, u_l, lat, modv, *merge_w, zero_cnt,
            tm=TM_ROW, seg0=0, tiles_per_seg=n // TM_ROW)
        n_tok = batch * n
        if not last:
            ya_c = _pool1d(u_c, pool_w[i], pool_scale[i], seq_len=n_ctx)
            fb_c = _fourier_ctx(u_c, seq_len=n_ctx, d=d)
            cx_m, h2_c, idx_c, rank_c, wts_c, cnt = _merge(
                ya_c, fb_c, yf_c, yb_c, u_c, cx, modv, *merge_w, cnt,
                tm=batch * n_ctx, seg0=seg_ctx, tiles_per_seg=1)
            n_tok += batch * n_ctx

        n_blocks = -(-(n_tok * TOP_K) // MOE_BLOCK) + N_EXPERTS
        n_slots = n_blocks * MOE_BLOCK
        counts = cnt[:N_EXPERTS, 0].astype(I32)
        pstarts, block_e, nvalid, next_e = _routing_tables(counts, n_blocks)
        dest_l = _dest_tiles(idx_l, rank_l, pstarts)
        xs = _dispatch(h2_l, dest_l, n_slots)
        if not last:
            dest_c = _dest_tiles(idx_c, rank_c, pstarts)
            xs = _dispatch(h2_c, dest_c, n_slots, xs_prev=xs)
        ys = _experts(xs, block_e, nvalid, next_e, moe_w1, moe_b1, moe_w2, moe_b2, layer=i)
        lat = _combine(lat_m, dest_l, wts_l.T, modv, final_norm_w, ys, seg0=0, tiles_per_seg=n // T_DMA,
                       final_norm=last)
        if not last:
            cx = _combine(cx_m, dest_c, wts_c.T, modv, final_norm_w, ys, seg0=seg_ctx,
                          tiles_per_seg=batch * n_ctx // T_DMA, final_norm=False)
    return lat.reshape(batch, n, d).astype(x.dtype)
```

```python
import functools
import math

import numpy as np
import jax
import jax.numpy as jnp
from jax import lax
from jax.experimental import pallas as pl
from jax.experimental.pallas import tpu as pltpu

F32 = jnp.float32
BF16 = jnp.bfloat16
I32 = jnp.int32
HIGHEST = lax.Precision.HIGHEST

GRID_W = 64
N_ADA = 6
N_BRANCH = 3
MIX_GROUPS = 4
POOL_WINDOWS = (2, 4, 8, 16)
SSM_HEADS = 16
SSM_HEAD_DIM = 64
SSM_GROUPS = 4
SSM_STATE = 128
SSM_CONV = 5
SSM_CHUNK = 128
N_EXPERTS = 32
TOP_K = 4
SWIGLU_ALPHA = 1.702
SWIGLU_LIMIT = 7.0
RMS_EPS = 1e-6
MOE_BLOCK = 1024

V7X_VMEM_BYTES = 64 * 2**20
V7X_LANES = 128
V7X_SUBLANES = 8
V7X_BF16_SUBLANES = 16
VMEM_COMPILER_ALLOWANCE = 6 * 2**20

TM_LAT = 2048
INPROJ_COLS = 1024
TM_ROW = 512
T_DMA = 512
T_DISPATCH = 2048
DMA_UNROLL = 4
DMA_PRIORITIES = 2
FOURIER_N1 = 64
F1_BLOCK = 16
F2_BLOCK = 8


def _params(semantics, vmem_bytes):
    limit = min(int(vmem_bytes) + VMEM_COMPILER_ALLOWANCE, V7X_VMEM_BYTES - 2 * 2**20)
    return pltpu.CompilerParams(dimension_semantics=semantics, vmem_limit_bytes=limit)


def _nbytes(shape, dtype):
    return int(np.prod(shape)) * jnp.dtype(dtype).itemsize


def _silu(v):
    return v * jax.nn.sigmoid(v)


def _rms(v, w):
    return v * lax.rsqrt(jnp.mean(v * v, axis=-1, keepdims=True) + RMS_EPS) * w


def _store_tile_rows(ref, v):
    n = v.shape[0]
    for j in range(V7X_SUBLANES):
        ref[pl.ds(j, n, stride=V7X_SUBLANES), :] = v[:, j * V7X_LANES:(j + 1) * V7X_LANES]


def _load_tile_rows(ref, n):
    return [ref[pl.ds(j, n, stride=V7X_SUBLANES), :] for j in range(V7X_SUBLANES)]


def _ada_kernel(c_ref, w_ref, b_ref, o_ref):
    o_ref[...] = jnp.dot(_silu(c_ref[...]), w_ref[...], preferred_element_type=F32,
                         precision=HIGHEST) + b_ref[...]


def _ada(cvec, w_ada, b_ada):
    depth, d, nd = w_ada.shape
    return pl.pallas_call(
        _ada_kernel,
        out_shape=jax.ShapeDtypeStruct((depth, V7X_SUBLANES, nd), F32),
        grid=(depth, nd // d),
        in_specs=[pl.BlockSpec((V7X_SUBLANES, d), lambda l, j: (0, 0)),
                  pl.BlockSpec((None, d, d), lambda l, j: (l, 0, j)),
                  pl.BlockSpec((None, 1, d), lambda l, j: (l, 0, j))],
        out_specs=pl.BlockSpec((None, V7X_SUBLANES, d), lambda l, j: (l, 0, j)),
        compiler_params=_params(("parallel", "parallel"), 2 * _nbytes((d, d), F32)),
        name="ada",
    )(cvec, w_ada, b_ada.reshape(depth, 1, nd))


U_XBC, U_POOL, U_FOUR, U_Z, U_GATE = 0, 2, 3, 4, 5
U_BLOCKS = 8


def _inproj_kernel(x_ref, mod_ref, nw_ref, w_ref, wdt_ref, u_ref, dt_ref, h_scr):
    @pl.when(pl.program_id(1) == 0)
    def _():
        h = _rms(x_ref[...], nw_ref[...]) * (1.0 + mod_ref[1:2, :]) + mod_ref[0:1, :]
        hb = h.astype(BF16)
        h_scr[...] = hb
        dt_ref[...] = jnp.dot(hb, wdt_ref[...], preferred_element_type=F32)

    u_ref[...] = jnp.dot(h_scr[...], w_ref[...], preferred_element_type=F32).astype(BF16)


def _inproj(x, modv, nw, w_main, w_dt, *, tm, seg0, tiles_per_seg):
    t, d = x.shape
    tn = INPROJ_COLS
    vmem = 2 * (_nbytes((tm, d), F32) + _nbytes((tm, tn), BF16) + _nbytes((d, tn), BF16)
                + _nbytes((tm, V7X_LANES), F32) + _nbytes((d, V7X_LANES), BF16)) + _nbytes((tm, d), BF16) \
        + _nbytes((tm, tn), F32)
    return pl.pallas_call(
        _inproj_kernel,
        out_shape=(jax.ShapeDtypeStruct((t, U_BLOCKS * d), BF16),
                   jax.ShapeDtypeStruct((t, V7X_LANES), F32)),
        grid=(t // tm, U_BLOCKS * d // tn),
        in_specs=[pl.BlockSpec((tm, d), lambda i, j: (i, 0)),
                  pl.BlockSpec((None, V7X_SUBLANES, d), lambda i, j: (seg0 + i // tiles_per_seg, 0, 0)),
                  pl.BlockSpec((1, d), lambda i, j: (0, 0)),
                  pl.BlockSpec((d, tn), lambda i, j: (0, j)),
                  pl.BlockSpec((d, V7X_LANES), lambda i, j: (0, 0))],
        out_specs=(pl.BlockSpec((tm, tn), lambda i, j: (i, j)),
                   pl.BlockSpec((tm, V7X_LANES), lambda i, j: (i, 0))),
        scratch_shapes=[pltpu.VMEM((tm, d), BF16)],
        compiler_params=_params(("parallel", "arbitrary"), vmem),
        name="inproj",
    )(x, modv, nw.reshape(1, d), w_main, w_dt)


CONV_HALO = V7X_BF16_SUBLANES
CONV_RB = 128
CONV_CB = 128


def _conv_kernel(cur_ref, prev_ref, next_ref, dtraw_ref, cw_ref, cb_ref, dtb_ref, alog_ref,
                 xt_ref, bc_ref, ct_ref, aux_ref, acum_ref, xp_scr, *, tc, tiles_per_seq, d_inner):
    p = pl.program_id(0) % tiles_per_seq
    xp_scr[0:CONV_HALO, :] = jnp.where(p > 0, prev_ref[...].astype(F32), 0.0)
    xp_scr[CONV_HALO:CONV_HALO + tc, :] = cur_ref[...].astype(F32)
    xp_scr[CONV_HALO + tc:, :] = jnp.where(p < tiles_per_seq - 1, next_ref[...].astype(F32), 0.0)
    n_bc = SSM_GROUPS * SSM_STATE
    for rb in range(tc // CONV_RB):
        rows = slice(rb * CONV_RB, (rb + 1) * CONV_RB)
        for cb in range(cur_ref.shape[1] // CONV_CB):
            c0 = cb * CONV_CB
            cols = slice(c0, c0 + CONV_CB)
            acc = jnp.broadcast_to(cb_ref[:, cols], (CONV_RB, CONV_CB))
            a0 = CONV_HALO - V7X_SUBLANES + rb * CONV_RB
            win = xp_scr[a0:a0 + CONV_RB + 2 * V7X_SUBLANES, cols]
            for k in range(SSM_CONV):
                s = SSM_CONV // 2 - k
                tap = win if s == 0 else pltpu.roll(win, s % win.shape[0], axis=0)
                acc = acc + cw_ref[k:k + 1, cols] * tap[V7X_SUBLANES:V7X_SUBLANES + CONV_RB, :]
            v = _silu(acc)
            if c0 < d_inner:
                xt_ref[cols, rows] = v.T.astype(BF16)
            elif c0 < d_inner + n_bc:
                bc_ref[rows, c0 - d_inner:c0 - d_inner + CONV_CB] = v.astype(BF16)
            else:
                bc_ref[rows, c0 - d_inner:c0 - d_inner + CONV_CB] = v.astype(BF16)
                cc = c0 - d_inner - n_bc
                ct_ref[cc:cc + CONV_CB, rows] = v.T.astype(BF16)

    dtr = dtraw_ref[...] + dtb_ref[...]
    dt = jnp.maximum(dtr, 0.0) + jnp.log1p(jnp.exp(-jnp.abs(dtr)))
    dta = dt * (-jnp.exp(alog_ref[...]))
    dt_t = dt.T
    dta_t = dta.T
    ki = lax.broadcasted_iota(I32, (tc, tc), 0)
    li = lax.broadcasted_iota(I32, (tc, tc), 1)
    same = (ki // SSM_CHUNK) == (li // SSM_CHUNK)
    one, zero = jnp.float32(1.0), jnp.float32(0.0)
    cumf = jnp.dot(dta_t, jnp.where(same & (ki <= li), one, zero), preferred_element_type=F32,
                   precision=HIGHEST)
    cumb = jnp.dot(dta_t, jnp.where(same & (ki >= li), one, zero), preferred_element_type=F32,
                   precision=HIGHEST)
    tot = jnp.dot(dta_t, jnp.where(same, one, zero), preferred_element_type=F32, precision=HIGHEST)
    nh = SSM_HEADS
    aux_ref[0:2 * nh, :] = dt_t[0:2 * nh, :]
    aux_ref[2 * nh:3 * nh, :] = cumf[0:nh, :]
    aux_ref[3 * nh:4 * nh, :] = cumb[nh:2 * nh, :]
    aux_ref[4 * nh:6 * nh, :] = tot[0:2 * nh, :]
    aux_ref[6 * nh:, :] = jnp.zeros((V7X_LANES - 6 * nh, tc), F32)
    hrow = lax.broadcasted_iota(I32, (V7X_LANES, tc), 0)
    acum_ref[...] = jnp.where(hrow < nh, cumf, cumb).T


def _conv(u, dt_raw, conv_w, conv_b, dt_bias, a_log, *, tc, seq_len, d_inner):
    t = u.shape[0]
    conv_dim = conv_w.shape[1]
    n_bc = SSM_GROUPS * SSM_STATE
    nt = t // tc
    hb = tc // CONV_HALO
    last_halo = t // CONV_HALO - 1
    cw = jnp.pad(conv_w.astype(F32), ((0, V7X_SUBLANES - SSM_CONV), (0, 0)))
    pad32 = lambda v: jnp.pad(v.astype(F32).reshape(1, -1), ((0, 0), (0, V7X_LANES - 2 * SSM_HEADS)))
    vmem = (2 * (_nbytes((tc, conv_dim), BF16) * 2 + _nbytes((tc, V7X_LANES), F32) * 3
                 + _nbytes((d_inner + 2 * n_bc, tc), BF16))
            + _nbytes((tc + 2 * CONV_HALO, conv_dim), F32) + 4 * _nbytes((tc, tc), F32))
    kern = functools.partial(_conv_kernel, tc=tc, tiles_per_seq=seq_len // tc, d_inner=d_inner)
    return pl.pallas_call(
        kern,
        out_shape=(jax.ShapeDtypeStruct((d_inner, t), BF16),
                   jax.ShapeDtypeStruct((t, 2 * n_bc), BF16),
                   jax.ShapeDtypeStruct((n_bc, t), BF16),
                   jax.ShapeDtypeStruct((V7X_LANES, t), F32),
                   jax.ShapeDtypeStruct((t, V7X_LANES), F32)),
        grid=(nt,),
        in_specs=[pl.BlockSpec((tc, conv_dim), lambda i: (i, U_XBC)),
                  pl.BlockSpec((CONV_HALO, conv_dim), lambda i: (jnp.maximum(i * hb - 1, 0), U_XBC)),
                  pl.BlockSpec((CONV_HALO, conv_dim), lambda i: (jnp.minimum((i + 1) * hb, last_halo), U_XBC)),
                  pl.BlockSpec((tc, V7X_LANES), lambda i: (i, 0)),
                  pl.BlockSpec((V7X_SUBLANES, conv_dim), lambda i: (0, 0)),
                  pl.BlockSpec((1, conv_dim), lambda i: (0, 0)),
                  pl.BlockSpec((1, V7X_LANES), lambda i: (0, 0)),
                  pl.BlockSpec((1, V7X_LANES), lambda i: (0, 0))],
        out_specs=(pl.BlockSpec((d_inner, tc), lambda i: (0, i)),
                   pl.BlockSpec((tc, 2 * n_bc), lambda i: (i, 0)),
                   pl.BlockSpec((n_bc, tc), lambda i: (0, i)),
                   pl.BlockSpec((V7X_LANES, tc), lambda i: (0, i)),
                   pl.BlockSpec((tc, V7X_LANES), lambda i: (i, 0))),
        scratch_shapes=[pltpu.VMEM((tc + 2 * CONV_HALO, conv_dim), F32)],
        compiler_params=_params(("parallel",), vmem),
        name="conv",
    )(u, u, u, dt_raw, cw, conv_b.astype(F32).reshape(1, -1), pad32(dt_bias), pad32(a_log))


def _ssd_direction(d, rev, xt_ref, bc_ref, ct_ref, aux_ref, acum_ref, st_scr, y_ref, dsk_ref):
    q = SSM_CHUNK
    hd = SSM_HEAD_DIM
    nh = SSM_HEADS
    si = lax.broadcasted_iota(I32, (q, q), 0)
    li = lax.broadcasted_iota(I32, (q, q), 1)
    mask = (li <= si) if rev else (li >= si)
    hpg = nh // SSM_GROUPS
    for g in range(SSM_GROUPS):
        b_g = bc_ref[:, g * SSM_STATE:(g + 1) * SSM_STATE]
        c_g = bc_ref[:, (SSM_GROUPS + g) * SSM_STATE:(SSM_GROUPS + g + 1) * SSM_STATE]
        ct_g = ct_ref[g * SSM_STATE:(g + 1) * SSM_STATE, :].astype(F32)
        cbt = lax.dot_general(b_g, c_g, (((1,), (1,)), ((), ())), preferred_element_type=F32)
        xdw_g, keep_g = [], []
        for hh in range(hpg):
            h = g * hpg + hh
            r = d * nh + h
            dt_row = aux_ref[r:r + 1, :]
            cum_row = aux_ref[2 * nh + r:2 * nh + r + 1, :]
            tot_row = aux_ref[4 * nh + r:4 * nh + r + 1, :]
            cum_col = acum_ref[:, r:r + 1]
            decay = jnp.exp(jnp.where(mask, cum_row - cum_col, -jnp.inf))
            w_top = (cbt * decay).astype(BF16)
            w_bot = (ct_g * jnp.exp(cum_row)).astype(BF16)
            w = jnp.concatenate([w_top, w_bot], axis=0)
            rows = slice(h * hd, (h + 1) * hd)
            x_t = xt_ref[rows, :].astype(F32)
            xd = x_t * dt_row
            s_old = st_scr[d, rows, :]
            lhs = jnp.concatenate([xd.astype(BF16), s_old.astype(BF16)], axis=1)
            y_t = jnp.dot(lhs, w, preferred_element_type=F32)
            if not rev:
                y_t = y_t + dsk_ref[h] * x_t
            y_ref[rows, :] = y_t.astype(y_ref.dtype)
            xdw_g.append((xd * jnp.exp(tot_row - cum_row)).astype(BF16))
            keep_g.append(s_old * jnp.exp(tot_row))
        grows = slice(g * hpg * hd, (g + 1) * hpg * hd)
        st_scr[d, grows, :] = jnp.concatenate(keep_g, axis=0) + jnp.dot(
            jnp.concatenate(xdw_g, axis=0), b_g, preferred_element_type=F32)


def _ssd_kernel(dsk_ref, xt_f, bc_f, ct_f, aux_f, ac_f, xt_b, bc_b, ct_b, aux_b, ac_b, init_ref,
                yf_ref, yb_ref, fin_ref, st_scr):
    c = pl.program_id(1)

    @pl.when(c == 0)
    def _():
        st_scr[...] = init_ref[...]

    _ssd_direction(0, False, xt_f, bc_f, ct_f, aux_f, ac_f, st_scr, yf_ref, dsk_ref)
    _ssd_direction(1, True, xt_b, bc_b, ct_b, aux_b, ac_b, st_scr, yb_ref, dsk_ref)

    @pl.when(c == pl.num_programs(1) - 1)
    def _():
        fin_ref[...] = st_scr[...]


def _ssd(conv_out, d_skip, init_state, *, batch, seq_len):
    xt, bc, ct, aux, acum = conv_out
    d_inner, t = xt.shape
    n_bc = ct.shape[0]
    q = SSM_CHUNK
    nc = seq_len // q
    fwd = lambda b, c: b * nc + c
    bwd = lambda b, c: b * nc + nc - 1 - c

    def specs(pos):
        return [pl.BlockSpec((d_inner, q), lambda b, c: (0, pos(b, c))),
                pl.BlockSpec((q, 2 * n_bc), lambda b, c: (pos(b, c), 0)),
                pl.BlockSpec((n_bc, q), lambda b, c: (0, pos(b, c))),
                pl.BlockSpec((V7X_LANES, q), lambda b, c: (0, pos(b, c))),
                pl.BlockSpec((q, V7X_LANES), lambda b, c: (pos(b, c), 0))]

    st_spec = pl.BlockSpec((None, 2, d_inner, SSM_STATE), lambda b, c: (b, 0, 0, 0))
    st_bytes = _nbytes((2, d_inner, SSM_STATE), F32)
    vmem = 5 * st_bytes + 4 * (_nbytes((d_inner, q), BF16) * 2 + _nbytes((q, 2 * n_bc), BF16)
                               + _nbytes((n_bc, q), BF16) + 2 * _nbytes((q, V7X_LANES), F32))
    return pl.pallas_call(
        _ssd_kernel,
        out_shape=(jax.ShapeDtypeStruct((d_inner, t), BF16),
                   jax.ShapeDtypeStruct((d_inner, t), BF16),
                   jax.ShapeDtypeStruct((batch, 2, d_inner, SSM_STATE), F32)),
        grid=(batch, nc),
        in_specs=[pl.BlockSpec(memory_space=pltpu.SMEM)] + specs(fwd) + specs(bwd) + [st_spec],
        out_specs=(pl.BlockSpec((d_inner, q), lambda b, c: (0, fwd(b, c))),
                   pl.BlockSpec((d_inner, q), lambda b, c: (0, bwd(b, c))),
                   st_spec),
        scratch_shapes=[pltpu.VMEM((2, d_inner, SSM_STATE), F32)],
        compiler_params=_params(("arbitrary", "arbitrary"), vmem),
        name="ssd",
    )(d_skip.astype(F32), xt, bc, ct, aux, acum, xt, bc, ct, aux, acum, init_state)


def _pool_body(slabs, cur_ref, mask_refs, icnt_ref, pw_ref, ps_ref, o_ref):
    gw = cur_ref.shape[1] // MIX_GROUPS
    for g in range(MIX_GROUPS):
        cols = slice(g * gw, (g + 1) * gw)
        xin = slabs(g, cols)
        s = jnp.dot(mask_refs[g][...], xin, preferred_element_type=F32)
        inv = icnt_ref[:, g * V7X_LANES:(g + 1) * V7X_LANES]
        p = s * jnp.concatenate([inv] * (gw // V7X_LANES), axis=1)
        dlt = p - cur_ref[:, cols].astype(F32)
        y = jnp.dot(dlt.astype(BF16), pw_ref[g], preferred_element_type=F32) * ps_ref[:, cols]
        o_ref[:, cols] = y.astype(BF16)


def _pool2d_kernel(prev_ref, cur_ref, next_ref, m0, m1, m2, m3, icnt_ref, pw_ref, ps_ref, o_ref):
    tile = cur_ref.shape[0]

    def slabs(g, cols):
        w = POOL_WINDOWS[g]
        up, dn = w // 2, w - w // 2 - 1
        parts = [prev_ref[tile - GRID_W * up:, cols], cur_ref[:, cols]]
        if dn:
            parts.append(next_ref[:GRID_W * dn, cols])
        return jnp.concatenate(parts, axis=0)

    _pool_body(slabs, cur_ref, (m0, m1, m2, m3), icnt_ref, pw_ref, ps_ref, o_ref)


def _pool1d_kernel(cur_ref, m0, m1, m2, m3, icnt_ref, pw_ref, ps_ref, o_ref):
    _pool_body(lambda g, cols: cur_ref[:, cols], cur_ref, (m0, m1, m2, m3), icnt_ref, pw_ref, ps_ref, o_ref)


def _window_ok(out_pos, in_pos, w):
    lo = out_pos[:, None] - w // 2
    return (in_pos[None, :] >= lo) & (in_pos[None, :] < lo + w)


def _pool_tables_2d(tile):
    rows = tile // GRID_W
    t = np.arange(tile)
    masks, invs = [], []
    for w in POOL_WINDOWS:
        up, dn = w // 2, w - w // 2 - 1
        u = np.arange(tile + GRID_W * (up + dn))
        r_in, c_in = u // GRID_W - up, u % GRID_W
        base = _window_ok(t // GRID_W, r_in, w) & _window_ok(t % GRID_W, c_in, w)
        var = np.stack([base & (r_in >= 0)[None, :], base, base & (r_in < rows)[None, :]])
        masks.append(jnp.asarray(var, dtype=BF16))
        invs.append(1.0 / np.sum(var, axis=2, dtype=np.float64))
    inv = np.stack(invs, axis=1)
    inv = np.broadcast_to(inv[:, :, :, None], inv.shape + (V7X_LANES,))
    inv = np.transpose(inv, (0, 2, 1, 3)).reshape(3, tile, MIX_GROUPS * V7X_LANES)
    return masks, jnp.asarray(inv, dtype=F32)


def _pool_tables_1d(n):
    t = np.arange(n)
    masks, invs = [], []
    for w in POOL_WINDOWS:
        m = _window_ok(t, t, w)
        masks.append(jnp.asarray(m, dtype=BF16))
        invs.append(1.0 / np.sum(m, axis=1, dtype=np.float64))
    inv = np.stack(invs, axis=0)
    inv = np.broadcast_to(inv[:, :, None], inv.shape + (V7X_LANES,))
    return masks, jnp.asarray(np.transpose(inv, (1, 0, 2)).reshape(n, MIX_GROUPS * V7X_LANES), dtype=F32)


def _pool2d(u, pool_w, pool_scale, *, seq_len):
    t = u.shape[0]
    d = pool_scale.shape[0]
    tile = TM_ROW
    nt, per_img = t // tile, seq_len // tile
    masks, inv = _pool_tables_2d(tile)

    def variant(i):
        p = i % per_img
        return jnp.where(p == 0, 0, jnp.where(p == per_img - 1, 2, 1))

    gw = d // MIX_GROUPS
    vmem = 2 * (4 * _nbytes((tile, d), BF16) + sum(_nbytes(m.shape[1:], BF16) for m in masks)
                + _nbytes(inv.shape[1:], F32) + _nbytes((MIX_GROUPS, gw, gw), BF16))
    return pl.pallas_call(
        _pool2d_kernel,
        out_shape=jax.ShapeDtypeStruct((t, d), BF16),
        grid=(nt,),
        in_specs=[pl.BlockSpec((tile, d), lambda i: (jnp.maximum(i - 1, 0), U_POOL)),
                  pl.BlockSpec((tile, d), lambda i: (i, U_POOL)),
                  pl.BlockSpec((tile, d), lambda i: (jnp.minimum(i + 1, nt - 1), U_POOL))]
                 + [pl.BlockSpec((None,) + m.shape[1:], lambda i: (variant(i), 0, 0)) for m in masks]
                 + [pl.BlockSpec((None,) + inv.shape[1:], lambda i: (variant(i), 0, 0)),
                    pl.BlockSpec((MIX_GROUPS, gw, gw), lambda i: (0, 0, 0)),
                    pl.BlockSpec((1, d), lambda i: (0, 0))],
        out_specs=pl.BlockSpec((tile, d), lambda i: (i, 0)),
        compiler_params=_params(("parallel",), vmem),
        name="pool2d",
    )(u, u, u, *masks, inv, pool_w.astype(BF16), pool_scale.astype(F32).reshape(1, d))


def _pool1d(u, pool_w, pool_scale, *, seq_len):
    t = u.shape[0]
    d = pool_scale.shape[0]
    masks, inv = _pool_tables_1d(seq_len)
    gw = d // MIX_GROUPS
    vmem = 2 * (2 * _nbytes((seq_len, d), BF16) + 4 * _nbytes((seq_len, seq_len), BF16)
                + _nbytes(inv.shape, F32) + _nbytes((MIX_GROUPS, gw, gw), BF16))
    return pl.pallas_call(
        _pool1d_kernel,
        out_shape=jax.ShapeDtypeStruct((t, d), BF16),
        grid=(t // seq_len,),
        in_specs=[pl.BlockSpec((seq_len, d), lambda i: (i, U_POOL))]
                 + [pl.BlockSpec(m.shape, lambda i: (0, 0)) for m in masks]
                 + [pl.BlockSpec(inv.shape, lambda i: (0, 0)),
                    pl.BlockSpec((MIX_GROUPS, gw, gw), lambda i: (0, 0, 0)),
                    pl.BlockSpec((1, d), lambda i: (0, 0))],
        out_specs=pl.BlockSpec((seq_len, d), lambda i: (i, 0)),
        compiler_params=_params(("parallel",), vmem),
        name="pool1d",
    )(u, *masks, inv, pool_w.astype(BF16), pool_scale.astype(F32).reshape(1, d))


def _dft_tables(n_pos, gw):
    n1, n2 = FOURIER_N1, n_pos // FOURIER_N1
    c = np.arange(gw)
    ang = 2.0 * np.pi * ((c[:, None] * c[None, :]) % gw) / gw
    cs = np.concatenate([np.cos(ang), -np.sin(ang)], axis=1)
    a = np.arange(n1)
    ang1 = 2.0 * np.pi * ((a[:, None] * a[None, :]) % n1) / n1
    wc, ws = np.cos(ang1), np.sin(ang1)
    w1 = np.block([[wc, ws], [-ws, wc]])
    k = a[:, None, None] + n1 * np.arange(n2)[None, :, None]
    ang2 = 2.0 * np.pi * ((k * np.arange(n2)[None, None, :]) % n_pos) / n_pos
    m2 = np.concatenate([np.cos(ang2), np.sin(ang2)], axis=2)
    to_bf = lambda v: jnp.asarray(v, dtype=F32).astype(BF16)
    return to_bf(cs), to_bf(w1), to_bf(m2)


def _f1_kernel(u_ref, cs_ref, w1_ref, o_ref, z_scr):
    nb, n1, d = u_ref.shape
    gw = d // MIX_GROUPS
    u = u_ref[...].reshape(nb * n1, d)
    for g in range(MIX_GROUPS):
        z = jnp.dot(u[:, g * gw:(g + 1) * gw], cs_ref[...], preferred_element_type=F32)
        z_scr[:, g * gw:(g + 1) * gw] = z[:, :gw].astype(BF16)
        z_scr[:, d + g * gw:d + (g + 1) * gw] = z[:, gw:].astype(BF16)
    for j in range(nb):
        zz = jnp.concatenate([z_scr[j * n1:(j + 1) * n1, :d], z_scr[j * n1:(j + 1) * n1, d:]], axis=0)
        o_ref[j] = jnp.dot(w1_ref[...], zz, preferred_element_type=F32).astype(BF16)


def _f2_kernel(a_ref, m2_ref, o_ref, *, scale):
    for j in range(a_ref.shape[0]):
        o_ref[j] = (jnp.dot(m2_ref[j], a_ref[j], preferred_element_type=F32) * scale).astype(BF16)


def _fourier_lat(u, *, batch, seq_len, d):
    n1, n2 = FOURIER_N1, seq_len // FOURIER_N1
    gw = d // MIX_GROUPS
    cs, w1, m2 = _dft_tables(seq_len, gw)
    ut = u[:, U_FOUR * d:(U_FOUR + 1) * d].reshape(batch, n1, n2, d).transpose(0, 2, 1, 3)
    nb = F1_BLOCK
    vmem1 = 2 * (_nbytes((nb, n1, d), BF16) + _nbytes((nb, 2 * n1, d), BF16)) + _nbytes((nb * n1, 2 * d), BF16) \
        + 2 * _nbytes((nb * n1, 2 * gw), F32)
    a = pl.pallas_call(
        _f1_kernel,
        out_shape=jax.ShapeDtypeStruct((batch, n2, 2 * n1, d), BF16),
        grid=(batch, n2 // nb),
        in_specs=[pl.BlockSpec((None, nb, n1, d), lambda b, j: (b, j, 0, 0)),
                  pl.BlockSpec(cs.shape, lambda b, j: (0, 0)),
                  pl.BlockSpec(w1.shape, lambda b, j: (0, 0))],
        out_specs=pl.BlockSpec((None, nb, 2 * n1, d), lambda b, j: (b, j, 0, 0)),
        scratch_shapes=[pltpu.VMEM((nb * n1, 2 * d), BF16)],
        compiler_params=_params(("parallel", "parallel"), vmem1),
        name="fourier_stage1",
    )(ut, cs, w1)
    at = a.reshape(batch, n2, 2, n1, d).transpose(0, 3, 2, 1, 4).reshape(batch, n1, 2 * n2, d)
    kb = F2_BLOCK
    vmem2 = 2 * (_nbytes((kb, 2 * n2, d), BF16) + _nbytes((kb, n2, 2 * n2), BF16) + _nbytes((kb, n2, d), BF16))
    x2 = pl.pallas_call(
        functools.partial(_f2_kernel, scale=1.0 / math.sqrt(seq_len * gw)),
        out_shape=jax.ShapeDtypeStruct((batch, n1, n2, d), BF16),
        grid=(batch, n1 // kb),
        in_specs=[pl.BlockSpec((None, kb, 2 * n2, d), lambda b, j: (b, j, 0, 0)),
                  pl.BlockSpec((kb, n2, 2 * n2), lambda b, j: (j, 0, 0))],
        out_specs=pl.BlockSpec((None, kb, n2, d), lambda b, j: (b, j, 0, 0)),
        compiler_params=_params(("parallel", "parallel"), vmem2),
        name="fourier_stage2",
    )(at, m2)
    return x2.transpose(0, 2, 1, 3).reshape(batch * seq_len, d)


def _fourier_ctx_kernel(u_ref, cs_ref, cn_ref, o_ref, *, scale):
    d = u_ref.shape[1]
    gw = d // MIX_GROUPS
    zr, zi = [], []
    for g in range(MIX_GROUPS):
        z = jnp.dot(u_ref[:, g * gw:(g + 1) * gw], cs_ref[...], preferred_element_type=F32)
        zr.append(z[:, :gw].astype(BF16))
        zi.append(z[:, gw:].astype(BF16))
    zz = jnp.concatenate([jnp.concatenate(zr, axis=1), jnp.concatenate(zi, axis=1)], axis=0)
    o_ref[...] = (jnp.dot(cn_ref[...], zz, preferred_element_type=F32) * scale).astype(BF16)


def _fourier_ctx(u, *, seq_len, d):
    t = u.shape[0]
    gw = d // MIX_GROUPS
    cs, _, _ = _dft_tables(FOURIER_N1 * 2, gw)
    n = np.arange(seq_len)
    ang = 2.0 * np.pi * ((n[:, None] * n[None, :]) % seq_len) / seq_len
    cn = jnp.asarray(np.concatenate([np.cos(ang), np.sin(ang)], axis=1), dtype=F32).astype(BF16)
    vmem = 2 * (2 * _nbytes((seq_len, d), BF16) + _nbytes(cs.shape, BF16) + _nbytes(cn.shape, BF16)) \
        + 4 * _nbytes((seq_len, d), F32)
    return pl.pallas_call(
        functools.partial(_fourier_ctx_kernel, scale=1.0 / math.sqrt(seq_len * gw)),
        out_shape=jax.ShapeDtypeStruct((t, d), BF16),
        grid=(t // seq_len,),
        in_specs=[pl.BlockSpec((seq_len, d), lambda i: (i, U_FOUR)),
                  pl.BlockSpec(cs.shape, lambda i: (0, 0)),
                  pl.BlockSpec(cn.shape, lambda i: (0, 0))],
        out_specs=pl.BlockSpec((seq_len, d), lambda i: (i, 0)),
        compiler_params=_params(("parallel",), vmem),
        name="fourier_ctx",
    )(u, cs, cn)


def _merge_kernel(ya_ref, yb_ref, yf_ref, ybw_ref, z_ref, g0_ref, g1_ref, g2_ref, lat_ref, mod_ref,
                  snw_ref, wb_ref, wo_ref, n2w_ref, rw_ref, rb_ref, tri_ref, ones_ref, cnt0_ref,
                  lat_o, h2_o, idx_o, rank_o, wts_o, cnt_o, run_scr):
    tm = lat_ref.shape[0]

    @pl.when(pl.program_id(0) == 0)
    def _():
        run_scr[...] = cnt0_ref[...]

    y_ssm = (yf_ref[...].astype(F32) + ybw_ref[...].astype(F32)).T
    yc = _rms(y_ssm * _silu(z_ref[...].astype(F32)), snw_ref[...])
    merged = jnp.zeros(lat_ref.shape, F32)
    for k, (y, g_ref) in enumerate(((ya_ref[...], g0_ref), (yb_ref[...], g1_ref), (yc.astype(BF16), g2_ref))):
        proj = jnp.dot(y, wb_ref[k], preferred_element_type=F32)
        merged = merged + jax.nn.sigmoid(g_ref[...].astype(F32)) * proj
    ol = jnp.dot(merged.astype(BF16), wo_ref[...], preferred_element_type=F32)
    lat = lat_ref[...] + mod_ref[2:3, :] * ol
    lat_o[...] = lat
    h2 = _rms(lat, n2w_ref[...]) * (1.0 + mod_ref[4:5, :]) + mod_ref[3:4, :]
    _store_tile_rows(h2_o, h2)

    h_hi = h2.astype(BF16)
    h_lo = (h2 - h_hi.astype(F32)).astype(BF16)
    nt_dims = (((1,), (1,)), ((), ()))
    lt = (lax.dot_general(rw_ref[0], h_hi, nt_dims, preferred_element_type=F32)
          + lax.dot_general(rw_ref[0], h_lo, nt_dims, preferred_element_type=F32)
          + lax.dot_general(rw_ref[1], h_hi, nt_dims, preferred_element_type=F32))
    lt = lt + jnp.concatenate([rb_ref[...]] * (tm // V7X_LANES), axis=1)
    ei = lax.broadcasted_iota(I32, lt.shape, 0).astype(F32)
    vals, idxs = [], []
    for _ in range(TOP_K):
        m = jnp.max(lt, axis=0, keepdims=True)
        sel = jnp.min(jnp.where(lt == m, ei, float(V7X_LANES)), axis=0, keepdims=True)
        vals.append(m)
        idxs.append(sel)
        lt = jnp.where(ei == sel, -jnp.inf, lt)
    exps = [jnp.exp(v - vals[0]) for v in vals]
    den = exps[0] + exps[1] + exps[2] + exps[3]
    onehots = [ei == s for s in idxs]
    chosen = jnp.zeros(lt.shape, F32)
    for oh in onehots:
        chosen = chosen + jnp.where(oh, 1.0, 0.0)
    chosen_b = chosen.astype(BF16)
    before = jnp.dot(chosen_b, tri_ref[...], preferred_element_type=F32) \
        + jnp.concatenate([run_scr[...]] * (tm // V7X_LANES), axis=1)
    for k in range(TOP_K):
        idx_o[k:k + 1, :] = idxs[k].astype(I32)
        rank_o[k:k + 1, :] = jnp.sum(jnp.where(onehots[k], before, 0.0), axis=0, keepdims=True).astype(I32)
        wts_o[k:k + 1, :] = exps[k] / den
    pad = V7X_SUBLANES - TOP_K
    idx_o[TOP_K:, :] = jnp.zeros((pad, tm), I32)
    rank_o[TOP_K:, :] = jnp.zeros((pad, tm), I32)
    wts_o[TOP_K:, :] = jnp.zeros((pad, tm), F32)
    run_scr[...] = run_scr[...] + jnp.dot(chosen_b, ones_ref[...], preferred_element_type=F32)
    cnt_o[...] = run_scr[...]


def _merge(ya, yb, yf_t, yb_t, u, lat, modv, ssm_norm_w, wb, wo, norm2_w, rw, rb, cnt0, *,
           tm, seg0, tiles_per_seg):
    t, d = lat.shape
    ne = V7X_LANES
    tri = jnp.asarray(np.arange(tm)[:, None] < np.arange(tm)[None, :], dtype=BF16)
    ones = jnp.ones((tm, ne), BF16)
    row = lambda c: pl.BlockSpec((tm, d), lambda i: (i, c))
    col = pl.BlockSpec((d, tm), lambda i: (0, i))
    const = lambda shape: pl.BlockSpec(shape, lambda i: (0,) * len(shape), pipeline_mode=pl.Buffered(1))
    tok = pl.BlockSpec((V7X_SUBLANES, tm), lambda i: (0, i))
    vmem = (2 * (4 * _nbytes((tm, d), BF16) + 2 * _nbytes((d, tm), BF16) + 2 * _nbytes((tm, d), BF16)
                 + 3 * _nbytes((tm, d), F32))
            + 4 * _nbytes((d, d), BF16) + _nbytes((2, ne, d), BF16) + _nbytes((tm, tm), BF16)
            + 8 * _nbytes((tm, d), F32))
    return pl.pallas_call(
        _merge_kernel,
        out_shape=(jax.ShapeDtypeStruct((t, d), F32), jax.ShapeDtypeStruct((t * V7X_SUBLANES, V7X_LANES), F32),
                   jax.ShapeDtypeStruct((V7X_SUBLANES, t), I32), jax.ShapeDtypeStruct((V7X_SUBLANES, t), I32),
                   jax.ShapeDtypeStruct((V7X_SUBLANES, t), F32), jax.ShapeDtypeStruct((ne, ne), F32)),
        grid=(t // tm,),
        in_specs=[row(0), row(0), col, col, row(U_Z), row(U_GATE), row(U_GATE + 1), row(U_GATE + 2),
                  row(0),
                  pl.BlockSpec((None, V7X_SUBLANES, d), lambda i: (seg0 + i // tiles_per_seg, 0, 0)),
                  const((1, d)), const((N_BRANCH, d, d)), const((d, d)), const((1, d)),
                  const((2, ne, d)), const((ne, ne)), const((tm, tm)), const((tm, ne)), const((ne, ne))],
        out_specs=(row(0), pl.BlockSpec((tm * V7X_SUBLANES, V7X_LANES), lambda i: (i, 0)), tok, tok, tok,
                   pl.BlockSpec((ne, ne), lambda i: (0, 0))),
        scratch_shapes=[pltpu.VMEM((ne, ne), F32)],
        compiler_params=_params(("arbitrary",), vmem),
        name="merge",
    )(ya, yb, yf_t, yb_t, u, u, u, u, lat, modv, ssm_norm_w.astype(F32).reshape(1, d), wb, wo,
      norm2_w.astype(F32).reshape(1, d), rw, rb, tri, ones, cnt0)


def _tile_row(r):
    return pl.ds(pl.multiple_of(r * V7X_SUBLANES, V7X_SUBLANES), V7X_SUBLANES)


def _dispatch_kernel(dest_ref, h_ref, *rest):
    xs_ref, sem = rest[-2], rest[-1]
    rows = h_ref.shape[0]
    n = rows // V7X_SUBLANES

    def start(t, carry):
        for k in range(TOP_K):
            pltpu.make_async_copy(h_ref.at[_tile_row(t), :], xs_ref.at[_tile_row(dest_ref[0, TOP_K * t + k]), :],
                                  sem).start(priority=k % DMA_PRIORITIES)
        return carry

    lax.fori_loop(0, n, start, 0, unroll=DMA_UNROLL)
    for k in range(TOP_K):
        pltpu.make_async_copy(h_ref, xs_ref.at[pl.ds(0, rows), :], sem).wait()


def _dispatch(h2, dest_tiles, n_slots, xs_prev=None):
    t, d = h2.shape[0] // V7X_SUBLANES, V7X_LANES
    td = min(T_DISPATCH, t)
    in_specs = [pl.BlockSpec((None, 1, TOP_K * td), lambda i: (i, 0, 0), memory_space=pltpu.SMEM),
                pl.BlockSpec((td * V7X_SUBLANES, d), lambda i: (i, 0))]
    args = [dest_tiles.reshape(t // td, 1, TOP_K * td), h2]
    aliases = {}
    if xs_prev is not None:
        in_specs.append(pl.BlockSpec(memory_space=pl.ANY))
        args.append(xs_prev)
        aliases = {2: 0}
    return pl.pallas_call(
        _dispatch_kernel,
        out_shape=jax.ShapeDtypeStruct((n_slots * V7X_SUBLANES, d), F32),
        grid=(t // td,),
        in_specs=in_specs,
        out_specs=pl.BlockSpec(memory_space=pl.ANY),
        scratch_shapes=[pltpu.SemaphoreType.DMA(())],
        input_output_aliases=aliases,
        compiler_params=_params(("arbitrary",), 2 * _nbytes((td * V7X_SUBLANES, d), F32)),
        name="dispatch",
    )(*args)


EXPERT_NB = 256
EXPERT_ROW_STEP = 256


def _expert_kernel(be_ref, nv_ref, nxt_ref, xs_ref, w1_hbm, b1_ref, w2_hbm, b2_ref, ys_ref,
                   w1_stage, w2_stage, w1_scr, w2_scr, act_scr, sems, *, layer):
    i = pl.program_id(0)
    d, dff = w1_stage.shape[0], w2_stage.shape[0]
    e = be_ref[i]
    changed = (i == 0) | (e != be_ref[jnp.maximum(i - 1, 0)])

    def weight_copies(ex):
        return (pltpu.make_async_copy(w1_hbm.at[layer, ex], w1_stage, sems.at[0]),
                pltpu.make_async_copy(w2_hbm.at[layer, ex], w2_stage, sems.at[1]))

    @pl.when(i == 0)
    def _():
        for cp in weight_copies(e):
            cp.start()

    @pl.when(changed)
    def _():
        for cp in weight_copies(e):
            cp.wait()
        rows = 128

        def cast1(r, carry):
            sl = pl.ds(pl.multiple_of(r * rows, rows), rows)
            w1_scr[sl, :] = w1_stage[sl, :].astype(BF16)
            return carry

        def cast2(r, carry):
            sl = pl.ds(pl.multiple_of(r * rows, rows), rows)
            w2_scr[sl, :] = w2_stage[sl, :].astype(BF16)
            return carry

        lax.fori_loop(0, d // rows, cast1, 0)
        lax.fori_loop(0, dff // rows, cast2, 0)
        nxt = nxt_ref[i]

        @pl.when(nxt >= 0)
        def _():
            for cp in weight_copies(nxt):
                cp.start()

    nv = nv_ref[i]

    def mlp(rows):
        xs_v = xs_ref.at[pl.ds(0, rows * V7X_SUBLANES), :]
        ys_v = ys_ref.at[pl.ds(0, rows * V7X_SUBLANES), :]
        x = jnp.concatenate(_load_tile_rows(xs_v, rows), axis=1)
        ridx = lax.broadcasted_iota(I32, x.shape, 0)
        x = jnp.where(ridx < nv, x, 0.0).astype(BF16)
        for c in range(dff // EXPERT_NB):
            cg = slice(c * EXPERT_NB, (c + 1) * EXPERT_NB)
            cl = slice(dff + c * EXPERT_NB, dff + (c + 1) * EXPERT_NB)
            gate = jnp.dot(x, w1_scr[:, cg], preferred_element_type=F32) + b1_ref[:, cg]
            lin = jnp.dot(x, w1_scr[:, cl], preferred_element_type=F32) + b1_ref[:, cl]
            gate = jnp.minimum(gate, SWIGLU_LIMIT)
            lin = jnp.clip(lin, -SWIGLU_LIMIT, SWIGLU_LIMIT)
            act = gate * jax.nn.sigmoid(SWIGLU_ALPHA * gate) * (lin + 1.0)
            act_scr[0:rows, cg] = act.astype(BF16)
        _store_tile_rows(ys_v, jnp.dot(act_scr[0:rows, :], w2_scr[...], preferred_element_type=F32) + b2_ref[...])

    def zero_from(row0):
        tail = ys_ref.at[pl.ds(row0 * V7X_SUBLANES, (MOE_BLOCK - row0) * V7X_SUBLANES), :]
        tail[...] = jnp.zeros(tail.shape, F32)

    @pl.when(nv == 0)
    def _():
        zero_from(0)

    for rows in range(EXPERT_ROW_STEP, MOE_BLOCK + 1, EXPERT_ROW_STEP):
        @pl.when((nv > rows - EXPERT_ROW_STEP) & (nv <= rows))
        def _(rows=rows):
            mlp(rows)
            if rows < MOE_BLOCK:
                zero_from(rows)


def _experts(xs, block_e, nvalid, next_e, w1, b1, w2, b2, *, layer):
    n_slots = xs.shape[0] // V7X_SUBLANES
    depth, ne, d, two_dff = w1.shape
    dff = two_dff // 2
    n_blocks = n_slots // MOE_BLOCK
    blk = pl.BlockSpec((MOE_BLOCK * V7X_SUBLANES, V7X_LANES), lambda i, be, nv, nx: (i, 0))
    grid_spec = pltpu.PrefetchScalarGridSpec(
        num_scalar_prefetch=3,
        grid=(n_blocks,),
        in_specs=[blk,
                  pl.BlockSpec(memory_space=pl.ANY),
                  pl.BlockSpec((None, None, 1, two_dff), lambda i, be, nv, nx: (layer, be[i], 0, 0)),
                  pl.BlockSpec(memory_space=pl.ANY),
                  pl.BlockSpec((None, None, 1, d), lambda i, be, nv, nx: (layer, be[i], 0, 0))],
        out_specs=blk,
        scratch_shapes=[pltpu.VMEM((d, two_dff), F32), pltpu.VMEM((dff, d), F32),
                        pltpu.VMEM((d, two_dff), BF16), pltpu.VMEM((dff, d), BF16),
                        pltpu.VMEM((MOE_BLOCK, dff), BF16), pltpu.SemaphoreType.DMA((2,))],
    )
    vmem = (_nbytes((d, two_dff), F32) + _nbytes((dff, d), F32) + 4 * _nbytes((MOE_BLOCK, d), F32)
            + _nbytes((d, two_dff), BF16) + _nbytes((dff, d), BF16) + 6 * _nbytes((MOE_BLOCK, dff), F32))
    return pl.pallas_call(
        functools.partial(_expert_kernel, layer=layer),
        out_shape=jax.ShapeDtypeStruct(xs.shape, F32),
        grid_spec=grid_spec,
        compiler_params=_params(("arbitrary",), vmem),
        name="experts",
    )(block_e, nvalid, next_e, xs, w1, b1.reshape(depth, ne, 1, two_dff), w2, b2.reshape(depth, ne, 1, d))


def _combine_kernel(dest_ref, dnext_ref, lat_ref, w_ref, mod_ref, fnw_ref, ys_ref, o_ref, buf, sems, *,
                    final_norm):
    i = pl.program_id(0)
    n = lat_ref.shape[0]
    slot = i % 2

    def issue(d_ref, s):
        def start(t, carry):
            for k in range(TOP_K):
                pltpu.make_async_copy(ys_ref.at[_tile_row(d_ref[0, TOP_K * t + k]), :],
                                      buf.at[s, k, _tile_row(t), :], sems.at[s]).start(priority=k % DMA_PRIORITIES)
            return carry

        lax.fori_loop(0, n, start, 0, unroll=DMA_UNROLL)

    @pl.when(i == 0)
    def _():
        issue(dest_ref, slot)

    @pl.when(i + 1 < pl.num_programs(0))
    def _():
        issue(dnext_ref, 1 - slot)

    for k in range(TOP_K):
        pltpu.make_async_copy(ys_ref.at[pl.ds(0, n * V7X_SUBLANES), :], buf.at[slot, k], sems.at[slot]).wait()
    parts = [jnp.zeros((n, V7X_LANES), F32)] * V7X_SUBLANES
    for k in range(TOP_K):
        wk = w_ref[:, k:k + 1]
        rows = _load_tile_rows(buf.at[slot, k], n)
        parts = [p + wk * r for p, r in zip(parts, rows)]
    out = lat_ref[...] + mod_ref[5:6, :] * jnp.concatenate(parts, axis=1)
    if final_norm:
        out = _rms(out, fnw_ref[...])
    o_ref[...] = out


def _combine(lat, dest_tiles, wts, modv, fnw, ys, *, seg0, tiles_per_seg, final_norm):
    t, d = lat.shape
    td = T_DMA
    nt = t // td
    vmem = 2 * (2 * _nbytes((td, d), F32) + _nbytes((td, V7X_LANES), F32)) + 2 * _nbytes((TOP_K, td, d), F32) \
        + 4 * _nbytes((td, d), F32)
    return pl.pallas_call(
        functools.partial(_combine_kernel, final_norm=final_norm),
        out_shape=jax.ShapeDtypeStruct((t, d), F32),
        grid=(nt,),
        in_specs=[pl.BlockSpec((None, 1, TOP_K * td), lambda i: (i, 0, 0), memory_space=pltpu.SMEM),
                  pl.BlockSpec((None, 1, TOP_K * td), lambda i: (jnp.minimum(i + 1, nt - 1), 0, 0),
                               memory_space=pltpu.SMEM),
                  pl.BlockSpec((td, d), lambda i: (i, 0)),
                  pl.BlockSpec((td, V7X_SUBLANES), lambda i: (i, 0)),
                  pl.BlockSpec((None, V7X_SUBLANES, d), lambda i: (seg0 + i // tiles_per_seg, 0, 0)),
                  pl.BlockSpec((1, d), lambda i: (0, 0)),
                  pl.BlockSpec(memory_space=pl.ANY)],
        out_specs=pl.BlockSpec((td, d), lambda i: (i, 0)),
        scratch_shapes=[pltpu.VMEM((2, TOP_K, td * V7X_SUBLANES, V7X_LANES), F32),
                        pltpu.SemaphoreType.DMA((2,))],
        compiler_params=_params(("arbitrary",), vmem),
        name="combine",
    )(dest_tiles, dest_tiles, lat, wts, modv, fnw.astype(F32).reshape(1, d), ys)


def _routing_tables(counts, n_blocks):
    padded = (counts + MOE_BLOCK - 1) // MOE_BLOCK * MOE_BLOCK
    pends = jnp.cumsum(padded)
    pstarts = pends - padded
    bstart = jnp.arange(n_blocks, dtype=I32) * MOE_BLOCK
    block_e = jnp.minimum(jnp.sum((pends[None, :] <= bstart[:, None]).astype(I32), axis=1), N_EXPERTS - 1)
    nvalid = jnp.clip(counts[block_e] - (bstart - pstarts[block_e]), 0, MOE_BLOCK)
    nvalid = jnp.where(bstart < pends[-1], nvalid, 0)
    blk = jnp.arange(n_blocks, dtype=I32)
    is_change = jnp.concatenate([jnp.zeros((1,), bool), block_e[1:] != block_e[:-1]])
    change_at = jnp.where(is_change, blk, n_blocks)
    next_change = lax.cummin(change_at, axis=0, reverse=True)
    after = jnp.concatenate([next_change[1:], jnp.full((1,), n_blocks, I32)])
    next_e = jnp.where(after < n_blocks, block_e[jnp.minimum(after, n_blocks - 1)], -1)
    return pstarts, block_e.astype(I32), nvalid.astype(I32), next_e.astype(I32)


def _dest_tiles(idx, rank, pstarts):
    e = idx[:TOP_K]
    start = jnp.sum(jnp.where(e[:, :, None] == jnp.arange(N_EXPERTS)[None, None, :], pstarts[None, None, :], 0),
                    axis=2)
    dest = (start + rank[:TOP_K]).astype(I32).T
    return dest.reshape(-1, 1, TOP_K * T_DMA)


def kernel(x, c, ctx, c_ctx, w_ada, b_ada, norm1_w, norm2_w, w_in, pool_w, pool_scale, conv_w, conv_b,
           dt_bias, a_log, d_skip, ssm_norm_w, w_branch, w_out, router_w, router_b, moe_w1, moe_b1,
           moe_w2, moe_b2, final_norm_w):
    batch, n, d = x.shape
    n_ctx = ctx.shape[1]
    depth = w_ada.shape[0]
    d_inner = SSM_HEADS * SSM_HEAD_DIM
    conv_dim = conv_w.shape[2]
    off_four, off_z, off_xbc = d, 2 * d, 3 * d
    off_dt = off_xbc + conv_dim
    off_gate = off_dt + 2 * SSM_HEADS
    assert batch + 1 <= V7X_SUBLANES and d_inner == d and conv_dim == 2 * d
    assert d == V7X_SUBLANES * V7X_LANES

    lat = x.reshape(batch * n, d).astype(F32)
    cx = ctx.reshape(batch * n_ctx, d).astype(F32)
    cvec = jnp.concatenate([c.astype(F32), c_ctx.astype(F32)[None, :],
                            jnp.zeros((V7X_SUBLANES - batch - 1, d), F32)], axis=0)
    mods = _ada(cvec, w_ada.astype(F32), b_ada.astype(F32))
    seg_ctx = batch
    zero_state = jnp.zeros((batch, 2, d_inner, SSM_STATE), F32)
    zero_cnt = jnp.zeros((V7X_LANES, V7X_LANES), F32)

    for i in range(depth):
        last = i == depth - 1
        modv = mods[i].reshape(V7X_SUBLANES, N_ADA, d)[:batch + 1]
        modv = jnp.pad(modv, ((0, 0), (0, V7X_SUBLANES - N_ADA), (0, 0)))
        wi = w_in[i]
        w_main = jnp.concatenate([wi[:, off_xbc:off_dt], wi[:, :off_z], wi[:, off_z:off_xbc], wi[:, off_gate:]],
                                 axis=1).astype(BF16)
        w_dt = jnp.pad(wi[:, off_dt:off_gate], ((0, 0), (0, V7X_LANES - 2 * SSM_HEADS))).astype(BF16)

        u_l, dtr_l = _inproj(lat, modv, norm1_w[i], w_main, w_dt, tm=TM_LAT, seg0=0, tiles_per_seg=n // TM_LAT)
        u_c, dtr_c = _inproj(cx, modv, norm1_w[i], w_main, w_dt, tm=batch * n_ctx, seg0=seg_ctx, tiles_per_seg=1)

        conv_args = (conv_w[i], conv_b[i], dt_bias[i], a_log[i])
        cv_c = _conv(u_c, dtr_c, *conv_args, tc=n_ctx, seq_len=n_ctx, d_inner=d_inner)
        cv_l = _conv(u_l, dtr_l, *conv_args, tc=TM_ROW, seq_len=n, d_inner=d_inner)
        yf_c, yb_c, st = _ssd(cv_c, d_skip[i], zero_state, batch=batch, seq_len=n_ctx)
        yf_l, yb_l, _ = _ssd(cv_l, d_skip[i], st, batch=batch, seq_len=n)

        ya_l = _pool2d(u_l, pool_w[i], pool_scale[i], seq_len=n)
        fb_l = _fourier_lat(u_l, batch=batch, seq_len=n, d=d)

        wb = w_branch[i].astype(BF16)
        wo = w_out[i].astype(BF16)
        rw_t = jnp.pad(router_w[i].astype(F32).T, ((0, V7X_LANES - N_EXPERTS), (0, 0)))
        rw_hi = rw_t.astype(BF16)
        rw = jnp.stack([rw_hi, (rw_t - rw_hi.astype(F32)).astype(BF16)])
        rb = jnp.pad(router_b[i].astype(F32), (0, V7X_LANES - N_EXPERTS), constant_values=-1e30)
        rb = jnp.broadcast_to(rb[:, None], (V7X_LANES, V7X_LANES))
        merge_w = (ssm_norm_w[i], wb, wo, norm2_w[i], rw, rb)
        lat_m, h2_l, idx_l, rank_l, wts_l, cnt = _merge(
            ya_l, fb_l, yf_l, yb_l, u_l, lat, modv, *merge_w, zero_cnt,
            tm=TM_ROW, seg0=0, tiles_per_seg=n // TM_ROW)
        n_tok = batch * n
        if not last:
            ya_c = _pool1d(u_c, pool_w[i], pool_scale[i], seq_len=n_ctx)
            fb_c = _fourier_ctx(u_c, seq_len=n_ctx, d=d)
            cx_m, h2_c, idx_c, rank_c, wts_c, cnt = _merge(
                ya_c, fb_c, yf_c, yb_c, u_c, cx, modv, *merge_w, cnt,
                tm=batch * n_ctx, seg0=seg_ctx, tiles_per_seg=1)
            n_tok += batch * n_ctx

        n_blocks = -(-(n_tok * TOP_K) // MOE_BLOCK) + N_EXPERTS
        n_slots = n_blocks * MOE_BLOCK
        counts = cnt[:N_EXPERTS, 0].astype(I32)
        pstarts, block_e, nvalid, next_e = _routing_tables(counts, n_blocks)
        dest_l = _dest_tiles(idx_l, rank_l, pstarts)
        xs = _dispatch(h2_l, dest_l, n_slots)
        if not last:
            dest_c = _dest_tiles(idx_c, rank_c, pstarts)
            xs = _dispatch(h2_c, dest_c, n_slots, xs_prev=xs)
        ys = _experts(xs, block_e, nvalid, next_e, moe_w1, moe_b1, moe_w2, moe_b2, layer=i)
        lat = _combine(lat_m, dest_l, wts_l.T, modv, final_norm_w, ys, seg0=0, tiles_per_seg=n // T_DMA,
                       final_norm=last)
        if not last:
            cx = _combine(cx_m, dest_c, wts_c.T, modv, final_norm_w, ys, seg0=seg_ctx,
                          tiles_per_seg=batch * n_ctx // T_DMA, final_norm=False)
    return lat.reshape(batch, n, d).astype(x.dtype)
```
